```python
import math
import jax, jax.numpy as jnp
from jax import lax
import numpy as np

D_MODEL = 1024
BATCH = 8
SEQ = 4096
DEPTH = 2

N_MIXERS = 2
NORM_EPS = 1e-6
ADA_STD = 0.5
MLA_HEADS = 8
Q_LORA = 256
KV_LORA = 256
QK_NOPE = 128
QK_ROPE = 64
V_HEAD = 128
ROPE_THETA = 10000.0
Q_BLOCK = 128
MLA_SCALE = (QK_NOPE + QK_ROPE) ** -0.5
MLA_IN_DIM = Q_LORA + KV_LORA + QK_ROPE
SSM_INNER = 2 * D_MODEL
SSM_HEAD_DIM = 64
SSM_HEADS = SSM_INNER // SSM_HEAD_DIM
SSM_GROUPS = 8
SSM_HPG = SSM_HEADS // SSM_GROUPS
SSM_STATE = 128
SSM_CONV = 4
SSM_CONV_DIM = SSM_INNER + 2 * SSM_GROUPS * SSM_STATE
SSM_IN_DIM = SSM_INNER + SSM_CONV_DIM + SSM_HEADS
SSM_CHUNK = 128
N_EXPERTS = 32
TOP_K = 4
EXPERT_FF = D_MODEL
SWIGLU_LIMIT = 7.0
SWIGLU_ALPHA = 1.702
MOE_BLOCK = 256

kernel_name = "hybrid_mla_ssd_moe_adaln"


def rmsnorm(x, g):
    xf = x.astype(jnp.float32)
    xf = xf * lax.rsqrt(jnp.mean(xf * xf, axis=-1, keepdims=True) + NORM_EPS)
    return (xf * g.astype(jnp.float32)).astype(x.dtype)


def rope_cos_sin(positions):
    inv_freq = 1.0 / (ROPE_THETA ** (jnp.arange(0, QK_ROPE, 2, dtype=jnp.float32) / QK_ROPE))
    ang = positions.astype(jnp.float32)[..., None] * inv_freq
    return jnp.cos(ang), jnp.sin(ang)


def apply_rope(x, cos, sin):
    x1, x2 = jnp.split(x, 2, axis=-1)
    return jnp.concatenate([x1 * cos - x2 * sin, x1 * sin + x2 * cos], axis=-1).astype(x.dtype)


def mla_mixer(h, positions, w_in, g_q, g_kv, w_q_up, w_kv_up, w_out):
    bsz, seqlen, _ = h.shape
    lat = h @ w_in
    q_lat = rmsnorm(lat[..., :Q_LORA], g_q)
    kv_lat = rmsnorm(lat[..., Q_LORA:Q_LORA + KV_LORA], g_kv)
    cos, sin = rope_cos_sin(positions)
    k_rope = apply_rope(lat[..., Q_LORA + KV_LORA:], cos, sin)
    q = (q_lat @ w_q_up).reshape(bsz, seqlen, MLA_HEADS, QK_NOPE + QK_ROPE)
    q_nope = q[..., :QK_NOPE]
    q_rope = apply_rope(q[..., QK_NOPE:], cos[:, :, None], sin[:, :, None])
    kv = (kv_lat @ w_kv_up).reshape(bsz, seqlen, MLA_HEADS, QK_NOPE + V_HEAD)
    k_nope = kv[..., :QK_NOPE]
    v = kv[..., QK_NOPE:]
    outs = []
    for blk in range(seqlen // Q_BLOCK):
        q0 = blk * Q_BLOCK
        q1 = q0 + Q_BLOCK
        s = (jnp.einsum('bqhd,bkhd->bhqk', q_nope[:, q0:q1], k_nope[:, :q1])
             + jnp.einsum('bqhr,bkr->bhqk', q_rope[:, q0:q1], k_rope[:, :q1]))
        s = s.astype(jnp.float32) * MLA_SCALE
        causal = (q0 + jnp.arange(Q_BLOCK))[:, None] >= jnp.arange(q1)[None, :]
        p = jax.nn.softmax(jnp.where(causal, s, -jnp.inf), axis=-1).astype(v.dtype)
        outs.append(jnp.einsum('bhqk,bkhd->bqhd', p, v[:, :q1]))
    o = jnp.concatenate(outs, axis=1).reshape(bsz, seqlen, MLA_HEADS * V_HEAD)
    return o @ w_out


def causal_depthwise_conv(u, w, b):
    k = w.shape[0]
    out = lax.conv_general_dilated(u, w[:, None, :].astype(u.dtype), window_strides=(1,),
                                   padding=[(k - 1, 0)], dimension_numbers=('NWC', 'WIO', 'NWC'),
                                   feature_group_count=u.shape[-1])
    return out + b


def ssd_chunked(x, a_dt, bm, cm):
    bsz, seqlen, ng, nr, hp = x.shape
    nc = seqlen // SSM_CHUNK
    x = x.reshape(bsz, nc, SSM_CHUNK, ng, nr, hp)
    bm = bm.reshape(bsz, nc, SSM_CHUNK, ng, SSM_STATE)
    cm = cm.reshape(bsz, nc, SSM_CHUNK, ng, SSM_STATE)
    a = a_dt.reshape(bsz, nc, SSM_CHUNK, ng, nr).transpose(0, 3, 4, 1, 2)
    a_cum = jnp.cumsum(a, axis=-1)
    seg = a_cum[..., :, None] - a_cum[..., None, :]
    causal = jnp.tril(jnp.ones((SSM_CHUNK, SSM_CHUNK), dtype=bool))
    decay = jnp.exp(jnp.where(causal, seg, -jnp.inf))
    cb = jnp.einsum('bclgn,bcsgn->bgcls', cm, bm)
    y_diag = jnp.einsum('bgrcls,bcsgrp->bclgrp', (cb[:, :, None] * decay).astype(x.dtype), x)
    decay_to_end = jnp.exp(a_cum[..., -1:] - a_cum).astype(x.dtype)
    states = jnp.einsum('bclgn,bgrcl,bclgrp->bcgrpn', bm, decay_to_end, x)
    chunk_decay = jnp.exp(a_cum[..., -1]).astype(x.dtype)

    def step(carry, inp):
        st, dec = inp
        return carry * dec[..., None, None] + st, carry

    init = jnp.zeros((bsz, ng, nr, hp, SSM_STATE), x.dtype)
    _, prev = lax.scan(step, init, (states.transpose(1, 0, 2, 3, 4, 5), chunk_decay.transpose(3, 0, 1, 2)))
    prev = prev.transpose(1, 0, 2, 3, 4, 5)
    y_off = jnp.einsum('bclgn,bcgrpn,bgrcl->bclgrp', cm, prev, jnp.exp(a_cum).astype(x.dtype))
    return (y_diag + y_off).reshape(bsz, seqlen, ng, nr, hp)


def gated_group_rmsnorm(y, z, g):
    yf = (y * jax.nn.silu(z)).astype(jnp.float32)
    yf = yf.reshape(*y.shape[:-1], SSM_GROUPS, SSM_INNER // SSM_GROUPS)
    yf = yf * lax.rsqrt(jnp.mean(yf * yf, axis=-1, keepdims=True) + NORM_EPS)
    return (yf.reshape(y.shape) * g.astype(jnp.float32)).astype(y.dtype)


def ssd_mixer(h, w_in, conv_w, conv_b, dt_bias, a_log, d_skip, g_norm, w_out):
    bsz, seqlen, _ = h.shape
    zxbcdt = h @ w_in
    z = zxbcdt[..., :SSM_INNER]
    xbc = zxbcdt[..., SSM_INNER:SSM_INNER + SSM_CONV_DIM]
    dt_raw = zxbcdt[..., SSM_INNER + SSM_CONV_DIM:]
    xbc = jax.nn.silu(causal_depthwise_conv(xbc, conv_w, conv_b))
    gn = SSM_GROUPS * SSM_STATE
    xs = xbc[..., :SSM_INNER].reshape(bsz, seqlen, SSM_GROUPS, SSM_HPG, SSM_HEAD_DIM)
    bm = xbc[..., SSM_INNER:SSM_INNER + gn].reshape(bsz, seqlen, SSM_GROUPS, SSM_STATE)
    cm = xbc[..., SSM_INNER + gn:].reshape(bsz, seqlen, SSM_GROUPS, SSM_STATE)
    dt = jax.nn.softplus(dt_raw.astype(jnp.float32) + dt_bias.astype(jnp.float32))
    dt = dt.reshape(bsz, seqlen, SSM_GROUPS, SSM_HPG)
    a = -jnp.exp(a_log.astype(jnp.float32)).reshape(SSM_GROUPS, SSM_HPG)
    x_dt = (xs * dt[..., None]).astype(xs.dtype)
    y = ssd_chunked(x_dt, dt * a, bm, cm)
    y = y + d_skip.reshape(SSM_GROUPS, SSM_HPG)[..., None] * xs
    y = y.reshape(bsz, seqlen, SSM_INNER).astype(h.dtype)
    return gated_group_rmsnorm(y, z, g_norm) @ w_out


def clamped_swiglu(gu):
    g, lin = jnp.split(gu, 2, axis=-1)
    g = jnp.minimum(g, SWIGLU_LIMIT)
    lin = jnp.clip(lin, -SWIGLU_LIMIT, SWIGLU_LIMIT)
    return g * jax.nn.sigmoid(SWIGLU_ALPHA * g) * (lin + 1.0)


def moe_ffn(h, w_router, b_router, w_gate_up, b_gate_up, w_down, b_down):
    n_tok, d = h.shape
    logits = (h @ w_router + b_router).astype(jnp.float32)
    top_logit, top_e = lax.top_k(logits, TOP_K)
    gate = jax.nn.softmax(top_logit, axis=-1)
    n_pair = n_tok * TOP_K
    pair_e = top_e.reshape(-1).astype(jnp.int32)
    pair_tok = jnp.arange(n_pair, dtype=jnp.int32) // TOP_K
    pair_gate = gate.reshape(-1)
    order = jnp.argsort(pair_e, stable=True)
    e_sorted = pair_e[order]
    counts = jnp.bincount(pair_e, length=N_EXPERTS).astype(jnp.int32)
    padded = (counts + MOE_BLOCK - 1) // MOE_BLOCK * MOE_BLOCK
    pad_end = jnp.cumsum(padded)
    pad_start = pad_end - padded
    start = jnp.cumsum(counts) - counts
    slot = pad_start[e_sorted] + jnp.arange(n_pair, dtype=jnp.int32) - start[e_sorted]
    n_rows = -(-n_pair // MOE_BLOCK) * MOE_BLOCK + N_EXPERTS * MOE_BLOCK
    n_blk = n_rows // MOE_BLOCK
    row_tok = jnp.zeros((n_rows,), jnp.int32).at[slot].set(pair_tok[order])
    row_gate = jnp.zeros((n_rows,), jnp.float32).at[slot].set(pair_gate[order])
    blk_start = jnp.arange(n_blk, dtype=jnp.int32) * MOE_BLOCK
    blk_e = jnp.minimum(jnp.searchsorted(pad_end, blk_start, side='right'), N_EXPERTS - 1)

    def expert_block(args):
        tok, e = args
        gu = h[tok] @ w_gate_up[e] + b_gate_up[e]
        return clamped_swiglu(gu) @ w_down[e] + b_down[e]

    y_rows = lax.map(expert_block, (row_tok.reshape(n_blk, MOE_BLOCK), blk_e)).reshape(n_rows, d)
    y_rows = y_rows * row_gate[:, None].astype(y_rows.dtype)
    return jnp.zeros_like(h).at[row_tok].add(y_rows.astype(h.dtype))


def setup_inputs(seed: int = 0) -> dict:
    key = jax.random.key(seed)
    ks = iter(jax.random.split(key, 32))
    f32 = jnp.float32
    n_mla = (DEPTH + 1) // N_MIXERS
    n_ssm = DEPTH // N_MIXERS

    def normal(shape, std):
        return std * jax.random.normal(next(ks), shape, f32)

    def gain(shape):
        return 1.0 + 0.1 * jax.random.normal(next(ks), shape, f32)

    x = normal((BATCH, SEQ, D_MODEL), 1.0)
    c = normal((BATCH, D_MODEL), 1.0)
    offsets = jax.random.randint(next(ks), (BATCH, 1), 0, 2048, dtype=jnp.int32)
    positions = offsets + jnp.arange(SEQ, dtype=jnp.int32)[None, :]
    w_mod = normal((DEPTH, D_MODEL, 6 * D_MODEL), ADA_STD * D_MODEL ** -0.5)
    b_mod = normal((DEPTH, 6 * D_MODEL), 0.02)
    g_mix_norm = gain((DEPTH, D_MODEL))
    g_ffn_norm = gain((DEPTH, D_MODEL))
    mla_w_in = normal((n_mla, D_MODEL, MLA_IN_DIM), D_MODEL ** -0.5)
    mla_g_q = gain((n_mla, Q_LORA))
    mla_g_kv = gain((n_mla, KV_LORA))
    mla_w_q_up = normal((n_mla, Q_LORA, MLA_HEADS * (QK_NOPE + QK_ROPE)), Q_LORA ** -0.5)
    mla_w_kv_up = normal((n_mla, KV_LORA, MLA_HEADS * (QK_NOPE + V_HEAD)), KV_LORA ** -0.5)
    mla_w_out = normal((n_mla, MLA_HEADS * V_HEAD, D_MODEL), (MLA_HEADS * V_HEAD) ** -0.5)
    ssm_w_in = normal((n_ssm, D_MODEL, SSM_IN_DIM), D_MODEL ** -0.5)
    ssm_conv_w = normal((n_ssm, SSM_CONV, SSM_CONV_DIM), SSM_CONV ** -0.5)
    ssm_conv_b = normal((n_ssm, SSM_CONV_DIM), 0.02)
    dt0 = jnp.exp(jax.random.uniform(next(ks), (n_ssm, SSM_HEADS), f32, math.log(1e-3), math.log(1e-1)))
    ssm_dt_bias = dt0 + jnp.log(-jnp.expm1(-dt0))
    ssm_a_log = jnp.log(jax.random.uniform(next(ks), (n_ssm, SSM_HEADS), f32, 1.0, 16.0))
    ssm_d = gain((n_ssm, SSM_HEADS))
    ssm_g_norm = gain((n_ssm, SSM_INNER))
    ssm_w_out = normal((n_ssm, SSM_INNER, D_MODEL), SSM_INNER ** -0.5)
    moe_w_router = normal((DEPTH, D_MODEL, N_EXPERTS), D_MODEL ** -0.5)
    moe_b_router = normal((DEPTH, N_EXPERTS), 0.01)
    moe_w_gate_up = normal((DEPTH, N_EXPERTS, D_MODEL, 2 * EXPERT_FF), D_MODEL ** -0.5)
    moe_b_gate_up = normal((DEPTH, N_EXPERTS, 2 * EXPERT_FF), 0.02)
    moe_w_down = normal((DEPTH, N_EXPERTS, EXPERT_FF, D_MODEL), EXPERT_FF ** -0.5)
    moe_b_down = normal((DEPTH, N_EXPERTS, D_MODEL), 0.02)
    g_final = gain((D_MODEL,))
    return {"x": x, "c": c, "positions": positions,
            "w_mod": w_mod, "b_mod": b_mod, "g_mix_norm": g_mix_norm, "g_ffn_norm": g_ffn_norm,
            "mla_w_in": mla_w_in, "mla_g_q": mla_g_q, "mla_g_kv": mla_g_kv,
            "mla_w_q_up": mla_w_q_up, "mla_w_kv_up": mla_w_kv_up, "mla_w_out": mla_w_out,
            "ssm_w_in": ssm_w_in, "ssm_conv_w": ssm_conv_w, "ssm_conv_b": ssm_conv_b,
            "ssm_dt_bias": ssm_dt_bias, "ssm_a_log": ssm_a_log, "ssm_d": ssm_d,
            "ssm_g_norm": ssm_g_norm, "ssm_w_out": ssm_w_out,
            "moe_w_router": moe_w_router, "moe_b_router": moe_b_router,
            "moe_w_gate_up": moe_w_gate_up, "moe_b_gate_up": moe_b_gate_up,
            "moe_w_down": moe_w_down, "moe_b_down": moe_b_down,
            "g_final": g_final}


def reference(x, c, positions, w_mod, b_mod, g_mix_norm, g_ffn_norm,
              mla_w_in, mla_g_q, mla_g_kv, mla_w_q_up, mla_w_kv_up, mla_w_out,
              ssm_w_in, ssm_conv_w, ssm_conv_b, ssm_dt_bias, ssm_a_log, ssm_d, ssm_g_norm, ssm_w_out,
              moe_w_router, moe_b_router, moe_w_gate_up, moe_b_gate_up, moe_w_down, moe_b_down,
              g_final):
    bsz, seqlen, d = x.shape
    cond = jax.nn.silu(c)
    for i in range(DEPTH):
        mod = cond @ w_mod[i] + b_mod[i]
        sh1, sc1, gt1, sh2, sc2, gt2 = [m[:, None, :] for m in jnp.split(mod, 6, axis=-1)]
        h = rmsnorm(x, g_mix_norm[i]) * (1.0 + sc1) + sh1
        j = i // N_MIXERS
        if i % N_MIXERS == 0:
            y = mla_mixer(h, positions, mla_w_in[j], mla_g_q[j], mla_g_kv[j],
                          mla_w_q_up[j], mla_w_kv_up[j], mla_w_out[j])
        else:
            y = ssd_mixer(h, ssm_w_in[j], ssm_conv_w[j], ssm_conv_b[j], ssm_dt_bias[j],
                          ssm_a_log[j], ssm_d[j], ssm_g_norm[j], ssm_w_out[j])
        x = x + (gt1 * y).astype(x.dtype)
        h = rmsnorm(x, g_ffn_norm[i]) * (1.0 + sc2) + sh2
        y = moe_ffn(h.reshape(bsz * seqlen, d), moe_w_router[i], moe_b_router[i],
                    moe_w_gate_up[i], moe_b_gate_up[i], moe_w_down[i], moe_b_down[i])
        x = x + (gt2 * y.reshape(bsz, seqlen, d)).astype(x.dtype)
    return rmsnorm(x, g_final)
```

```python
import functools
import math

import jax
import jax.numpy as jnp
from jax import lax
from jax.experimental import pallas as pl
from jax.experimental.pallas import tpu as pltpu

F32 = jnp.float32
BF16 = jnp.bfloat16
I32 = jnp.int32

NORM_EPS = 1e-6
MLA_HEADS = 8
Q_LORA = 256
KV_LORA = 256
QK_NOPE = 128
QK_ROPE = 64
V_HEAD = 128
QK_HEAD = QK_NOPE + QK_ROPE
ROPE_THETA = 10000.0
MLA_SCALE = QK_HEAD ** -0.5
SSM_HEAD_DIM = 64
SSM_GROUPS = 8
SSM_HPG = 4
SSM_HEADS = SSM_GROUPS * SSM_HPG
SSM_STATE = 128
SSM_CONV = 4
SSM_GROUP_W = SSM_HPG * SSM_HEAD_DIM
SSM_INNER = SSM_GROUPS * SSM_GROUP_W
SSM_BC = SSM_GROUPS * SSM_STATE
SSM_CONV_DIM = SSM_INNER + 2 * SSM_BC
N_EXPERTS = 32
TOP_K = 4
SWIGLU_LIMIT = 7.0
SWIGLU_ALPHA = 1.702
MOE_BLOCK = 256

LANES = 128
VMEM_LIMIT = 56 * 1024 * 1024


def _cparams(*sem):
    return pltpu.CompilerParams(dimension_semantics=tuple(sem), vmem_limit_bytes=VMEM_LIMIT)


def _dot(a, b):
    return jnp.dot(a, b, preferred_element_type=F32)


def _dot_nt(a, b):
    return lax.dot_general(a, b, (((1,), (1,)), ((), ())), preferred_element_type=F32)


def _dot_tn(a, b):
    return lax.dot_general(a, b, (((0,), (0,)), ((), ())), preferred_element_type=F32)


def _split3(a):
    hi = a.astype(BF16)
    r1 = a - hi.astype(F32)
    mid = r1.astype(BF16)
    lo = (r1 - mid.astype(F32)).astype(BF16)
    return hi, mid, lo


def _dot_f32ish(a, b):
    ah, am, _ = _split3(a)
    bh, bm, _ = _split3(b)
    return _dot(ah, bh) + (_dot(ah, bm) + _dot(am, bh))


def _sigmoid(x):
    return 1.0 / (1.0 + jnp.exp(-x))


def _rms(x):
    return x * lax.rsqrt(jnp.mean(x * x, axis=-1, keepdims=True) + NORM_EPS)


def _mod_kernel(c_ref, w_ref, b_ref, o_ref):
    c = c_ref[...]
    cond = c * _sigmoid(c)
    o_ref[...] = _dot_f32ish(cond, w_ref[...]) + b_ref[...]


def _modulation(c, w_mod, b_mod):
    depth, d, d6 = w_mod.shape
    bsz = c.shape[0]
    nj = d6 // d
    return pl.pallas_call(
        _mod_kernel,
        out_shape=jax.ShapeDtypeStruct((depth, nj, bsz, d), F32),
        grid=(depth, nj),
        in_specs=[
            pl.BlockSpec((bsz, d), lambda l, j: (0, 0)),
            pl.BlockSpec((None, d, d), lambda l, j: (l, 0, j)),
            pl.BlockSpec((None, 1, d), lambda l, j: (l, 0, j)),
        ],
        out_specs=pl.BlockSpec((None, None, bsz, d), lambda l, j: (l, j, 0, 0)),
        compiler_params=_cparams("arbitrary", "arbitrary"),
        name="mod",
    )(c, w_mod, b_mod.reshape(depth, 1, d6))


def _mla_pre_kernel(x_ref, pos_ref, sh_ref, sc_ref, g_ref, win_ref, gq_ref, gkv_ref, wq_ref, wkv_ref,
                    invf_ref, q_ref, k_ref, v_ref):
    x = x_ref[...]
    h = _rms(x) * g_ref[...] * (1.0 + sc_ref[...]) + sh_ref[...]
    lat = _dot(h.astype(BF16), win_ref[...])
    q_lat = _rms(lat[:, :Q_LORA]) * gq_ref[...]
    kv_lat = _rms(lat[:, Q_LORA:Q_LORA + KV_LORA]) * gkv_ref[...]
    ang = pos_ref[...] * invf_ref[...]
    cs = jnp.cos(ang)
    sn = jnp.sin(ang)
    o = Q_LORA + KV_LORA
    k_rope = lat[:, o:o + LANES] * cs + lat[:, o + LANES:o + 2 * LANES] * sn
    qq = _dot(q_lat.astype(BF16), wq_ref[...])
    kv = _dot(kv_lat.astype(BF16), wkv_ref[...])
    rot0 = MLA_HEADS * 2 * LANES
    k_rope_b = k_rope[:, :QK_ROPE].astype(BF16)
    for hd in range(MLA_HEADS):
        c0 = hd * 2 * LANES
        q_nope = qq[:, c0:c0 + LANES] * MLA_SCALE
        q_rope = (qq[:, c0 + LANES:c0 + 2 * LANES] * cs
                  + qq[:, rot0 + hd * LANES:rot0 + (hd + 1) * LANES] * sn) * MLA_SCALE
        q_ref[hd, :, 0:QK_NOPE] = q_nope.astype(BF16)
        q_ref[hd, :, QK_NOPE:QK_HEAD] = q_rope[:, :QK_ROPE].astype(BF16)
        k_ref[hd, :, 0:QK_NOPE] = kv[:, c0:c0 + LANES].astype(BF16)
        k_ref[hd, :, QK_NOPE:QK_HEAD] = k_rope_b
        v_ref[hd] = kv[:, c0 + LANES:c0 + 2 * LANES].astype(BF16)


def _rot_half_cols(w):
    half = QK_ROPE // 2
    return jnp.concatenate([-w[..., half:], w[..., :half]], axis=-1)


def _mla_pre(x2d, pos_f, sh, sc, g, w_in, g_q, g_kv, w_q_up, w_kv_up, bsz, seqlen, tm):
    n, d = x2d.shape
    hh = MLA_HEADS
    o = Q_LORA + KV_LORA
    wr = w_in[:, o:o + QK_ROPE]
    zpad = jnp.zeros((d, LANES - QK_ROPE), F32)
    w_in_ext = jnp.concatenate([w_in[:, :o], wr, zpad, _rot_half_cols(wr), zpad], axis=1).astype(BF16)
    wq = w_q_up.reshape(Q_LORA, hh, QK_HEAD)
    zq = jnp.zeros((Q_LORA, hh, LANES - QK_ROPE), F32)
    wq_main = jnp.concatenate([wq, zq], axis=-1).reshape(Q_LORA, hh * 2 * LANES)
    wq_rot = jnp.concatenate([_rot_half_cols(wq[..., QK_NOPE:]), zq], axis=-1).reshape(Q_LORA, hh * LANES)
    wq_ext = jnp.concatenate([wq_main, wq_rot], axis=1).astype(BF16)
    inv_freq = 1.0 / (ROPE_THETA ** (jnp.arange(0, QK_ROPE, 2, dtype=F32) / QK_ROPE))
    invf = jnp.concatenate([inv_freq, inv_freq, jnp.zeros((LANES - QK_ROPE,), F32)]).reshape(1, LANES)
    tpb = seqlen // tm
    vec = lambda i: (0, 0)
    outs = pl.pallas_call(
        _mla_pre_kernel,
        out_shape=(
            jax.ShapeDtypeStruct((bsz, hh, seqlen, QK_HEAD), BF16),
            jax.ShapeDtypeStruct((bsz, hh, seqlen, QK_HEAD), BF16),
            jax.ShapeDtypeStruct((bsz, hh, seqlen, V_HEAD), BF16),
        ),
        grid=(n // tm,),
        in_specs=[
            pl.BlockSpec((tm, d), lambda i: (i, 0)),
            pl.BlockSpec((tm, 1), lambda i: (i, 0)),
            pl.BlockSpec((None, 1, d), lambda i: (i // tpb, 0, 0)),
            pl.BlockSpec((None, 1, d), lambda i: (i // tpb, 0, 0)),
            pl.BlockSpec((1, d), vec),
            pl.BlockSpec(w_in_ext.shape, vec),
            pl.BlockSpec((1, Q_LORA), vec),
            pl.BlockSpec((1, KV_LORA), vec),
            pl.BlockSpec(wq_ext.shape, vec),
            pl.BlockSpec((KV_LORA, hh * 2 * LANES), vec),
            pl.BlockSpec((1, LANES), vec),
        ],
        out_specs=(
            pl.BlockSpec((None, hh, tm, QK_HEAD), lambda i: (i // tpb, 0, i % tpb, 0)),
            pl.BlockSpec((None, hh, tm, QK_HEAD), lambda i: (i // tpb, 0, i % tpb, 0)),
            pl.BlockSpec((None, hh, tm, V_HEAD), lambda i: (i // tpb, 0, i % tpb, 0)),
        ),
        compiler_params=_cparams("arbitrary"),
        name="mla_pre",
    )(x2d, pos_f, sh, sc, g.reshape(1, d), w_in_ext, g_q.reshape(1, -1), g_kv.reshape(1, -1), wq_ext,
      w_kv_up.astype(BF16), invf)
    return outs


def _attn_kernel(q_ref, k_ref, v_ref, o_ref, m_scr, l_scr, acc_scr, *, tq):
    qi = pl.program_id(2)
    q = q_ref[...]
    m_scr[...] = jnp.full(m_scr.shape, -jnp.inf, F32)
    l_scr[...] = jnp.zeros(l_scr.shape, F32)
    acc_scr[...] = jnp.zeros(acc_scr.shape, F32)

    def block(j, masked):
        r0 = pl.multiple_of(j * tq, tq)
        k = k_ref[pl.ds(r0, tq), :]
        v = v_ref[pl.ds(r0, tq), :]
        s = _dot_nt(q, k)
        if masked:
            row = lax.broadcasted_iota(I32, (tq, tq), 0)
            col = lax.broadcasted_iota(I32, (tq, tq), 1)
            s = jnp.where(row >= col, s, -jnp.inf)
        m_prev = m_scr[...]
        m_new = jnp.maximum(m_prev, jnp.max(s, axis=-1, keepdims=True))
        alpha = jnp.exp(m_prev - m_new)
        p = jnp.exp(s - m_new)
        l_scr[...] = alpha * l_scr[...] + jnp.sum(p, axis=-1, keepdims=True)
        acc_scr[...] = alpha * acc_scr[...] + _dot(p.astype(BF16), v)
        m_scr[...] = m_new

    def body(j, carry):
        block(j, False)
        return carry

    lax.fori_loop(0, qi, body, 0)
    block(qi, True)
    o_ref[...] = (acc_scr[...] / l_scr[...]).astype(o_ref.dtype)


def _attention(q, k, v, tq):
    bsz, hh, seqlen, _ = q.shape
    return pl.pallas_call(
        functools.partial(_attn_kernel, tq=tq),
        out_shape=jax.ShapeDtypeStruct((bsz, seqlen, hh * V_HEAD), BF16),
        grid=(bsz, hh, seqlen // tq),
        in_specs=[
            pl.BlockSpec((None, None, tq, QK_HEAD), lambda b, h, i: (b, h, i, 0)),
            pl.BlockSpec((None, None, seqlen, QK_HEAD), lambda b, h, i: (b, h, 0, 0)),
            pl.BlockSpec((None, None, seqlen, V_HEAD), lambda b, h, i: (b, h, 0, 0)),
        ],
        out_specs=pl.BlockSpec((None, tq, V_HEAD), lambda b, h, i: (b, i, h)),
        scratch_shapes=[
            pltpu.VMEM((tq, 1), F32),
            pltpu.VMEM((tq, 1), F32),
            pltpu.VMEM((tq, V_HEAD), F32),
        ],
        compiler_params=_cparams("arbitrary", "arbitrary", "arbitrary"),
        name="attn",
    )(q, k, v)


def _post_mixer_kernel(x_ref, o_ref, wout_ref, gt1_ref, sh_ref, sc_ref, g_ref, wr_ref, br_ref,
                       x1_ref, h2_ref, meta_ref, gate_ref, cnt_ref, carry_scr, *, tm):
    i = pl.program_id(0)

    @pl.when(i == 0)
    def _():
        carry_scr[...] = jnp.zeros(carry_scr.shape, F32)

    y = _dot(o_ref[...], wout_ref[...])
    x1 = x_ref[...] + gt1_ref[...] * y
    x1_ref[...] = x1
    h2 = _rms(x1) * g_ref[...] * (1.0 + sc_ref[...]) + sh_ref[...]
    h2_ref[...] = h2
    logits = _dot_f32ish(h2, wr_ref[...]) + br_ref[...]
    lane = lax.broadcasted_iota(I32, (tm, LANES), 1).astype(F32)
    work = logits
    idxs, vals = [], []
    for _ in range(TOP_K):
        mx = jnp.max(work, axis=-1, keepdims=True)
        idx = jnp.min(jnp.where(work == mx, lane, float(LANES)), axis=-1, keepdims=True)
        idxs.append(idx)
        vals.append(mx)
        work = jnp.where(lane == idx, -jnp.inf, work)
    exps = [jnp.exp(vk - vals[0]) for vk in vals]
    denom = exps[0] + exps[1] + exps[2] + exps[3]
    onehot = jnp.zeros((tm, LANES), F32)
    for idx in idxs:
        onehot = onehot + jnp.where(lane == idx, 1.0, 0.0)
    row = lax.broadcasted_iota(I32, (tm, tm), 0)
    col = lax.broadcasted_iota(I32, (tm, tm), 1)
    ltri = jnp.where(row > col, 1.0, 0.0).astype(BF16)
    cum = _dot(ltri, onehot.astype(BF16)) + carry_scr[...]
    meta = jnp.zeros((tm, LANES), F32)
    gates = jnp.zeros((tm, LANES), F32)
    for kk in range(TOP_K):
        rank = jnp.sum(jnp.where(lane == idxs[kk], cum, 0.0), axis=-1, keepdims=True)
        meta = jnp.where(lane == float(kk), idxs[kk], meta)
        meta = jnp.where(lane == float(TOP_K + kk), rank, meta)
        gates = jnp.where(lane == float(kk), exps[kk] / denom, gates)
    meta_ref[...] = meta.astype(I32)
    gate_ref[...] = gates
    carry = carry_scr[...] + jnp.sum(onehot, axis=0, keepdims=True)
    carry_scr[...] = carry
    cnt_ref[...] = carry.astype(I32)


def _post_mixer(x2d, o2d, w_out, gt1, sh2, sc2, g_ffn, w_router, b_router, seqlen, tm):
    n, d = x2d.shape
    kdim = o2d.shape[1]
    tpb = seqlen // tm
    wr = jnp.concatenate([w_router, jnp.zeros((d, LANES - N_EXPERTS), F32)], axis=1)
    br = jnp.concatenate([b_router, jnp.full((LANES - N_EXPERTS,), -1e30, F32)]).reshape(1, LANES)
    vec = lambda i: (0, 0)
    bvec = lambda i: (i // tpb, 0, 0)
    return pl.pallas_call(
        functools.partial(_post_mixer_kernel, tm=tm),
        out_shape=(
            jax.ShapeDtypeStruct((n, d), F32),
            jax.ShapeDtypeStruct((n, d), F32),
            jax.ShapeDtypeStruct((n, LANES), I32),
            jax.ShapeDtypeStruct((n, LANES), F32),
            jax.ShapeDtypeStruct((1, LANES), I32),
        ),
        grid=(n // tm,),
        in_specs=[
            pl.BlockSpec((tm, d), lambda i: (i, 0)),
            pl.BlockSpec((tm, kdim), lambda i: (i, 0)),
            pl.BlockSpec((kdim, d), vec),
            pl.BlockSpec((None, 1, d), bvec),
            pl.BlockSpec((None, 1, d), bvec),
            pl.BlockSpec((None, 1, d), bvec),
            pl.BlockSpec((1, d), vec),
            pl.BlockSpec((d, LANES), vec),
            pl.BlockSpec((1, LANES), vec),
        ],
        out_specs=(
            pl.BlockSpec((tm, d), lambda i: (i, 0)),
            pl.BlockSpec((tm, d), lambda i: (i, 0)),
            pl.BlockSpec((tm, LANES), lambda i: (i, 0)),
            pl.BlockSpec((tm, LANES), lambda i: (i, 0)),
            pl.BlockSpec((1, LANES), vec),
        ),
        scratch_shapes=[pltpu.VMEM((1, LANES), F32)],
        compiler_params=_cparams("arbitrary"),
        name="post_mixer",
    )(x2d, o2d, w_out.astype(BF16), gt1, sh2, sc2, g_ffn.reshape(1, d), wr, br)


def _row_copy(src, s, dst, t, sem):
    return pltpu.make_async_copy(src.at[pl.ds(s, 1)], dst.at[pl.ds(t, 1)], sem)


def _dispatch_kernel(pstart_ref, cnt_ref, e_ref, r_ref, h_hbm, xs_hbm, zbuf, sem, zsem, *, ch):
    i = pl.program_id(0)
    base_tok = i * (ch // TOP_K)

    @pl.when(i == 0)
    def _():
        zbuf[...] = jnp.zeros(zbuf.shape, F32)

        def per_expert(e, carry):
            lo = pstart_ref[e] + cnt_ref[e]
            hi = pstart_ref[e + 1]

            def start(j, c):
                _row_copy(zbuf, 0, xs_hbm, j, zsem).start()
                return c

            def wait(j, c):
                _row_copy(zbuf, 0, xs_hbm, j, zsem).wait()
                return c

            lax.fori_loop(lo, hi, start, 0)
            lax.fori_loop(lo, hi, wait, 0)
            return carry

        lax.fori_loop(0, N_EXPERTS, per_expert, 0)

    def start(p, c):
        dst = pstart_ref[e_ref[0, p]] + r_ref[0, p]
        _row_copy(h_hbm, base_tok + p // TOP_K, xs_hbm, dst, sem).start()
        return c

    def wait(p, c):
        _row_copy(h_hbm, 0, xs_hbm, 0, sem).wait()
        return c

    lax.fori_loop(0, ch, start, 0)
    lax.fori_loop(0, ch, wait, 0)


def _dispatch(h2, e_chunks, r_chunks, pstart, counts, n_rows):
    n, d = h2.shape
    nch, _, ch = e_chunks.shape
    grid_spec = pltpu.PrefetchScalarGridSpec(
        num_scalar_prefetch=2,
        grid=(nch,),
        in_specs=[
            pl.BlockSpec((None, 1, ch), lambda i, ps, cn: (i, 0, 0), memory_space=pltpu.SMEM),
            pl.BlockSpec((None, 1, ch), lambda i, ps, cn: (i, 0, 0), memory_space=pltpu.SMEM),
            pl.BlockSpec(memory_space=pl.ANY),
        ],
        out_specs=pl.BlockSpec(memory_space=pl.ANY),
        scratch_shapes=[
            pltpu.VMEM((8, d), F32),
            pltpu.SemaphoreType.DMA,
            pltpu.SemaphoreType.DMA,
        ],
    )
    return pl.pallas_call(
        functools.partial(_dispatch_kernel, ch=ch),
        out_shape=jax.ShapeDtypeStruct((n_rows, d), F32),
        grid_spec=grid_spec,
        compiler_params=pltpu.CompilerParams(dimension_semantics=("arbitrary",), vmem_limit_bytes=VMEM_LIMIT,
                                             has_side_effects=True),
        name="dispatch",
    )(pstart, counts, e_chunks, r_chunks, h2)


def _ffn_kernel(be_ref, nu_ref, xs_ref, wgu_ref, bgu_ref, wd_ref, bd_ref, ys_ref, wgu_b, wd_b, *, ff):
    i = pl.program_id(0)
    prev = be_ref[jnp.maximum(i - 1, 0)]
    changed = jnp.logical_or(i == 0, be_ref[i] != prev)

    @pl.when(changed)
    def _():
        wgu_b[...] = wgu_ref[...].astype(BF16)
        wd_b[...] = wd_ref[...].astype(BF16)

    @pl.when(i < nu_ref[0])
    def _():
        gu = _dot(xs_ref[...].astype(BF16), wgu_b[...]) + bgu_ref[...]
        g = jnp.minimum(gu[:, :ff], SWIGLU_LIMIT)
        lin = jnp.clip(gu[:, ff:], -SWIGLU_LIMIT, SWIGLU_LIMIT)
        act = g * _sigmoid(SWIGLU_ALPHA * g) * (lin + 1.0)
        ys_ref[...] = _dot(act.astype(BF16), wd_b[...]) + bd_ref[...]

    @pl.when(i >= nu_ref[0])
    def _():
        ys_ref[...] = jnp.zeros(ys_ref.shape, F32)


def _ffn(xs, blk_e, n_used, w_gate_up, b_gate_up, w_down, b_down):
    n_rows, d = xs.shape
    ne, _, ff2 = w_gate_up.shape
    ff = ff2 // 2
    n_blk = n_rows // MOE_BLOCK
    grid_spec = pltpu.PrefetchScalarGridSpec(
        num_scalar_prefetch=2,
        grid=(n_blk,),
        in_specs=[
            pl.BlockSpec((MOE_BLOCK, d), lambda i, be, nu: (jnp.minimum(i, nu[0] - 1), 0)),
            pl.BlockSpec((None, d, ff2), lambda i, be, nu: (be[i], 0, 0)),
            pl.BlockSpec((None, 1, ff2), lambda i, be, nu: (be[i], 0, 0)),
            pl.BlockSpec((None, ff, d), lambda i, be, nu: (be[i], 0, 0)),
            pl.BlockSpec((None, 1, d), lambda i, be, nu: (be[i], 0, 0)),
        ],
        out_specs=pl.BlockSpec((MOE_BLOCK, d), lambda i, be, nu: (i, 0)),
        scratch_shapes=[pltpu.VMEM((d, ff2), BF16), pltpu.VMEM((ff, d), BF16)],
    )
    return pl.pallas_call(
        functools.partial(_ffn_kernel, ff=ff),
        out_shape=jax.ShapeDtypeStruct((n_rows, d), F32),
        grid_spec=grid_spec,
        compiler_params=_cparams("arbitrary"),
        name="ffn",
    )(blk_e, n_used, xs, w_gate_up, b_gate_up.reshape(ne, 1, ff2), w_down, b_down.reshape(ne, 1, d))


def _combine_kernel(pstart_ref, e_ref, r_ref, x1_ref, gate_ref, gt2_ref, gfin_ref, ys_hbm, o_ref, buf, sem,
                    *, tm, final):
    def start(t, c):
        for kk in range(TOP_K):
            p = t * TOP_K + kk
            src = pstart_ref[e_ref[0, p]] + r_ref[0, p]
            pltpu.make_async_copy(ys_hbm.at[pl.ds(src, 1)], buf.at[kk, pl.ds(t, 1)], sem).start()
        return c

    def wait(t, c):
        for kk in range(TOP_K):
            pltpu.make_async_copy(ys_hbm.at[pl.ds(0, 1)], buf.at[kk, pl.ds(t, 1)], sem).wait()
        return c

    lax.fori_loop(0, tm, start, 0)
    lax.fori_loop(0, tm, wait, 0)
    gates = gate_ref[...]
    acc = gates[:, 0:1] * buf[0]
    for kk in range(1, TOP_K):
        acc = acc + gates[:, kk:kk + 1] * buf[kk]
    x2 = x1_ref[...] + gt2_ref[...] * acc
    if final:
        x2 = _rms(x2) * gfin_ref[...]
    o_ref[...] = x2


def _combine(x1, gates, gt2, g_final, ys, e_chunks, r_chunks, pstart, seqlen, tm, final):
    n, d = x1.shape
    tpb = seqlen // tm
    ch = tm * TOP_K
    grid_spec = pltpu.PrefetchScalarGridSpec(
        num_scalar_prefetch=1,
        grid=(n // tm,),
        in_specs=[
            pl.BlockSpec((None, 1, ch), lambda i, ps: (i, 0, 0), memory_space=pltpu.SMEM),
            pl.BlockSpec((None, 1, ch), lambda i, ps: (i, 0, 0), memory_space=pltpu.SMEM),
            pl.BlockSpec((tm, d), lambda i, ps: (i, 0)),
            pl.BlockSpec((tm, LANES), lambda i, ps: (i, 0)),
            pl.BlockSpec((None, 1, d), lambda i, ps: (i // tpb, 0, 0)),
            pl.BlockSpec((1, d), lambda i, ps: (0, 0)),
            pl.BlockSpec(memory_space=pl.ANY),
        ],
        out_specs=pl.BlockSpec((tm, d), lambda i, ps: (i, 0)),
        scratch_shapes=[pltpu.VMEM((TOP_K, tm, d), F32), pltpu.SemaphoreType.DMA],
    )
    return pl.pallas_call(
        functools.partial(_combine_kernel, tm=tm, final=final),
        out_shape=jax.ShapeDtypeStruct((n, d), F32),
        grid_spec=grid_spec,
        compiler_params=_cparams("arbitrary"),
        name="combine",
    )(pstart, e_chunks.reshape(n // tm, 1, ch), r_chunks.reshape(n // tm, 1, ch), x1, gates, gt2,
      g_final.reshape(1, d), ys)


def _moe(x1, h2, meta, gates, counts, gt2, g_final, w_gate_up, b_gate_up, w_down, b_down, seqlen, final):
    n, d = x1.shape
    n_pair = n * TOP_K
    n_rows = -(-n_pair // MOE_BLOCK) * MOE_BLOCK + N_EXPERTS * MOE_BLOCK
    n_blk = n_rows // MOE_BLOCK
    cnt = counts[0, :N_EXPERTS]
    padded = (cnt + MOE_BLOCK - 1) // MOE_BLOCK * MOE_BLOCK
    pad_end = jnp.cumsum(padded)
    pstart = jnp.concatenate([pad_end - padded, pad_end[-1:]]).astype(I32)
    n_used = (pad_end[-1:] // MOE_BLOCK).astype(I32)
    blk_start = jnp.arange(n_blk, dtype=I32) * MOE_BLOCK
    blk_e = jnp.minimum(jnp.searchsorted(pad_end, blk_start, side='right'), N_EXPERTS - 1).astype(I32)
    ch = min(4096, n_pair)
    e_flat = meta[:, :TOP_K].reshape(n_pair // ch, 1, ch)
    r_flat = meta[:, TOP_K:2 * TOP_K].reshape(n_pair // ch, 1, ch)
    xs = _dispatch(h2, e_flat, r_flat, pstart, cnt, n_rows)
    ys = _ffn(xs, blk_e, n_used, w_gate_up, b_gate_up, w_down, b_down)
    tm = min(256, seqlen)
    return _combine(x1, gates, gt2, g_final, ys, e_flat, r_flat, pstart, seqlen, tm, final)


def _softplus(x):
    return jnp.maximum(x, 0.0) + jnp.log(1.0 + jnp.exp(-jnp.abs(x)))


def _ssd_pre_kernel(x_ref, sh_ref, sc_ref, g_ref, wz_ref, wx_ref, wdt_ref, wdtT_ref, cw_ref, cb_ref,
                    dtb_ref, dtbT_ref, a_ref, aT_ref,
                    z_ref, xbc_ref, dt_ref, adt_ref, adtT_ref, ubuf, *, tm, tpb, cchunk):
    i = pl.program_id(0)
    x = x_ref[...]
    h = _rms(x) * g_ref[...] * (1.0 + sc_ref[...]) + sh_ref[...]
    hb = h.astype(BF16)
    z_ref[...] = _dot(hb, wz_ref[...]).astype(BF16)
    dt = _softplus(_dot(hb, wdt_ref[...]) + dtb_ref[...])
    dt_ref[...] = dt
    adt_ref[...] = dt * a_ref[...]
    dtT = _softplus(_dot_nt(wdtT_ref[...], hb) + dtbT_ref[...])
    adtT_ref[...] = dtT * aT_ref[...]

    @pl.when(i % tpb == 0)
    def _():
        ubuf[0:8, :] = jnp.zeros((8, ubuf.shape[1]), F32)

    for c0 in range(0, SSM_CONV_DIM, cchunk):
        cols = slice(c0, c0 + cchunk)
        ubuf[8:8 + tm, cols] = _dot(hb, wx_ref[:, cols])
        acc = cb_ref[:, cols] + cw_ref[0:1, cols] * ubuf[pl.ds(5, tm), cols]
        for kk in range(1, SSM_CONV):
            acc = acc + cw_ref[kk:kk + 1, cols] * ubuf[pl.ds(5 + kk, tm), cols]
        xbc_ref[:, cols] = (acc * _sigmoid(acc)).astype(BF16)
        ubuf[0:8, cols] = ubuf[tm:tm + 8, cols]


def _ssd_pre(x2d, sh, sc, g, w_in, conv_w, conv_b, dt_bias, a_log, bsz, seqlen, tm):
    n, d = x2d.shape
    tpb = seqlen // tm
    nh = SSM_HEADS
    wz = w_in[:, :SSM_INNER].astype(BF16)
    wx = w_in[:, SSM_INNER:SSM_INNER + SSM_CONV_DIM].astype(BF16)
    wdt_raw = w_in[:, SSM_INNER + SSM_CONV_DIM:]
    wdt = jnp.concatenate([wdt_raw, jnp.zeros((d, LANES - nh), F32)], axis=1).astype(BF16)
    wdtT = wdt_raw.T.astype(BF16)
    pad = jnp.zeros((LANES - nh,), F32)
    dtb = jnp.concatenate([dt_bias, pad]).reshape(1, LANES)
    a_neg = -jnp.exp(a_log.astype(F32))
    a_row = jnp.concatenate([a_neg, pad]).reshape(1, LANES)
    vec = lambda i: (0, 0)
    bvec = lambda i: (i // tpb, 0, 0)
    return pl.pallas_call(
        functools.partial(_ssd_pre_kernel, tm=tm, tpb=tpb, cchunk=1024),
        out_shape=(
            jax.ShapeDtypeStruct((n, SSM_INNER), BF16),
            jax.ShapeDtypeStruct((n, SSM_CONV_DIM), BF16),
            jax.ShapeDtypeStruct((n, LANES), F32),
            jax.ShapeDtypeStruct((n, LANES), F32),
            jax.ShapeDtypeStruct((bsz, nh, seqlen), F32),
        ),
        grid=(n // tm,),
        in_specs=[
            pl.BlockSpec((tm, d), lambda i: (i, 0)),
            pl.BlockSpec((None, 1, d), bvec),
            pl.BlockSpec((None, 1, d), bvec),
            pl.BlockSpec((1, d), vec),
            pl.BlockSpec((d, SSM_INNER), vec),
            pl.BlockSpec((d, SSM_CONV_DIM), vec),
            pl.BlockSpec((d, LANES), vec),
            pl.BlockSpec((nh, d), vec),
            pl.BlockSpec((SSM_CONV, SSM_CONV_DIM), vec),
            pl.BlockSpec((1, SSM_CONV_DIM), vec),
            pl.BlockSpec((1, LANES), vec),
            pl.BlockSpec((nh, 1), vec),
            pl.BlockSpec((1, LANES), vec),
            pl.BlockSpec((nh, 1), vec),
        ],
        out_specs=(
            pl.BlockSpec((tm, SSM_INNER), lambda i: (i, 0)),
            pl.BlockSpec((tm, SSM_CONV_DIM), lambda i: (i, 0)),
            pl.BlockSpec((tm, LANES), lambda i: (i, 0)),
            pl.BlockSpec((tm, LANES), lambda i: (i, 0)),
            pl.BlockSpec((None, nh, tm), lambda i: (i // tpb, 0, i % tpb)),
        ),
        scratch_shapes=[pltpu.VMEM((tm + 8, SSM_CONV_DIM), F32)],
        compiler_params=_cparams("arbitrary"),
        name="ssd_pre",
    )(x2d, sh, sc, g.reshape(1, d), wz, wx, wdt, wdtT, conv_w, conv_b.reshape(1, -1), dtb,
      dt_bias.reshape(nh, 1), a_row, a_neg.reshape(nh, 1))


def _expand_heads(v, g, rows):
    lane = lax.broadcasted_iota(I32, (rows, LANES), 1)
    lo = lane < SSM_HEAD_DIM
    parts = []
    for j in range(0, SSM_HPG, 2):
        h0 = g * SSM_HPG + j
        parts.append(jnp.where(lo, v[:, h0:h0 + 1], v[:, h0 + 1:h0 + 2]))
    return jnp.concatenate(parts, axis=1)


def _ssd_scan_kernel(xbc_ref, z_ref, dt_ref, adt_ref, adtT_ref, d_ref, gn_ref, yn_ref, state, *, lc):
    c = pl.program_id(1)

    @pl.when(c == 0)
    def _():
        state[...] = jnp.zeros(state.shape, F32)

    row = lax.broadcasted_iota(I32, (lc, lc), 0)
    col = lax.broadcasted_iota(I32, (lc, lc), 1)
    causal = row >= col
    tri = jnp.where(causal, 1.0, 0.0).astype(BF16)
    triT = jnp.where(row <= col, 1.0, 0.0).astype(BF16)
    ah, am, al = _split3(adt_ref[...])
    a_cum = _dot(tri, ah) + (_dot(tri, am) + _dot(tri, al))
    th, tmid, tl = _split3(adtT_ref[...])
    a_cumT = _dot(th, triT) + (_dot(tmid, triT) + _dot(tl, triT))
    dt = dt_ref[...]
    a_last = a_cum[lc - 1:lc, :]
    e_cum = jnp.exp(a_cum)
    d2e = jnp.exp(a_last - a_cum)
    cdec = jnp.exp(a_last)
    dskip = d_ref[...]
    lane2 = lax.broadcasted_iota(I32, (lc, SSM_GROUP_W), 1)
    for g in range(SSM_GROUPS):
        xg = xbc_ref[:, g * SSM_GROUP_W:(g + 1) * SSM_GROUP_W].astype(F32)
        b0 = SSM_INNER + g * SSM_STATE
        c0 = SSM_INNER + SSM_BC + g * SSM_STATE
        bg = xbc_ref[:, b0:b0 + SSM_STATE]
        cg = xbc_ref[:, c0:c0 + SSM_STATE]
        xdt = xg * _expand_heads(dt, g, lc)
        xdt_b = xdt.astype(BF16)
        cb = _dot_nt(cg, bg)
        y = jnp.zeros((lc, SSM_GROUP_W), F32)
        for j in range(SSM_HPG):
            hd = g * SSM_HPG + j
            seg = a_cum[:, hd:hd + 1] - a_cumT[hd:hd + 1, :]
            decay = jnp.exp(jnp.where(causal, seg, -jnp.inf))
            m = (cb * decay).astype(BF16)
            in_head = (lane2 >= j * SSM_HEAD_DIM) & (lane2 < (j + 1) * SSM_HEAD_DIM)
            y = y + _dot(m, jnp.where(in_head, xdt_b, jnp.zeros_like(xdt_b)))
        st = state[g]
        y = y + _dot(cg, st.astype(BF16)) * _expand_heads(e_cum, g, lc)
        xd2e = (xdt * _expand_heads(d2e, g, lc)).astype(BF16)
        state[g] = st * _expand_heads(cdec, g, 1) + _dot_tn(bg, xd2e)
        y = y + _expand_heads(dskip, g, 1) * xg
        zg = z_ref[:, g * SSM_GROUP_W:(g + 1) * SSM_GROUP_W].astype(F32)
        yz = y * (zg * _sigmoid(zg))
        yn = _rms(yz) * gn_ref[:, g * SSM_GROUP_W:(g + 1) * SSM_GROUP_W]
        yn_ref[:, g * SSM_GROUP_W:(g + 1) * SSM_GROUP_W] = yn.astype(BF16)


def _ssd_scan(z, xbc, dt, adt, adtT, d_skip, g_norm, bsz, seqlen, lc):
    n = z.shape[0]
    nc = seqlen // lc
    pad = jnp.zeros((LANES - SSM_HEADS,), F32)
    d_row = jnp.concatenate([d_skip, pad]).reshape(1, LANES)
    rows = lambda b, c: (b * nc + c, 0)
    vec = lambda b, c: (0, 0)
    return pl.pallas_call(
        functools.partial(_ssd_scan_kernel, lc=lc),
        out_shape=jax.ShapeDtypeStruct((n, SSM_INNER), BF16),
        grid=(bsz, nc),
        in_specs=[
            pl.BlockSpec((lc, SSM_CONV_DIM), rows),
            pl.BlockSpec((lc, SSM_INNER), rows),
            pl.BlockSpec((lc, LANES), rows),
            pl.BlockSpec((lc, LANES), rows),
            pl.BlockSpec((None, SSM_HEADS, lc), lambda b, c: (b, 0, c)),
            pl.BlockSpec((1, LANES), vec),
            pl.BlockSpec((1, SSM_INNER), vec),
        ],
        out_specs=pl.BlockSpec((lc, SSM_INNER), rows),
        scratch_shapes=[pltpu.VMEM((SSM_GROUPS, SSM_STATE, SSM_GROUP_W), F32)],
        compiler_params=_cparams("arbitrary", "arbitrary"),
        name="ssd_scan",
    )(xbc, z, dt, adt, adtT, d_row, g_norm.reshape(1, -1))


def kernel(x, c, positions, w_mod, b_mod, g_mix_norm, g_ffn_norm, mla_w_in, mla_g_q, mla_g_kv, mla_w_q_up, mla_w_kv_up, mla_w_out, ssm_w_in, ssm_conv_w, ssm_conv_b, ssm_dt_bias, ssm_a_log, ssm_d, ssm_g_norm, ssm_w_out, moe_w_router, moe_b_router, moe_w_gate_up, moe_b_gate_up, moe_w_down, moe_b_down, g_final):
    bsz, seqlen, d = x.shape
    depth = w_mod.shape[0]
    n = bsz * seqlen
    tm = min(512, seqlen)
    mod = _modulation(c, w_mod, b_mod)
    mod = mod.reshape(depth, 6, bsz, 1, d)
    pos_f = positions.astype(F32).reshape(n, 1)
    xc = x.reshape(n, d)
    for i in range(depth):
        sh1, sc1, gt1, sh2, sc2, gt2 = [mod[i, j] for j in range(6)]
        j = i // 2
        if i % 2 == 0:
            q, k, v = _mla_pre(xc, pos_f, sh1, sc1, g_mix_norm[i], mla_w_in[j], mla_g_q[j], mla_g_kv[j],
                               mla_w_q_up[j], mla_w_kv_up[j], bsz, seqlen, tm)
            o = _attention(q, k, v, tm).reshape(n, MLA_HEADS * V_HEAD)
            w_out = mla_w_out[j]
        else:
            tms = min(256, seqlen)
            z, xbc, dt, adt, adtT = _ssd_pre(xc, sh1, sc1, g_mix_norm[i], ssm_w_in[j], ssm_conv_w[j],
                                             ssm_conv_b[j], ssm_dt_bias[j], ssm_a_log[j], bsz, seqlen, tms)
            o = _ssd_scan(z, xbc, dt, adt, adtT, ssm_d[j], ssm_g_norm[j], bsz, seqlen, min(256, seqlen))
            w_out = ssm_w_out[j]
        x1, h2, meta, gates, counts = _post_mixer(xc, o, w_out, gt1, sh2, sc2, g_ffn_norm[i],
                                                  moe_w_router[i], moe_b_router[i], seqlen, tm)
        xc = _moe(x1, h2, meta, gates, counts, gt2, g_final, moe_w_gate_up[i], moe_b_gate_up[i],
                  moe_w_down[i], moe_b_down[i], seqlen, final=(i == depth - 1))
    return xc.reshape(bsz, seqlen, d)
```

```python
import functools
import math

import jax
import jax.numpy as jnp
from jax import lax
from jax.experimental import pallas as pl
from jax.experimental.pallas import tpu as pltpu

F32 = jnp.float32
BF16 = jnp.bfloat16
I32 = jnp.int32
U32 = jnp.uint32

NORM_EPS = 1e-6
MLA_HEADS = 8
Q_LORA = 256
KV_LORA = 256
QK_NOPE = 128
QK_ROPE = 64
V_HEAD = 128
QK_HEAD = QK_NOPE + QK_ROPE
ROPE_THETA = 10000.0
MLA_SCALE = QK_HEAD ** -0.5
SSM_HEAD_DIM = 64
SSM_GROUPS = 8
SSM_HPG = 4
SSM_HEADS = SSM_GROUPS * SSM_HPG
SSM_STATE = 128
SSM_CONV = 4
SSM_GROUP_W = SSM_HPG * SSM_HEAD_DIM
SSM_INNER = SSM_GROUPS * SSM_GROUP_W
SSM_BC = SSM_GROUPS * SSM_STATE
SSM_CONV_DIM = SSM_INNER + 2 * SSM_BC
N_EXPERTS = 32
TOP_K = 4
SWIGLU_LIMIT = 7.0
SWIGLU_ALPHA = 1.702
MOE_BLOCK = 256

LANES = 128
VMEM_LIMIT = 56 * 1024 * 1024


def _cparams(*sem):
    return pltpu.CompilerParams(dimension_semantics=tuple(sem), vmem_limit_bytes=VMEM_LIMIT)


def _dot(a, b):
    return jnp.dot(a, b, preferred_element_type=F32)


def _dot_nt(a, b):
    return lax.dot_general(a, b, (((1,), (1,)), ((), ())), preferred_element_type=F32)


def _dot_tn(a, b):
    return lax.dot_general(a, b, (((0,), (0,)), ((), ())), preferred_element_type=F32)


def _split3(a):
    hi = a.astype(BF16)
    r1 = a - hi.astype(F32)
    mid = r1.astype(BF16)
    lo = (r1 - mid.astype(F32)).astype(BF16)
    return hi, mid, lo


def _dot_f32ish(a, b):
    ah, am, _ = _split3(a)
    bh, bm, _ = _split3(b)
    return _dot(ah, bh) + (_dot(ah, bm) + _dot(am, bh))


def _pack_bf16_pair(lo, hi):
    lo_b = lax.bitcast_convert_type(lo.astype(BF16).astype(F32), U32)
    hi_b = lax.bitcast_convert_type(hi.astype(BF16).astype(F32), U32)
    return hi_b | lax.shift_right_logical(lo_b, jnp.uint32(16))


def _unpack_bf16_pair(w):
    lo = lax.bitcast_convert_type(lax.shift_left(w, jnp.uint32(16)), F32)
    hi = lax.bitcast_convert_type(w & jnp.uint32(0xFFFF0000), F32)
    return lo, hi


def _sigmoid(x):
    return 1.0 / (1.0 + jnp.exp(-x))


def _rms(x):
    return x * lax.rsqrt(jnp.mean(x * x, axis=-1, keepdims=True) + NORM_EPS)


def _mod_kernel(c_ref, w_ref, b_ref, o_ref):
    c = c_ref[...]
    cond = c * _sigmoid(c)
    o_ref[...] = _dot_f32ish(cond, w_ref[...]) + b_ref[...]


def _modulation(c, w_mod, b_mod):
    depth, d, d6 = w_mod.shape
    bsz = c.shape[0]
    nj = d6 // d
    return pl.pallas_call(
        _mod_kernel,
        out_shape=jax.ShapeDtypeStruct((depth, nj, bsz, d), F32),
        grid=(depth, nj),
        in_specs=[
            pl.BlockSpec((bsz, d), lambda l, j: (0, 0)),
            pl.BlockSpec((None, d, d), lambda l, j: (l, 0, j)),
            pl.BlockSpec((None, 1, d), lambda l, j: (l, 0, j)),
        ],
        out_specs=pl.BlockSpec((None, None, bsz, d), lambda l, j: (l, j, 0, 0)),
        compiler_params=_cparams("arbitrary", "arbitrary"),
        name="mod",
    )(c, w_mod, b_mod.reshape(depth, 1, d6))


def _mla_pre_kernel(x_ref, pos_ref, sh_ref, sc_ref, g_ref, win_ref, gq_ref, gkv_ref, wq_ref, wkv_ref,
                    invf_ref, q_ref, k_ref, v_ref):
    x = x_ref[...]
    h = _rms(x) * g_ref[...] * (1.0 + sc_ref[...]) + sh_ref[...]
    lat = _dot(h.astype(BF16), win_ref[...])
    q_lat = _rms(lat[:, :Q_LORA]) * gq_ref[...]
    kv_lat = _rms(lat[:, Q_LORA:Q_LORA + KV_LORA]) * gkv_ref[...]
    ang = pos_ref[...] * invf_ref[...]
    cs = jnp.cos(ang)
    sn = jnp.sin(ang)
    o = Q_LORA + KV_LORA
    k_rope = lat[:, o:o + LANES] * cs + lat[:, o + LANES:o + 2 * LANES] * sn
    qq = _dot(q_lat.astype(BF16), wq_ref[...])
    kv = _dot(kv_lat.astype(BF16), wkv_ref[...])
    rot0 = MLA_HEADS * 2 * LANES
    k_rope_b = k_rope[:, :QK_ROPE].astype(BF16)
    for hd in range(MLA_HEADS):
        c0 = hd * 2 * LANES
        q_nope = qq[:, c0:c0 + LANES] * MLA_SCALE
        q_rope = (qq[:, c0 + LANES:c0 + 2 * LANES] * cs
                  + qq[:, rot0 + hd * LANES:rot0 + (hd + 1) * LANES] * sn) * MLA_SCALE
        q_ref[hd, :, 0:QK_NOPE] = q_nope.astype(BF16)
        q_ref[hd, :, QK_NOPE:QK_HEAD] = q_rope[:, :QK_ROPE].astype(BF16)
        k_ref[hd, :, 0:QK_NOPE] = kv[:, c0:c0 + LANES].astype(BF16)
        k_ref[hd, :, QK_NOPE:QK_HEAD] = k_rope_b
        v_ref[hd] = kv[:, c0 + LANES:c0 + 2 * LANES].astype(BF16)


def _rot_half_cols(w):
    half = QK_ROPE // 2
    return jnp.concatenate([-w[..., half:], w[..., :half]], axis=-1)


def _mla_pre(x2d, pos_f, sh, sc, g, w_in, g_q, g_kv, w_q_up, w_kv_up, bsz, seqlen, tm):
    n, d = x2d.shape
    hh = MLA_HEADS
    o = Q_LORA + KV_LORA
    wr = w_in[:, o:o + QK_ROPE]
    zpad = jnp.zeros((d, LANES - QK_ROPE), F32)
    w_in_ext = jnp.concatenate([w_in[:, :o], wr, zpad, _rot_half_cols(wr), zpad], axis=1).astype(BF16)
    wq = w_q_up.reshape(Q_LORA, hh, QK_HEAD)
    zq = jnp.zeros((Q_LORA, hh, LANES - QK_ROPE), F32)
    wq_main = jnp.concatenate([wq, zq], axis=-1).reshape(Q_LORA, hh * 2 * LANES)
    wq_rot = jnp.concatenate([_rot_half_cols(wq[..., QK_NOPE:]), zq], axis=-1).reshape(Q_LORA, hh * LANES)
    wq_ext = jnp.concatenate([wq_main, wq_rot], axis=1).astype(BF16)
    inv_freq = 1.0 / (ROPE_THETA ** (jnp.arange(0, QK_ROPE, 2, dtype=F32) / QK_ROPE))
    invf = jnp.concatenate([inv_freq, inv_freq, jnp.zeros((LANES - QK_ROPE,), F32)]).reshape(1, LANES)
    tpb = seqlen // tm
    vec = lambda i: (0, 0)
    outs = pl.pallas_call(
        _mla_pre_kernel,
        out_shape=(
            jax.ShapeDtypeStruct((bsz, hh, seqlen, QK_HEAD), BF16),
            jax.ShapeDtypeStruct((bsz, hh, seqlen, QK_HEAD), BF16),
            jax.ShapeDtypeStruct((bsz, hh, seqlen, V_HEAD), BF16),
        ),
        grid=(n // tm,),
        in_specs=[
            pl.BlockSpec((tm, d), lambda i: (i, 0)),
            pl.BlockSpec((tm, 1), lambda i: (i, 0)),
            pl.BlockSpec((None, 1, d), lambda i: (i // tpb, 0, 0)),
            pl.BlockSpec((None, 1, d), lambda i: (i // tpb, 0, 0)),
            pl.BlockSpec((1, d), vec),
            pl.BlockSpec(w_in_ext.shape, vec),
            pl.BlockSpec((1, Q_LORA), vec),
            pl.BlockSpec((1, KV_LORA), vec),
            pl.BlockSpec(wq_ext.shape, vec),
            pl.BlockSpec((KV_LORA, hh * 2 * LANES), vec),
            pl.BlockSpec((1, LANES), vec),
        ],
        out_specs=(
            pl.BlockSpec((None, hh, tm, QK_HEAD), lambda i: (i // tpb, 0, i % tpb, 0)),
            pl.BlockSpec((None, hh, tm, QK_HEAD), lambda i: (i // tpb, 0, i % tpb, 0)),
            pl.BlockSpec((None, hh, tm, V_HEAD), lambda i: (i // tpb, 0, i % tpb, 0)),
        ),
        compiler_params=_cparams("arbitrary"),
        name="mla_pre",
    )(x2d, pos_f, sh, sc, g.reshape(1, d), w_in_ext, g_q.reshape(1, -1), g_kv.reshape(1, -1), wq_ext,
      w_kv_up.astype(BF16), invf)
    return outs


def _attn_kernel(q_ref, k_ref, v_ref, o_ref, m_scr, l_scr, acc_scr, *, tq):
    qi = pl.program_id(2)
    q = q_ref[...]
    m_scr[...] = jnp.full(m_scr.shape, -jnp.inf, F32)
    l_scr[...] = jnp.zeros(l_scr.shape, F32)
    acc_scr[...] = jnp.zeros(acc_scr.shape, F32)

    def block(j, masked):
        r0 = pl.multiple_of(j * tq, tq)
        k = k_ref[pl.ds(r0, tq), :]
        v = v_ref[pl.ds(r0, tq), :]
        s = _dot_nt(q, k)
        if masked:
            row = lax.broadcasted_iota(I32, (tq, tq), 0)
            col = lax.broadcasted_iota(I32, (tq, tq), 1)
            s = jnp.where(row >= col, s, -jnp.inf)
        m_prev = m_scr[...]
        m_new = jnp.maximum(m_prev, jnp.max(s, axis=-1, keepdims=True))
        alpha = jnp.exp(m_prev - m_new)
        p = jnp.exp(s - m_new)
        l_scr[...] = alpha * l_scr[...] + jnp.sum(p, axis=-1, keepdims=True)
        acc_scr[...] = alpha * acc_scr[...] + _dot(p.astype(BF16), v)
        m_scr[...] = m_new

    def body(j, carry):
        block(j, False)
        return carry

    lax.fori_loop(0, qi, body, 0)
    block(qi, True)
    o_ref[...] = (acc_scr[...] / l_scr[...]).astype(o_ref.dtype)


def _attention(q, k, v, tq):
    bsz, hh, seqlen, _ = q.shape
    return pl.pallas_call(
        functools.partial(_attn_kernel, tq=tq),
        out_shape=jax.ShapeDtypeStruct((bsz, seqlen, hh * V_HEAD), BF16),
        grid=(bsz, hh, seqlen // tq),
        in_specs=[
            pl.BlockSpec((None, None, tq, QK_HEAD), lambda b, h, i: (b, h, i, 0)),
            pl.BlockSpec((None, None, seqlen, QK_HEAD), lambda b, h, i: (b, h, 0, 0)),
            pl.BlockSpec((None, None, seqlen, V_HEAD), lambda b, h, i: (b, h, 0, 0)),
        ],
        out_specs=pl.BlockSpec((None, tq, V_HEAD), lambda b, h, i: (b, i, h)),
        scratch_shapes=[
            pltpu.VMEM((tq, 1), F32),
            pltpu.VMEM((tq, 1), F32),
            pltpu.VMEM((tq, V_HEAD), F32),
        ],
        compiler_params=_cparams("arbitrary", "arbitrary", "arbitrary"),
        name="attn",
    )(q, k, v)


def _post_mixer_kernel(x_ref, o_ref, wout_ref, gt1_ref, sh_ref, sc_ref, g_ref, wr_ref, br_ref,
                       x1_ref, h2_ref, meta_ref, gate_ref, cnt_ref, carry_scr, *, tm):
    i = pl.program_id(0)

    @pl.when(i == 0)
    def _():
        carry_scr[...] = jnp.zeros(carry_scr.shape, F32)

    y = _dot(o_ref[...], wout_ref[...])
    x1 = x_ref[...] + gt1_ref[...] * y
    x1_ref[...] = x1
    h2 = _rms(x1) * g_ref[...] * (1.0 + sc_ref[...]) + sh_ref[...]
    half = h2.shape[1] // 2
    h2_ref[...] = _pack_bf16_pair(h2[:, :half], h2[:, half:])
    logits = _dot_f32ish(h2, wr_ref[...]) + br_ref[...]
    lane = lax.broadcasted_iota(I32, (tm, LANES), 1).astype(F32)
    work = logits
    idxs, vals = [], []
    for _ in range(TOP_K):
        mx = jnp.max(work, axis=-1, keepdims=True)
        idx = jnp.min(jnp.where(work == mx, lane, float(LANES)), axis=-1, keepdims=True)
        idxs.append(idx)
        vals.append(mx)
        work = jnp.where(lane == idx, -jnp.inf, work)
    exps = [jnp.exp(vk - vals[0]) for vk in vals]
    denom = exps[0] + exps[1] + exps[2] + exps[3]
    onehot = jnp.zeros((tm, LANES), F32)
    for idx in idxs:
        onehot = onehot + jnp.where(lane == idx, 1.0, 0.0)
    row = lax.broadcasted_iota(I32, (tm, tm), 0)
    col = lax.broadcasted_iota(I32, (tm, tm), 1)
    ltri = jnp.where(row > col, 1.0, 0.0).astype(BF16)
    cum = _dot(ltri, onehot.astype(BF16)) + carry_scr[...]
    meta = jnp.zeros((tm, LANES), F32)
    gates = jnp.zeros((tm, LANES), F32)
    for kk in range(TOP_K):
        rank = jnp.sum(jnp.where(lane == idxs[kk], cum, 0.0), axis=-1, keepdims=True)
        meta = jnp.where(lane == float(kk), idxs[kk], meta)
        meta = jnp.where(lane == float(TOP_K + kk), rank, meta)
        gates = jnp.where(lane == float(kk), exps[kk] / denom, gates)
    meta_ref[...] = jnp.transpose(meta)[:2 * TOP_K, :].astype(I32)
    gate_ref[...] = gates
    carry = carry_scr[...] + jnp.sum(onehot, axis=0, keepdims=True)
    carry_scr[...] = carry
    cnt_ref[...] = carry.astype(I32)


def _post_mixer(x2d, o2d, w_out, gt1, sh2, sc2, g_ffn, w_router, b_router, seqlen, tm):
    n, d = x2d.shape
    kdim = o2d.shape[1]
    tpb = seqlen // tm
    wr = jnp.concatenate([w_router, jnp.zeros((d, LANES - N_EXPERTS), F32)], axis=1)
    br = jnp.concatenate([b_router, jnp.full((LANES - N_EXPERTS,), -1e30, F32)]).reshape(1, LANES)
    vec = lambda i: (0, 0)
    bvec = lambda i: (i // tpb, 0, 0)
    return pl.pallas_call(
        functools.partial(_post_mixer_kernel, tm=tm),
        out_shape=(
            jax.ShapeDtypeStruct((n, d), F32),
            jax.ShapeDtypeStruct((n, d // 2), U32),
            jax.ShapeDtypeStruct((2 * TOP_K, n), I32),
            jax.ShapeDtypeStruct((n, LANES), F32),
            jax.ShapeDtypeStruct((1, LANES), I32),
        ),
        grid=(n // tm,),
        in_specs=[
            pl.BlockSpec((tm, d), lambda i: (i, 0)),
            pl.BlockSpec((tm, kdim), lambda i: (i, 0)),
            pl.BlockSpec((kdim, d), vec),
            pl.BlockSpec((None, 1, d), bvec),
            pl.BlockSpec((None, 1, d), bvec),
            pl.BlockSpec((None, 1, d), bvec),
            pl.BlockSpec((1, d), vec),
            pl.BlockSpec((d, LANES), vec),
            pl.BlockSpec((1, LANES), vec),
        ],
        out_specs=(
            pl.BlockSpec((tm, d), lambda i: (i, 0)),
            pl.BlockSpec((tm, d // 2), lambda i: (i, 0)),
            pl.BlockSpec((2 * TOP_K, tm), lambda i: (0, i)),
            pl.BlockSpec((tm, LANES), lambda i: (i, 0)),
            pl.BlockSpec((1, LANES), vec),
        ),
        scratch_shapes=[pltpu.VMEM((1, LANES), F32)],
        compiler_params=_cparams("arbitrary"),
        name="post_mixer",
    )(x2d, o2d, w_out.astype(BF16), gt1, sh2, sc2, g_ffn.reshape(1, d), wr, br)


ROW_UNROLL = 8


def _row_copy(src, s, dst, t, sem):
    return pltpu.make_async_copy(src.at[pl.ds(s, 1)], dst.at[pl.ds(t, 1)], sem)


def _dispatch_kernel(pstart_ref, cnt_ref, meta_ref, h_ref, xs_hbm, zbuf, sem, zsem, *, tm):
    i = pl.program_id(0)

    @pl.when(i == 0)
    def _():
        zbuf[...] = jnp.zeros(zbuf.shape, U32)

        def per_expert(e, carry):
            lo = pstart_ref[e] + cnt_ref[e]
            hi = pstart_ref[e + 1]

            def start(j, c):
                _row_copy(zbuf, 0, xs_hbm, j, zsem).start()
                return c

            def wait(j, c):
                _row_copy(zbuf, 0, xs_hbm, j, zsem).wait()
                return c

            lax.fori_loop(lo, hi, start, 0)
            lax.fori_loop(lo, hi, wait, 0)
            return carry

        lax.fori_loop(0, N_EXPERTS, per_expert, 0)

    def start(tt, c):
        for u in range(ROW_UNROLL):
            t = tt * ROW_UNROLL + u
            for kk in range(TOP_K):
                dst = pstart_ref[meta_ref[kk, t]] + meta_ref[TOP_K + kk, t]
                _row_copy(h_ref, t, xs_hbm, dst, sem).start()
        return c

    def wait(tt, c):
        for u in range(ROW_UNROLL * TOP_K):
            _row_copy(h_ref, 0, xs_hbm, 0, sem).wait()
        return c

    lax.fori_loop(0, tm // ROW_UNROLL, start, 0)
    lax.fori_loop(0, tm // ROW_UNROLL, wait, 0)


def _dispatch(h2p, meta_t, pstart, counts, n_rows, tm):
    n, dh = h2p.shape
    grid_spec = pltpu.PrefetchScalarGridSpec(
        num_scalar_prefetch=2,
        grid=(n // tm,),
        in_specs=[
            pl.BlockSpec((2 * TOP_K, tm), lambda i, ps, cn: (0, i), memory_space=pltpu.SMEM),
            pl.BlockSpec((tm, dh), lambda i, ps, cn: (i, 0)),
        ],
        out_specs=pl.BlockSpec(memory_space=pl.ANY),
        scratch_shapes=[
            pltpu.VMEM((8, dh), U32),
            pltpu.SemaphoreType.DMA,
            pltpu.SemaphoreType.DMA,
        ],
    )
    return pl.pallas_call(
        functools.partial(_dispatch_kernel, tm=tm),
        out_shape=jax.ShapeDtypeStruct((n_rows, dh), U32),
        grid_spec=grid_spec,
        compiler_params=pltpu.CompilerParams(dimension_semantics=("arbitrary",), vmem_limit_bytes=VMEM_LIMIT,
                                             has_side_effects=True),
        name="dispatch",
    )(pstart, counts, meta_t, h2p)


def _ffn_kernel(be_ref, nu_ref, xs_ref, wgu_ref, bgu_ref, wd_ref, bd_ref, ys_ref, wgu_b, wd_b, *, ff):
    i = pl.program_id(0)
    prev = be_ref[jnp.maximum(i - 1, 0)]
    changed = jnp.logical_or(i == 0, be_ref[i] != prev)

    @pl.when(changed)
    def _():
        wgu_b[...] = wgu_ref[...].astype(BF16)
        wd_b[...] = wd_ref[...].astype(BF16)

    @pl.when(i < nu_ref[0])
    def _():
        x_lo, x_hi = _unpack_bf16_pair(xs_ref[...])
        dh = x_lo.shape[1]
        gu = (_dot(x_lo.astype(BF16), wgu_b[0:dh, :]) + _dot(x_hi.astype(BF16), wgu_b[dh:2 * dh, :])
              + bgu_ref[...])
        g = jnp.minimum(gu[:, :ff], SWIGLU_LIMIT)
        lin = jnp.clip(gu[:, ff:], -SWIGLU_LIMIT, SWIGLU_LIMIT)
        act = g * _sigmoid(SWIGLU_ALPHA * g) * (lin + 1.0)
        y = _dot(act.astype(BF16), wd_b[...]) + bd_ref[...]
        ys_ref[...] = _pack_bf16_pair(y[:, :dh], y[:, dh:])

    @pl.when(i >= nu_ref[0])
    def _():
        ys_ref[...] = jnp.zeros(ys_ref.shape, U32)


def _ffn(xs, blk_e, n_used, w_gate_up, b_gate_up, w_down, b_down):
    n_rows, dh = xs.shape
    ne, d, ff2 = w_gate_up.shape
    ff = ff2 // 2
    n_blk = n_rows // MOE_BLOCK
    grid_spec = pltpu.PrefetchScalarGridSpec(
        num_scalar_prefetch=2,
        grid=(n_blk,),
        in_specs=[
            pl.BlockSpec((MOE_BLOCK, dh), lambda i, be, nu: (jnp.minimum(i, nu[0] - 1), 0)),
            pl.BlockSpec((None, d, ff2), lambda i, be, nu: (be[i], 0, 0)),
            pl.BlockSpec((None, 1, ff2), lambda i, be, nu: (be[i], 0, 0)),
            pl.BlockSpec((None, ff, d), lambda i, be, nu: (be[i], 0, 0)),
            pl.BlockSpec((None, 1, d), lambda i, be, nu: (be[i], 0, 0)),
        ],
        out_specs=pl.BlockSpec((MOE_BLOCK, dh), lambda i, be, nu: (i, 0)),
        scratch_shapes=[pltpu.VMEM((d, ff2), BF16), pltpu.VMEM((ff, d), BF16)],
    )
    return pl.pallas_call(
        functools.partial(_ffn_kernel, ff=ff),
        out_shape=jax.ShapeDtypeStruct((n_rows, dh), U32),
        grid_spec=grid_spec,
        compiler_params=_cparams("arbitrary"),
        name="ffn",
    )(blk_e, n_used, xs, w_gate_up, b_gate_up.reshape(ne, 1, ff2), w_down, b_down.reshape(ne, 1, d))


def _combine_kernel(pstart_ref, meta_ref, x1_ref, gate_ref, gt2_ref, gfin_ref, ys_hbm, o_ref, buf, sem,
                    *, tm, final):
    def start(tt, c):
        for u in range(ROW_UNROLL):
            t = tt * ROW_UNROLL + u
            for kk in range(TOP_K):
                src = pstart_ref[meta_ref[kk, t]] + meta_ref[TOP_K + kk, t]
                _row_copy(ys_hbm, src, buf.at[kk], t, sem).start()
        return c

    def wait(tt, c):
        for u in range(ROW_UNROLL * TOP_K):
            _row_copy(ys_hbm, 0, buf.at[0], 0, sem).wait()
        return c

    lax.fori_loop(0, tm // ROW_UNROLL, start, 0)
    lax.fori_loop(0, tm // ROW_UNROLL, wait, 0)
    gates = gate_ref[...]
    acc_lo = acc_hi = None
    for kk in range(TOP_K):
        lo, hi = _unpack_bf16_pair(buf[kk])
        gk = gates[:, kk:kk + 1]
        acc_lo = gk * lo if acc_lo is None else acc_lo + gk * lo
        acc_hi = gk * hi if acc_hi is None else acc_hi + gk * hi
    x2 = x1_ref[...] + gt2_ref[...] * jnp.concatenate([acc_lo, acc_hi], axis=1)
    if final:
        x2 = _rms(x2) * gfin_ref[...]
    o_ref[...] = x2


def _combine(x1, gates, gt2, g_final, ys, meta_t, pstart, seqlen, tm, final):
    n, d = x1.shape
    dh = ys.shape[1]
    tpb = seqlen // tm
    grid_spec = pltpu.PrefetchScalarGridSpec(
        num_scalar_prefetch=1,
        grid=(n // tm,),
        in_specs=[
            pl.BlockSpec((2 * TOP_K, tm), lambda i, ps: (0, i), memory_space=pltpu.SMEM),
            pl.BlockSpec((tm, d), lambda i, ps: (i, 0)),
            pl.BlockSpec((tm, LANES), lambda i, ps: (i, 0)),
            pl.BlockSpec((None, 1, d), lambda i, ps: (i // tpb, 0, 0)),
            pl.BlockSpec((1, d), lambda i, ps: (0, 0)),
            pl.BlockSpec(memory_space=pl.ANY),
        ],
        out_specs=pl.BlockSpec((tm, d), lambda i, ps: (i, 0)),
        scratch_shapes=[pltpu.VMEM((TOP_K, tm, dh), U32), pltpu.SemaphoreType.DMA],
    )
    return pl.pallas_call(
        functools.partial(_combine_kernel, tm=tm, final=final),
        out_shape=jax.ShapeDtypeStruct((n, d), F32),
        grid_spec=grid_spec,
        compiler_params=_cparams("arbitrary"),
        name="combine",
    )(pstart, meta_t, x1, gates, gt2, g_final.reshape(1, d), ys)


def _moe(x1, h2p, meta_t, gates, counts, gt2, g_final, w_gate_up, b_gate_up, w_down, b_down, seqlen, final):
    n, d = x1.shape
    n_pair = n * TOP_K
    n_rows = -(-n_pair // MOE_BLOCK) * MOE_BLOCK + N_EXPERTS * MOE_BLOCK
    n_blk = n_rows // MOE_BLOCK
    cnt = counts[0, :N_EXPERTS]
    padded = (cnt + MOE_BLOCK - 1) // MOE_BLOCK * MOE_BLOCK
    pad_end = jnp.cumsum(padded)
    pstart = jnp.concatenate([pad_end - padded, pad_end[-1:]]).astype(I32)
    n_used = (pad_end[-1:] // MOE_BLOCK).astype(I32)
    blk_start = jnp.arange(n_blk, dtype=I32) * MOE_BLOCK
    blk_e = jnp.minimum(jnp.sum(blk_start[:, None] >= pad_end[None, :], axis=1), N_EXPERTS - 1).astype(I32)
    tm = min(256, seqlen)
    xs = _dispatch(h2p, meta_t, pstart, cnt, n_rows, tm)
    ys = _ffn(xs, blk_e, n_used, w_gate_up, b_gate_up, w_down, b_down)
    return _combine(x1, gates, gt2, g_final, ys, meta_t, pstart, seqlen, tm, final)


def _softplus(x):
    return jnp.maximum(x, 0.0) + jnp.log(1.0 + jnp.exp(-jnp.abs(x)))


def _ssd_pre_kernel(x_ref, sh_ref, sc_ref, g_ref, wz_ref, wx_ref, wdt_ref, wdtT_ref, cw_ref, cb_ref,
                    dtb_ref, dtbT_ref, a_ref, aT_ref,
                    z_ref, xbc_ref, dt_ref, adt_ref, adtT_ref, ubuf, *, tm, tpb, cchunk):
    i = pl.program_id(0)
    x = x_ref[...]
    h = _rms(x) * g_ref[...] * (1.0 + sc_ref[...]) + sh_ref[...]
    hb = h.astype(BF16)
    z_ref[...] = _dot(hb, wz_ref[...]).astype(BF16)
    dt = _softplus(_dot(hb, wdt_ref[...]) + dtb_ref[...])
    dt_ref[...] = dt
    adt_ref[...] = dt * a_ref[...]
    dtT = _softplus(_dot_nt(wdtT_ref[...], hb) + dtbT_ref[...])
    adtT_ref[...] = dtT * aT_ref[...]

    @pl.when(i % tpb == 0)
    def _():
        ubuf[0:8, :] = jnp.zeros((8, ubuf.shape[1]), F32)

    for c0 in range(0, SSM_CONV_DIM, cchunk):
        cols = slice(c0, c0 + cchunk)
        ubuf[8:8 + tm, cols] = _dot(hb, wx_ref[:, cols])
        acc = cb_ref[:, cols] + cw_ref[0:1, cols] * ubuf[pl.ds(5, tm), cols]
        for kk in range(1, SSM_CONV):
            acc = acc + cw_ref[kk:kk + 1, cols] * ubuf[pl.ds(5 + kk, tm), cols]
        xbc_ref[:, cols] = (acc * _sigmoid(acc)).astype(BF16)
        ubuf[0:8, cols] = ubuf[tm:tm + 8, cols]


def _ssd_pre(x2d, sh, sc, g, w_in, conv_w, conv_b, dt_bias, a_log, bsz, seqlen, tm):
    n, d = x2d.shape
    tpb = seqlen // tm
    nh = SSM_HEADS
    wz = w_in[:, :SSM_INNER].astype(BF16)
    wx = w_in[:, SSM_INNER:SSM_INNER + SSM_CONV_DIM].astype(BF16)
    wdt_raw = w_in[:, SSM_INNER + SSM_CONV_DIM:]
    wdt = jnp.concatenate([wdt_raw, jnp.zeros((d, LANES - nh), F32)], axis=1).astype(BF16)
    wdtT = wdt_raw.T.astype(BF16)
    pad = jnp.zeros((LANES - nh,), F32)
    dtb = jnp.concatenate([dt_bias, pad]).reshape(1, LANES)
    a_neg = -jnp.exp(a_log.astype(F32))
    a_row = jnp.concatenate([a_neg, pad]).reshape(1, LANES)
    vec = lambda i: (0, 0)
    bvec = lambda i: (i // tpb, 0, 0)
    return pl.pallas_call(
        functools.partial(_ssd_pre_kernel, tm=tm, tpb=tpb, cchunk=1024),
        out_shape=(
            jax.ShapeDtypeStruct((n, SSM_INNER), BF16),
            jax.ShapeDtypeStruct((n, SSM_CONV_DIM), BF16),
            jax.ShapeDtypeStruct((n, LANES), F32),
            jax.ShapeDtypeStruct((n, LANES), F32),
            jax.ShapeDtypeStruct((bsz, nh, seqlen), F32),
        ),
        grid=(n // tm,),
        in_specs=[
            pl.BlockSpec((tm, d), lambda i: (i, 0)),
            pl.BlockSpec((None, 1, d), bvec),
            pl.BlockSpec((None, 1, d), bvec),
            pl.BlockSpec((1, d), vec),
            pl.BlockSpec((d, SSM_INNER), vec),
            pl.BlockSpec((d, SSM_CONV_DIM), vec),
            pl.BlockSpec((d, LANES), vec),
            pl.BlockSpec((nh, d), vec),
            pl.BlockSpec((SSM_CONV, SSM_CONV_DIM), vec),
            pl.BlockSpec((1, SSM_CONV_DIM), vec),
            pl.BlockSpec((1, LANES), vec),
            pl.BlockSpec((nh, 1), vec),
            pl.BlockSpec((1, LANES), vec),
            pl.BlockSpec((nh, 1), vec),
        ],
        out_specs=(
            pl.BlockSpec((tm, SSM_INNER), lambda i: (i, 0)),
            pl.BlockSpec((tm, SSM_CONV_DIM), lambda i: (i, 0)),
            pl.BlockSpec((tm, LANES), lambda i: (i, 0)),
            pl.BlockSpec((tm, LANES), lambda i: (i, 0)),
            pl.BlockSpec((None, nh, tm), lambda i: (i // tpb, 0, i % tpb)),
        ),
        scratch_shapes=[pltpu.VMEM((tm + 8, SSM_CONV_DIM), F32)],
        compiler_params=_cparams("arbitrary"),
        name="ssd_pre",
    )(x2d, sh, sc, g.reshape(1, d), wz, wx, wdt, wdtT, conv_w, conv_b.reshape(1, -1), dtb,
      dt_bias.reshape(nh, 1), a_row, a_neg.reshape(nh, 1))


def _expand_heads(v, g, rows):
    lane = lax.broadcasted_iota(I32, (rows, LANES), 1)
    lo = lane < SSM_HEAD_DIM
    parts = []
    for j in range(0, SSM_HPG, 2):
        h0 = g * SSM_HPG + j
        parts.append(jnp.where(lo, v[:, h0:h0 + 1], v[:, h0 + 1:h0 + 2]))
    return jnp.concatenate(parts, axis=1)


def _ssd_scan_kernel(xbc_ref, z_ref, dt_ref, adt_ref, adtT_ref, d_ref, gn_ref, yn_ref, state, *, lc):
    c = pl.program_id(1)

    @pl.when(c == 0)
    def _():
        state[...] = jnp.zeros(state.shape, F32)

    row = lax.broadcasted_iota(I32, (lc, lc), 0)
    col = lax.broadcasted_iota(I32, (lc, lc), 1)
    causal = row >= col
    tri = jnp.where(causal, 1.0, 0.0).astype(BF16)
    triT = jnp.where(row <= col, 1.0, 0.0).astype(BF16)
    ah, am, al = _split3(adt_ref[...])
    a_cum = _dot(tri, ah) + (_dot(tri, am) + _dot(tri, al))
    th, tmid, tl = _split3(adtT_ref[...])
    a_cumT = _dot(th, triT) + (_dot(tmid, triT) + _dot(tl, triT))
    dt = dt_ref[...]
    a_last = a_cum[lc - 1:lc, :]
    e_cum = jnp.exp(a_cum)
    d2e = jnp.exp(a_last - a_cum)
    cdec = jnp.exp(a_last)
    dskip = d_ref[...]
    lane2 = lax.broadcasted_iota(I32, (lc, SSM_GROUP_W), 1)
    for g in range(SSM_GROUPS):
        xg = xbc_ref[:, g * SSM_GROUP_W:(g + 1) * SSM_GROUP_W].astype(F32)
        b0 = SSM_INNER + g * SSM_STATE
        c0 = SSM_INNER + SSM_BC + g * SSM_STATE
        bg = xbc_ref[:, b0:b0 + SSM_STATE]
        cg = xbc_ref[:, c0:c0 + SSM_STATE]
        xdt = xg * _expand_heads(dt, g, lc)
        xdt_b = xdt.astype(BF16)
        cb = _dot_nt(cg, bg)
        y = jnp.zeros((lc, SSM_GROUP_W), F32)
        for j in range(SSM_HPG):
            hd = g * SSM_HPG + j
            seg = a_cum[:, hd:hd + 1] - a_cumT[hd:hd + 1, :]
            decay = jnp.exp(jnp.where(causal, seg, -jnp.inf))
            m = (cb * decay).astype(BF16)
            in_head = (lane2 >= j * SSM_HEAD_DIM) & (lane2 < (j + 1) * SSM_HEAD_DIM)
            y = y + _dot(m, jnp.where(in_head, xdt_b, jnp.zeros_like(xdt_b)))
        st = state[g]
        y = y + _dot(cg, st.astype(BF16)) * _expand_heads(e_cum, g, lc)
        xd2e = (xdt * _expand_heads(d2e, g, lc)).astype(BF16)
        state[g] = st * _expand_heads(cdec, g, 1) + _dot_tn(bg, xd2e)
        y = y + _expand_heads(dskip, g, 1) * xg
        zg = z_ref[:, g * SSM_GROUP_W:(g + 1) * SSM_GROUP_W].astype(F32)
        yz = y * (zg * _sigmoid(zg))
        yn = _rms(yz) * gn_ref[:, g * SSM_GROUP_W:(g + 1) * SSM_GROUP_W]
        yn_ref[:, g * SSM_GROUP_W:(g + 1) * SSM_GROUP_W] = yn.astype(BF16)


def _ssd_scan(z, xbc, dt, adt, adtT, d_skip, g_norm, bsz, seqlen, lc):
    n = z.shape[0]
    nc = seqlen // lc
    pad = jnp.zeros((LANES - SSM_HEADS,), F32)
    d_row = jnp.concatenate([d_skip, pad]).reshape(1, LANES)
    rows = lambda b, c: (b * nc + c, 0)
    vec = lambda b, c: (0, 0)
    return pl.pallas_call(
        functools.partial(_ssd_scan_kernel, lc=lc),
        out_shape=jax.ShapeDtypeStruct((n, SSM_INNER), BF16),
        grid=(bsz, nc),
        in_specs=[
            pl.BlockSpec((lc, SSM_CONV_DIM), rows),
            pl.BlockSpec((lc, SSM_INNER), rows),
            pl.BlockSpec((lc, LANES), rows),
            pl.BlockSpec((lc, LANES), rows),
            pl.BlockSpec((None, SSM_HEADS, lc), lambda b, c: (b, 0, c)),
            pl.BlockSpec((1, LANES), vec),
            pl.BlockSpec((1, SSM_INNER), vec),
        ],
        out_specs=pl.BlockSpec((lc, SSM_INNER), rows),
        scratch_shapes=[pltpu.VMEM((SSM_GROUPS, SSM_STATE, SSM_GROUP_W), F32)],
        compiler_params=_cparams("arbitrary", "arbitrary"),
        name="ssd_scan",
    )(xbc, z, dt, adt, adtT, d_row, g_norm.reshape(1, -1))


def kernel(x, c, positions, w_mod, b_mod, g_mix_norm, g_ffn_norm, mla_w_in, mla_g_q, mla_g_kv, mla_w_q_up, mla_w_kv_up, mla_w_out, ssm_w_in, ssm_conv_w, ssm_conv_b, ssm_dt_bias, ssm_a_log, ssm_d, ssm_g_norm, ssm_w_out, moe_w_router, moe_b_router, moe_w_gate_up, moe_b_gate_up, moe_w_down, moe_b_down, g_final):
    bsz, seqlen, d = x.shape
    depth = w_mod.shape[0]
    n = bsz * seqlen
    tm = min(512, seqlen)
    mod = _modulation(c, w_mod, b_mod)
    mod = mod.reshape(depth, 6, bsz, 1, d)
    pos_f = positions.astype(F32).reshape(n, 1)
    xc = x.reshape(n, d)
    for i in range(depth):
        sh1, sc1, gt1, sh2, sc2, gt2 = [mod[i, j] for j in range(6)]
        j = i // 2
        if i % 2 == 0:
            q, k, v = _mla_pre(xc, pos_f, sh1, sc1, g_mix_norm[i], mla_w_in[j], mla_g_q[j], mla_g_kv[j],
                               mla_w_q_up[j], mla_w_kv_up[j], bsz, seqlen, tm)
            o = _attention(q, k, v, tm).reshape(n, MLA_HEADS * V_HEAD)
            w_out = mla_w_out[j]
        else:
            tms = min(256, seqlen)
            z, xbc, dt, adt, adtT = _ssd_pre(xc, sh1, sc1, g_mix_norm[i], ssm_w_in[j], ssm_conv_w[j],
                                             ssm_conv_b[j], ssm_dt_bias[j], ssm_a_log[j], bsz, seqlen, tms)
            o = _ssd_scan(z, xbc, dt, adt, adtT, ssm_d[j], ssm_g_norm[j], bsz, seqlen, min(256, seqlen))
            w_out = ssm_w_out[j]
        x1, h2, meta, gates, counts = _post_mixer(xc, o, w_out, gt1, sh2, sc2, g_ffn_norm[i],
                                                  moe_w_router[i], moe_b_router[i], seqlen, tm)
        xc = _moe(x1, h2, meta, gates, counts, gt2, g_final, moe_w_gate_up[i], moe_b_gate_up[i],
                  moe_w_down[i], moe_b_down[i], seqlen, final=(i == depth - 1))
    return xc.reshape(bsz, seqlen, d)
```

```python
import functools
import math

import jax
import jax.numpy as jnp
from jax import lax
from jax.experimental import pallas as pl
from jax.experimental.pallas import tpu as pltpu

F32 = jnp.float32
BF16 = jnp.bfloat16
I32 = jnp.int32
U32 = jnp.uint32

NORM_EPS = 1e-6
MLA_HEADS = 8
Q_LORA = 256
KV_LORA = 256
QK_NOPE = 128
QK_ROPE = 64
V_HEAD = 128
QK_HEAD = QK_NOPE + QK_ROPE
ROPE_THETA = 10000.0
MLA_SCALE = QK_HEAD ** -0.5
SSM_HEAD_DIM = 64
SSM_GROUPS = 8
SSM_HPG = 4
SSM_HEADS = SSM_GROUPS * SSM_HPG
SSM_STATE = 128
SSM_CONV = 4
SSM_GROUP_W = SSM_HPG * SSM_HEAD_DIM
SSM_INNER = SSM_GROUPS * SSM_GROUP_W
SSM_BC = SSM_GROUPS * SSM_STATE
SSM_CONV_DIM = SSM_INNER + 2 * SSM_BC
N_EXPERTS = 32
TOP_K = 4
SWIGLU_LIMIT = 7.0
SWIGLU_ALPHA = 1.702
MOE_BLOCK = 256

LANES = 128
VMEM_LIMIT = 56 * 1024 * 1024


def _cparams(*sem):
    return pltpu.CompilerParams(dimension_semantics=tuple(sem), vmem_limit_bytes=VMEM_LIMIT)


def _dot(a, b):
    return jnp.dot(a, b, preferred_element_type=F32)


def _dot_nt(a, b):
    return lax.dot_general(a, b, (((1,), (1,)), ((), ())), preferred_element_type=F32)


def _dot_tn(a, b):
    return lax.dot_general(a, b, (((0,), (0,)), ((), ())), preferred_element_type=F32)


def _split3(a):
    hi = a.astype(BF16)
    r1 = a - hi.astype(F32)
    mid = r1.astype(BF16)
    lo = (r1 - mid.astype(F32)).astype(BF16)
    return hi, mid, lo


def _dot_f32ish(a, b):
    ah, am, _ = _split3(a)
    bh, bm, _ = _split3(b)
    return _dot(ah, bh) + (_dot(ah, bm) + _dot(am, bh))


def _pack_bf16_pair(lo, hi):
    lo_b = lax.bitcast_convert_type(lo.astype(BF16).astype(F32), U32)
    hi_b = lax.bitcast_convert_type(hi.astype(BF16).astype(F32), U32)
    return hi_b | lax.shift_right_logical(lo_b, jnp.uint32(16))


def _unpack_bf16_pair(w):
    lo = lax.bitcast_convert_type(lax.shift_left(w, jnp.uint32(16)), F32)
    hi = lax.bitcast_convert_type(w & jnp.uint32(0xFFFF0000), F32)
    return lo, hi


def _sigmoid(x):
    return 1.0 / (1.0 + jnp.exp(-x))


def _rms(x):
    return x * lax.rsqrt(jnp.mean(x * x, axis=-1, keepdims=True) + NORM_EPS)


def _mod_kernel(c_ref, w_ref, b_ref, o_ref):
    c = c_ref[...]
    cond = c * _sigmoid(c)
    o_ref[...] = _dot_f32ish(cond, w_ref[...]) + b_ref[...]


def _modulation(c, w_mod, b_mod):
    depth, d, d6 = w_mod.shape
    bsz = c.shape[0]
    nj = d6 // d
    return pl.pallas_call(
        _mod_kernel,
        out_shape=jax.ShapeDtypeStruct((depth, nj, bsz, d), F32),
        grid=(depth, nj),
        in_specs=[
            pl.BlockSpec((bsz, d), lambda l, j: (0, 0)),
            pl.BlockSpec((None, d, d), lambda l, j: (l, 0, j)),
            pl.BlockSpec((None, 1, d), lambda l, j: (l, 0, j)),
        ],
        out_specs=pl.BlockSpec((None, None, bsz, d), lambda l, j: (l, j, 0, 0)),
        compiler_params=_cparams("arbitrary", "arbitrary"),
        name="mod",
    )(c, w_mod, b_mod.reshape(depth, 1, d6))


def _mla_pre_kernel(x_ref, pos_ref, sh_ref, sc_ref, g_ref, win_ref, gq_ref, gkv_ref, wq_ref, wkv_ref,
                    invf_ref, q_ref, k_ref, v_ref):
    x = x_ref[...]
    h = _rms(x) * g_ref[...] * (1.0 + sc_ref[...]) + sh_ref[...]
    lat = _dot(h.astype(BF16), win_ref[...])
    q_lat = _rms(lat[:, :Q_LORA]) * gq_ref[...]
    kv_lat = _rms(lat[:, Q_LORA:Q_LORA + KV_LORA]) * gkv_ref[...]
    ang = pos_ref[...] * invf_ref[...]
    cs = jnp.cos(ang)
    sn = jnp.sin(ang)
    o = Q_LORA + KV_LORA
    k_rope = lat[:, o:o + LANES] * cs + lat[:, o + LANES:o + 2 * LANES] * sn
    qq = _dot(q_lat.astype(BF16), wq_ref[...])
    kv = _dot(kv_lat.astype(BF16), wkv_ref[...])
    rot0 = MLA_HEADS * 2 * LANES
    k_rope_b = k_rope[:, :QK_ROPE].astype(BF16)
    for hd in range(MLA_HEADS):
        c0 = hd * 2 * LANES
        q_nope = qq[:, c0:c0 + LANES] * MLA_SCALE
        q_rope = (qq[:, c0 + LANES:c0 + 2 * LANES] * cs
                  + qq[:, rot0 + hd * LANES:rot0 + (hd + 1) * LANES] * sn) * MLA_SCALE
        q_ref[hd, :, 0:QK_NOPE] = q_nope.astype(BF16)
        q_ref[hd, :, QK_NOPE:QK_HEAD] = q_rope[:, :QK_ROPE].astype(BF16)
        k_ref[hd, :, 0:QK_NOPE] = kv[:, c0:c0 + LANES].astype(BF16)
        k_ref[hd, :, QK_NOPE:QK_HEAD] = k_rope_b
        v_ref[hd] = kv[:, c0 + LANES:c0 + 2 * LANES].astype(BF16)


def _rot_half_cols(w):
    half = QK_ROPE // 2
    return jnp.concatenate([-w[..., half:], w[..., :half]], axis=-1)


def _mla_pre(x2d, pos_f, sh, sc, g, w_in, g_q, g_kv, w_q_up, w_kv_up, bsz, seqlen, tm):
    n, d = x2d.shape
    hh = MLA_HEADS
    o = Q_LORA + KV_LORA
    wr = w_in[:, o:o + QK_ROPE]
    zpad = jnp.zeros((d, LANES - QK_ROPE), F32)
    w_in_ext = jnp.concatenate([w_in[:, :o], wr, zpad, _rot_half_cols(wr), zpad], axis=1).astype(BF16)
    wq = w_q_up.reshape(Q_LORA, hh, QK_HEAD)
    zq = jnp.zeros((Q_LORA, hh, LANES - QK_ROPE), F32)
    wq_main = jnp.concatenate([wq, zq], axis=-1).reshape(Q_LORA, hh * 2 * LANES)
    wq_rot = jnp.concatenate([_rot_half_cols(wq[..., QK_NOPE:]), zq], axis=-1).reshape(Q_LORA, hh * LANES)
    wq_ext = jnp.concatenate([wq_main, wq_rot], axis=1).astype(BF16)
    inv_freq = 1.0 / (ROPE_THETA ** (jnp.arange(0, QK_ROPE, 2, dtype=F32) / QK_ROPE))
    invf = jnp.concatenate([inv_freq, inv_freq, jnp.zeros((LANES - QK_ROPE,), F32)]).reshape(1, LANES)
    tpb = seqlen // tm
    vec = lambda i: (0, 0)
    outs = pl.pallas_call(
        _mla_pre_kernel,
        out_shape=(
            jax.ShapeDtypeStruct((bsz, hh, seqlen, QK_HEAD), BF16),
            jax.ShapeDtypeStruct((bsz, hh, seqlen, QK_HEAD), BF16),
            jax.ShapeDtypeStruct((bsz, hh, seqlen, V_HEAD), BF16),
        ),
        grid=(n // tm,),
        in_specs=[
            pl.BlockSpec((tm, d), lambda i: (i, 0)),
            pl.BlockSpec((tm, 1), lambda i: (i, 0)),
            pl.BlockSpec((None, 1, d), lambda i: (i // tpb, 0, 0)),
            pl.BlockSpec((None, 1, d), lambda i: (i // tpb, 0, 0)),
            pl.BlockSpec((1, d), vec),
            pl.BlockSpec(w_in_ext.shape, vec),
            pl.BlockSpec((1, Q_LORA), vec),
            pl.BlockSpec((1, KV_LORA), vec),
            pl.BlockSpec(wq_ext.shape, vec),
            pl.BlockSpec((KV_LORA, hh * 2 * LANES), vec),
            pl.BlockSpec((1, LANES), vec),
        ],
        out_specs=(
            pl.BlockSpec((None, hh, tm, QK_HEAD), lambda i: (i // tpb, 0, i % tpb, 0)),
            pl.BlockSpec((None, hh, tm, QK_HEAD), lambda i: (i // tpb, 0, i % tpb, 0)),
            pl.BlockSpec((None, hh, tm, V_HEAD), lambda i: (i // tpb, 0, i % tpb, 0)),
        ),
        compiler_params=_cparams("arbitrary"),
        name="mla_pre",
    )(x2d, pos_f, sh, sc, g.reshape(1, d), w_in_ext, g_q.reshape(1, -1), g_kv.reshape(1, -1), wq_ext,
      w_kv_up.astype(BF16), invf)
    return outs


ATTN_HEADS_PER_STEP = 4


def _attn_kernel(q_ref, k_ref, v_ref, o_ref, m_scr, acc_scr, *, tq, hp):
    qi = pl.program_id(2)
    m_scr[...] = jnp.full(m_scr.shape, -jnp.inf, F32)
    acc_scr[...] = jnp.zeros(acc_scr.shape, F32)
    ones = jnp.ones((tq, V_HEAD), BF16)

    def block(hd, j, masked):
        r0 = pl.multiple_of(j * tq, tq)
        k = k_ref[hd, pl.ds(r0, tq), :]
        v_ext = jnp.concatenate([v_ref[hd, pl.ds(r0, tq), :], ones], axis=1)
        s = _dot_nt(q_ref[hd], k)
        if masked:
            row = lax.broadcasted_iota(I32, (tq, tq), 0)
            col = lax.broadcasted_iota(I32, (tq, tq), 1)
            s = jnp.where(row >= col, s, -jnp.inf)
        m_prev = m_scr[hd]
        m_new = jnp.maximum(m_prev, jnp.max(s, axis=-1, keepdims=True))
        alpha = jnp.exp(m_prev - m_new)
        p = jnp.exp(s - m_new).astype(BF16)
        acc_scr[hd] = alpha * acc_scr[hd] + _dot(p, v_ext)
        m_scr[hd] = m_new

    def body(j, carry):
        for hd in range(hp):
            block(hd, j, False)
        return carry

    lax.fori_loop(0, qi, body, 0)
    for hd in range(hp):
        block(hd, qi, True)
    for hd in range(hp):
        acc = acc_scr[hd]
        o_ref[:, hd * V_HEAD:(hd + 1) * V_HEAD] = (acc[:, :V_HEAD] / acc[:, V_HEAD:]).astype(o_ref.dtype)


def _attention(q, k, v, tq):
    bsz, hh, seqlen, _ = q.shape
    hp = ATTN_HEADS_PER_STEP
    return pl.pallas_call(
        functools.partial(_attn_kernel, tq=tq, hp=hp),
        out_shape=jax.ShapeDtypeStruct((bsz, seqlen, hh * V_HEAD), BF16),
        grid=(bsz, hh // hp, seqlen // tq),
        in_specs=[
            pl.BlockSpec((None, hp, tq, QK_HEAD), lambda b, h, i: (b, h, i, 0)),
            pl.BlockSpec((None, hp, seqlen, QK_HEAD), lambda b, h, i: (b, h, 0, 0)),
            pl.BlockSpec((None, hp, seqlen, V_HEAD), lambda b, h, i: (b, h, 0, 0)),
        ],
        out_specs=pl.BlockSpec((None, tq, hp * V_HEAD), lambda b, h, i: (b, i, h)),
        scratch_shapes=[
            pltpu.VMEM((hp, tq, 1), F32),
            pltpu.VMEM((hp, tq, 2 * V_HEAD), F32),
        ],
        compiler_params=_cparams("arbitrary", "arbitrary", "arbitrary"),
        name="attn",
    )(q, k, v)


def _post_mixer_kernel(x_ref, o_ref, wout_ref, gt1_ref, sh_ref, sc_ref, g_ref, wr_ref, br_ref,
                       x1_ref, h2_ref, meta_ref, gate_ref, cnt_ref, carry_scr, *, tm):
    i = pl.program_id(0)

    @pl.when(i == 0)
    def _():
        carry_scr[...] = jnp.zeros(carry_scr.shape, F32)

    y = _dot(o_ref[...], wout_ref[...])
    x1 = x_ref[...] + gt1_ref[...] * y
    x1_ref[...] = x1
    h2 = _rms(x1) * g_ref[...] * (1.0 + sc_ref[...]) + sh_ref[...]
    half = h2.shape[1] // 2
    h2_ref[...] = _pack_bf16_pair(h2[:, :half], h2[:, half:])
    logits = _dot_f32ish(h2, wr_ref[...]) + br_ref[...]
    lane = lax.broadcasted_iota(I32, (tm, LANES), 1).astype(F32)
    work = logits
    idxs, vals = [], []
    for _ in range(TOP_K):
        mx = jnp.max(work, axis=-1, keepdims=True)
        idx = jnp.min(jnp.where(work == mx, lane, float(LANES)), axis=-1, keepdims=True)
        idxs.append(idx)
        vals.append(mx)
        work = jnp.where(lane == idx, -jnp.inf, work)
    exps = [jnp.exp(vk - vals[0]) for vk in vals]
    denom = exps[0] + exps[1] + exps[2] + exps[3]
    onehot = jnp.zeros((tm, LANES), F32)
    for idx in idxs:
        onehot = onehot + jnp.where(lane == idx, 1.0, 0.0)
    row = lax.broadcasted_iota(I32, (tm, tm), 0)
    col = lax.broadcasted_iota(I32, (tm, tm), 1)
    ltri = jnp.where(row > col, 1.0, 0.0).astype(BF16)
    cum = _dot(ltri, onehot.astype(BF16)) + carry_scr[...]
    meta = jnp.zeros((tm, LANES), F32)
    gates = jnp.zeros((tm, LANES), F32)
    for kk in range(TOP_K):
        rank = jnp.sum(jnp.where(lane == idxs[kk], cum, 0.0), axis=-1, keepdims=True)
        meta = jnp.where(lane == float(kk), idxs[kk], meta)
        meta = jnp.where(lane == float(TOP_K + kk), rank, meta)
        gates = jnp.where(lane == float(kk), exps[kk] / denom, gates)
    meta_ref[...] = jnp.transpose(meta)[:2 * TOP_K, :].astype(I32)
    gate_ref[...] = gates
    carry = carry_scr[...] + jnp.sum(onehot, axis=0, keepdims=True)
    carry_scr[...] = carry
    cnt_ref[...] = carry.astype(I32)


def _post_mixer(x2d, o2d, w_out, gt1, sh2, sc2, g_ffn, w_router, b_router, seqlen, tm):
    n, d = x2d.shape
    kdim = o2d.shape[1]
    tpb = seqlen // tm
    wr = jnp.concatenate([w_router, jnp.zeros((d, LANES - N_EXPERTS), F32)], axis=1)
    br = jnp.concatenate([b_router, jnp.full((LANES - N_EXPERTS,), -1e30, F32)]).reshape(1, LANES)
    vec = lambda i: (0, 0)
    bvec = lambda i: (i // tpb, 0, 0)
    return pl.pallas_call(
        functools.partial(_post_mixer_kernel, tm=tm),
        out_shape=(
            jax.ShapeDtypeStruct((n, d), F32),
            jax.ShapeDtypeStruct((n, d // 2), U32),
            jax.ShapeDtypeStruct((2 * TOP_K, n), I32),
            jax.ShapeDtypeStruct((n, LANES), F32),
            jax.ShapeDtypeStruct((1, LANES), I32),
        ),
        grid=(n // tm,),
        in_specs=[
            pl.BlockSpec((tm, d), lambda i: (i, 0)),
            pl.BlockSpec((tm, kdim), lambda i: (i, 0)),
            pl.BlockSpec((kdim, d), vec),
            pl.BlockSpec((None, 1, d), bvec),
            pl.BlockSpec((None, 1, d), bvec),
            pl.BlockSpec((None, 1, d), bvec),
            pl.BlockSpec((1, d), vec),
            pl.BlockSpec((d, LANES), vec),
            pl.BlockSpec((1, LANES), vec),
        ],
        out_specs=(
            pl.BlockSpec((tm, d), lambda i: (i, 0)),
            pl.BlockSpec((tm, d // 2), lambda i: (i, 0)),
            pl.BlockSpec((2 * TOP_K, tm), lambda i: (0, i)),
            pl.BlockSpec((tm, LANES), lambda i: (i, 0)),
            pl.BlockSpec((1, LANES), vec),
        ),
        scratch_shapes=[pltpu.VMEM((1, LANES), F32)],
        compiler_params=_cparams("arbitrary"),
        name="post_mixer",
    )(x2d, o2d, w_out.astype(BF16), gt1, sh2, sc2, g_ffn.reshape(1, d), wr, br)


ROW_UNROLL = 8


def _row_copy(src, s, dst, t, sem):
    return pltpu.make_async_copy(src.at[pl.ds(s, 1)], dst.at[pl.ds(t, 1)], sem)


def _dest_kernel(pstart_ref, meta_ref, dest_ref):
    e = meta_ref[0:TOP_K, :]
    dest = meta_ref[TOP_K:2 * TOP_K, :]
    for j in range(N_EXPERTS):
        dest = dest + jnp.where(e == j, pstart_ref[j], 0)
    dest_ref[...] = dest


def _dest_rows(meta_t, pstart):
    n = meta_t.shape[1]
    tn = min(4096, n)
    grid_spec = pltpu.PrefetchScalarGridSpec(
        num_scalar_prefetch=1,
        grid=(n // tn,),
        in_specs=[pl.BlockSpec((2 * TOP_K, tn), lambda i, ps: (0, i))],
        out_specs=pl.BlockSpec((TOP_K, tn), lambda i, ps: (0, i)),
    )
    return pl.pallas_call(
        _dest_kernel,
        out_shape=jax.ShapeDtypeStruct((TOP_K, n), I32),
        grid_spec=grid_spec,
        compiler_params=_cparams("arbitrary"),
        name="dest_rows",
    )(pstart, meta_t)


def _dispatch_kernel(pstart_ref, cnt_ref, dest_ref, h_ref, xs_hbm, zbuf, sem, zsem, *, tm):
    i = pl.program_id(0)

    @pl.when(i == 0)
    def _():
        zbuf[...] = jnp.zeros(zbuf.shape, U32)

        def per_expert(e, carry):
            lo = pstart_ref[e] + cnt_ref[e]
            hi = pstart_ref[e + 1]

            def start(j, c):
                _row_copy(zbuf, 0, xs_hbm, j, zsem).start()
                return c

            def wait(j, c):
                _row_copy(zbuf, 0, xs_hbm, j, zsem).wait()
                return c

            lax.fori_loop(lo, hi, start, 0)
            lax.fori_loop(lo, hi, wait, 0)
            return carry

        lax.fori_loop(0, N_EXPERTS, per_expert, 0)

    def start(tt, c):
        for u in range(ROW_UNROLL):
            t = tt * ROW_UNROLL + u
            for kk in range(TOP_K):
                _row_copy(h_ref, t, xs_hbm, dest_ref[kk, t], sem).start()
        return c

    def wait(tt, c):
        for u in range(ROW_UNROLL * TOP_K):
            _row_copy(h_ref, 0, xs_hbm, 0, sem).wait()
        return c

    lax.fori_loop(0, tm // ROW_UNROLL, start, 0)
    lax.fori_loop(0, tm // ROW_UNROLL, wait, 0)


def _dispatch(h2p, dest_t, pstart, counts, n_rows, tm):
    n, dh = h2p.shape
    grid_spec = pltpu.PrefetchScalarGridSpec(
        num_scalar_prefetch=2,
        grid=(n // tm,),
        in_specs=[
            pl.BlockSpec((TOP_K, tm), lambda i, ps, cn: (0, i), memory_space=pltpu.SMEM),
            pl.BlockSpec((tm, dh), lambda i, ps, cn: (i, 0)),
        ],
        out_specs=pl.BlockSpec(memory_space=pl.ANY),
        scratch_shapes=[
            pltpu.VMEM((8, dh), U32),
            pltpu.SemaphoreType.DMA,
            pltpu.SemaphoreType.DMA,
        ],
    )
    return pl.pallas_call(
        functools.partial(_dispatch_kernel, tm=tm),
        out_shape=jax.ShapeDtypeStruct((n_rows, dh), U32),
        grid_spec=grid_spec,
        compiler_params=pltpu.CompilerParams(dimension_semantics=("arbitrary",), vmem_limit_bytes=VMEM_LIMIT,
                                             has_side_effects=True),
        name="dispatch",
    )(pstart, counts, dest_t, h2p)


def _ffn_kernel(be_ref, nu_ref, xs_ref, wgu_ref, bgu_ref, wd_ref, bd_ref, ys_ref, wgu_b, wd_b, *, ff):
    i = pl.program_id(0)
    prev = be_ref[jnp.maximum(i - 1, 0)]
    changed = jnp.logical_or(i == 0, be_ref[i] != prev)

    @pl.when(changed)
    def _():
        wgu_b[...] = wgu_ref[...].astype(BF16)
        wd_b[...] = wd_ref[...].astype(BF16)

    @pl.when(i < nu_ref[0])
    def _():
        x_lo, x_hi = _unpack_bf16_pair(xs_ref[...])
        dh = x_lo.shape[1]
        gu = (_dot(x_lo.astype(BF16), wgu_b[0:dh, :]) + _dot(x_hi.astype(BF16), wgu_b[dh:2 * dh, :])
              + bgu_ref[...])
        g = jnp.minimum(gu[:, :ff], SWIGLU_LIMIT)
        lin = jnp.clip(gu[:, ff:], -SWIGLU_LIMIT, SWIGLU_LIMIT)
        act = g * _sigmoid(SWIGLU_ALPHA * g) * (lin + 1.0)
        y = _dot(act.astype(BF16), wd_b[...]) + bd_ref[...]
        ys_ref[...] = _pack_bf16_pair(y[:, :dh], y[:, dh:])

    @pl.when(i >= nu_ref[0])
    def _():
        ys_ref[...] = jnp.zeros(ys_ref.shape, U32)


def _ffn(xs, blk_e, n_used, w_gate_up, b_gate_up, w_down, b_down, layer):
    n_rows, dh = xs.shape
    depth, ne, d, ff2 = w_gate_up.shape
    ff = ff2 // 2
    n_blk = n_rows // MOE_BLOCK
    grid_spec = pltpu.PrefetchScalarGridSpec(
        num_scalar_prefetch=2,
        grid=(n_blk,),
        in_specs=[
            pl.BlockSpec((MOE_BLOCK, dh), lambda i, be, nu: (jnp.minimum(i, nu[0] - 1), 0)),
            pl.BlockSpec((None, None, d, ff2), lambda i, be, nu: (layer, be[i], 0, 0)),
            pl.BlockSpec((None, None, 1, ff2), lambda i, be, nu: (layer, be[i], 0, 0)),
            pl.BlockSpec((None, None, ff, d), lambda i, be, nu: (layer, be[i], 0, 0)),
            pl.BlockSpec((None, None, 1, d), lambda i, be, nu: (layer, be[i], 0, 0)),
        ],
        out_specs=pl.BlockSpec((MOE_BLOCK, dh), lambda i, be, nu: (i, 0)),
        scratch_shapes=[pltpu.VMEM((d, ff2), BF16), pltpu.VMEM((ff, d), BF16)],
    )
    return pl.pallas_call(
        functools.partial(_ffn_kernel, ff=ff),
        out_shape=jax.ShapeDtypeStruct((n_rows, dh), U32),
        grid_spec=grid_spec,
        compiler_params=_cparams("arbitrary"),
        name="ffn",
    )(blk_e, n_used, xs, w_gate_up, b_gate_up.reshape(depth, ne, 1, ff2), w_down,
      b_down.reshape(depth, ne, 1, d))


def _combine_kernel(dest_ref, x1_ref, gate_ref, gt2_ref, gfin_ref, ys_hbm, o_ref, buf, sem, *, tm, final):
    def start(tt, c):
        for u in range(ROW_UNROLL):
            t = tt * ROW_UNROLL + u
            for kk in range(TOP_K):
                _row_copy(ys_hbm, dest_ref[kk, t], buf.at[kk], t, sem).start()
        return c

    def wait(tt, c):
        for u in range(ROW_UNROLL * TOP_K):
            _row_copy(ys_hbm, 0, buf.at[0], 0, sem).wait()
        return c

    lax.fori_loop(0, tm // ROW_UNROLL, start, 0)
    lax.fori_loop(0, tm // ROW_UNROLL, wait, 0)
    gates = gate_ref[...]
    acc_lo = acc_hi = None
    for kk in range(TOP_K):
        lo, hi = _unpack_bf16_pair(buf[kk])
        gk = gates[:, kk:kk + 1]
        acc_lo = gk * lo if acc_lo is None else acc_lo + gk * lo
        acc_hi = gk * hi if acc_hi is None else acc_hi + gk * hi
    x2 = x1_ref[...] + gt2_ref[...] * jnp.concatenate([acc_lo, acc_hi], axis=1)
    if final:
        x2 = _rms(x2) * gfin_ref[...]
    o_ref[...] = x2


def _combine(x1, gates, gt2, g_final, ys, dest_t, seqlen, tm, final):
    n, d = x1.shape
    dh = ys.shape[1]
    tpb = seqlen // tm
    grid_spec = pltpu.PrefetchScalarGridSpec(
        num_scalar_prefetch=0,
        grid=(n // tm,),
        in_specs=[
            pl.BlockSpec((TOP_K, tm), lambda i: (0, i), memory_space=pltpu.SMEM),
            pl.BlockSpec((tm, d), lambda i: (i, 0)),
            pl.BlockSpec((tm, LANES), lambda i: (i, 0)),
            pl.BlockSpec((None, 1, d), lambda i: (i // tpb, 0, 0)),
            pl.BlockSpec((1, d), lambda i: (0, 0)),
            pl.BlockSpec(memory_space=pl.ANY),
        ],
        out_specs=pl.BlockSpec((tm, d), lambda i: (i, 0)),
        scratch_shapes=[pltpu.VMEM((TOP_K, tm, dh), U32), pltpu.SemaphoreType.DMA],
    )
    return pl.pallas_call(
        functools.partial(_combine_kernel, tm=tm, final=final),
        out_shape=jax.ShapeDtypeStruct((n, d), F32),
        grid_spec=grid_spec,
        compiler_params=_cparams("arbitrary"),
        name="combine",
    )(dest_t, x1, gates, gt2, g_final.reshape(1, d), ys)


def _moe(x1, h2p, meta_t, gates, counts, gt2, g_final, w_gate_up, b_gate_up, w_down, b_down, layer, seqlen,
         final):
    n, d = x1.shape
    n_pair = n * TOP_K
    n_rows = -(-n_pair // MOE_BLOCK) * MOE_BLOCK + N_EXPERTS * MOE_BLOCK
    n_blk = n_rows // MOE_BLOCK
    cnt = counts[0, :N_EXPERTS]
    padded = (cnt + MOE_BLOCK - 1) // MOE_BLOCK * MOE_BLOCK
    pad_end = jnp.cumsum(padded)
    pstart = jnp.concatenate([pad_end - padded, pad_end[-1:]]).astype(I32)
    n_used = (pad_end[-1:] // MOE_BLOCK).astype(I32)
    blk_start = jnp.arange(n_blk, dtype=I32) * MOE_BLOCK
    blk_e = jnp.minimum(jnp.sum(blk_start[:, None] >= pad_end[None, :], axis=1), N_EXPERTS - 1).astype(I32)
    tm = min(256, seqlen)
    dest_t = _dest_rows(meta_t, pstart)
    xs = _dispatch(h2p, dest_t, pstart, cnt, n_rows, tm)
    ys = _ffn(xs, blk_e, n_used, w_gate_up, b_gate_up, w_down, b_down, layer)
    return _combine(x1, gates, gt2, g_final, ys, dest_t, seqlen, tm, final)


def _softplus(x):
    return jnp.maximum(x, 0.0) + jnp.log(1.0 + jnp.exp(-jnp.abs(x)))


def _ssd_pre_kernel(x_ref, sh_ref, sc_ref, g_ref, wz_ref, wx_ref, wdt_ref, wdtT_ref, cw_ref, cb_ref,
                    dtb_ref, dtbT_ref, a_ref, aT_ref,
                    z_ref, xbc_ref, dt_ref, adt_ref, adtT_ref, ubuf, *, tm, tpb, cchunk):
    i = pl.program_id(0)
    x = x_ref[...]
    h = _rms(x) * g_ref[...] * (1.0 + sc_ref[...]) + sh_ref[...]
    hb = h.astype(BF16)
    z_ref[...] = _dot(hb, wz_ref[...]).astype(BF16)
    dt = _softplus(_dot(hb, wdt_ref[...]) + dtb_ref[...])
    dt_ref[...] = dt
    adt_ref[...] = dt * a_ref[...]
    dtT = _softplus(_dot_nt(wdtT_ref[...], hb) + dtbT_ref[...])
    adtT_ref[...] = dtT * aT_ref[...]

    @pl.when(i % tpb == 0)
    def _():
        ubuf[...] = jnp.zeros(ubuf.shape, F32)

    for c0 in range(0, SSM_CONV_DIM, cchunk):
        cols = slice(c0, c0 + cchunk)
        u = _dot(hb, wx_ref[:, cols])
        ext = jnp.concatenate([ubuf[:, cols], u], axis=0)
        acc = cb_ref[:, cols] + cw_ref[SSM_CONV - 1:SSM_CONV, cols] * u
        for j in range(1, SSM_CONV):
            shifted = pltpu.roll(ext, j, axis=0)[8:, :]
            acc = acc + cw_ref[SSM_CONV - 1 - j:SSM_CONV - j, cols] * shifted
        xbc_ref[:, cols] = (acc * _sigmoid(acc)).astype(BF16)
        ubuf[:, cols] = u[tm - 8:, :]


def _ssd_pre(x2d, sh, sc, g, w_in, conv_w, conv_b, dt_bias, a_log, bsz, seqlen, tm):
    n, d = x2d.shape
    tpb = seqlen // tm
    nh = SSM_HEADS
    wz = w_in[:, :SSM_INNER].astype(BF16)
    wx = w_in[:, SSM_INNER:SSM_INNER + SSM_CONV_DIM].astype(BF16)
    wdt_raw = w_in[:, SSM_INNER + SSM_CONV_DIM:]
    wdt = jnp.concatenate([wdt_raw, jnp.zeros((d, LANES - nh), F32)], axis=1).astype(BF16)
    wdtT = wdt_raw.T.astype(BF16)
    pad = jnp.zeros((LANES - nh,), F32)
    dtb = jnp.concatenate([dt_bias, pad]).reshape(1, LANES)
    a_neg = -jnp.exp(a_log.astype(F32))
    a_row = jnp.concatenate([a_neg, pad]).reshape(1, LANES)
    vec = lambda i: (0, 0)
    bvec = lambda i: (i // tpb, 0, 0)
    return pl.pallas_call(
        functools.partial(_ssd_pre_kernel, tm=tm, tpb=tpb, cchunk=1024),
        out_shape=(
            jax.ShapeDtypeStruct((n, SSM_INNER), BF16),
            jax.ShapeDtypeStruct((n, SSM_CONV_DIM), BF16),
            jax.ShapeDtypeStruct((n, LANES), F32),
            jax.ShapeDtypeStruct((n, LANES), F32),
            jax.ShapeDtypeStruct((bsz, nh, seqlen), F32),
        ),
        grid=(n // tm,),
        in_specs=[
            pl.BlockSpec((tm, d), lambda i: (i, 0)),
            pl.BlockSpec((None, 1, d), bvec),
            pl.BlockSpec((None, 1, d), bvec),
            pl.BlockSpec((1, d), vec),
            pl.BlockSpec((d, SSM_INNER), vec),
            pl.BlockSpec((d, SSM_CONV_DIM), vec),
            pl.BlockSpec((d, LANES), vec),
            pl.BlockSpec((nh, d), vec),
            pl.BlockSpec((SSM_CONV, SSM_CONV_DIM), vec),
            pl.BlockSpec((1, SSM_CONV_DIM), vec),
            pl.BlockSpec((1, LANES), vec),
            pl.BlockSpec((nh, 1), vec),
            pl.BlockSpec((1, LANES), vec),
            pl.BlockSpec((nh, 1), vec),
        ],
        out_specs=(
            pl.BlockSpec((tm, SSM_INNER), lambda i: (i, 0)),
            pl.BlockSpec((tm, SSM_CONV_DIM), lambda i: (i, 0)),
            pl.BlockSpec((tm, LANES), lambda i: (i, 0)),
            pl.BlockSpec((tm, LANES), lambda i: (i, 0)),
            pl.BlockSpec((None, nh, tm), lambda i: (i // tpb, 0, i % tpb)),
        ),
        scratch_shapes=[pltpu.VMEM((8, SSM_CONV_DIM), F32)],
        compiler_params=_cparams("arbitrary"),
        name="ssd_pre",
    )(x2d, sh, sc, g.reshape(1, d), wz, wx, wdt, wdtT, conv_w, conv_b.reshape(1, -1), dtb,
      dt_bias.reshape(nh, 1), a_row, a_neg.reshape(nh, 1))


def _expand_heads(v, g, rows):
    lane = lax.broadcasted_iota(I32, (rows, LANES), 1)
    lo = lane < SSM_HEAD_DIM
    parts = []
    for j in range(0, SSM_HPG, 2):
        h0 = g * SSM_HPG + j
        parts.append(jnp.where(lo, v[:, h0:h0 + 1], v[:, h0 + 1:h0 + 2]))
    return jnp.concatenate(parts, axis=1)


def _ssd_scan_kernel(xbc_ref, z_ref, dt_ref, adt_ref, adtT_ref, d_ref, gn_ref, hx_ref, yn_ref, state, *, lc):
    c = pl.program_id(1)

    @pl.when(c == 0)
    def _():
        state[...] = jnp.zeros(state.shape, F32)

    row = lax.broadcasted_iota(I32, (lc, lc), 0)
    col = lax.broadcasted_iota(I32, (lc, lc), 1)
    causal = row >= col
    tri = jnp.where(causal, 1.0, 0.0).astype(BF16)
    triT = jnp.where(row <= col, 1.0, 0.0).astype(BF16)
    ah, am, al = _split3(adt_ref[...])
    a_cum = _dot(tri, ah) + (_dot(tri, am) + _dot(tri, al))
    th, tmid, tl = _split3(adtT_ref[...])
    a_cumT = _dot(th, triT) + (_dot(tmid, triT) + _dot(tl, triT))
    dt = dt_ref[...]
    a_last = a_cum[lc - 1:lc, :]
    e_cum = jnp.exp(a_cum)
    d2e = jnp.exp(a_last - a_cum)
    cdec = jnp.exp(a_last)
    dskip = d_ref[...]
    hx = hx_ref[...]
    dt_x = _dot(dt.astype(BF16), hx)
    dtd2e_x = _dot((dt * d2e).astype(BF16), hx)
    ecum_x = _dot(e_cum.astype(BF16), hx)
    lane2 = lax.broadcasted_iota(I32, (lc, SSM_GROUP_W), 1)
    for g in range(SSM_GROUPS):
        xg = xbc_ref[:, g * SSM_GROUP_W:(g + 1) * SSM_GROUP_W].astype(F32)
        b0 = SSM_INNER + g * SSM_STATE
        c0 = SSM_INNER + SSM_BC + g * SSM_STATE
        bg = xbc_ref[:, b0:b0 + SSM_STATE]
        cg = xbc_ref[:, c0:c0 + SSM_STATE]
        gcols = slice(g * SSM_GROUP_W, (g + 1) * SSM_GROUP_W)
        xdt_b = (xg * dt_x[:, gcols]).astype(BF16)
        cb = _dot_nt(cg, bg)
        y = jnp.zeros((lc, SSM_GROUP_W), F32)
        for j in range(SSM_HPG):
            hd = g * SSM_HPG + j
            seg = a_cum[:, hd:hd + 1] - a_cumT[hd:hd + 1, :]
            decay = jnp.exp(jnp.where(causal, seg, -jnp.inf))
            m = (cb * decay).astype(BF16)
            in_head = (lane2 >= j * SSM_HEAD_DIM) & (lane2 < (j + 1) * SSM_HEAD_DIM)
            y = y + _dot(m, jnp.where(in_head, xdt_b, jnp.zeros_like(xdt_b)))
        st = state[g]
        y = y + _dot(cg, st.astype(BF16)) * ecum_x[:, gcols]
        xd2e = (xg * dtd2e_x[:, gcols]).astype(BF16)
        state[g] = st * _expand_heads(cdec, g, 1) + _dot_tn(bg, xd2e)
        y = y + _expand_heads(dskip, g, 1) * xg
        zg = z_ref[:, g * SSM_GROUP_W:(g + 1) * SSM_GROUP_W].astype(F32)
        yz = y * (zg * _sigmoid(zg))
        yn = _rms(yz) * gn_ref[:, g * SSM_GROUP_W:(g + 1) * SSM_GROUP_W]
        yn_ref[:, g * SSM_GROUP_W:(g + 1) * SSM_GROUP_W] = yn.astype(BF16)


def _ssd_scan(z, xbc, dt, adt, adtT, d_skip, g_norm, bsz, seqlen, lc):
    n = z.shape[0]
    nc = seqlen // lc
    pad = jnp.zeros((LANES - SSM_HEADS,), F32)
    d_row = jnp.concatenate([d_skip, pad]).reshape(1, LANES)
    rows = lambda b, c: (b * nc + c, 0)
    vec = lambda b, c: (0, 0)
    head_of_col = jnp.arange(SSM_INNER, dtype=I32) // SSM_HEAD_DIM
    head_expand = (jnp.arange(LANES, dtype=I32)[:, None] == head_of_col[None, :]).astype(BF16)
    return pl.pallas_call(
        functools.partial(_ssd_scan_kernel, lc=lc),
        out_shape=jax.ShapeDtypeStruct((n, SSM_INNER), BF16),
        grid=(bsz, nc),
        in_specs=[
            pl.BlockSpec((lc, SSM_CONV_DIM), rows),
            pl.BlockSpec((lc, SSM_INNER), rows),
            pl.BlockSpec((lc, LANES), rows),
            pl.BlockSpec((lc, LANES), rows),
            pl.BlockSpec((None, SSM_HEADS, lc), lambda b, c: (b, 0, c)),
            pl.BlockSpec((1, LANES), vec),
            pl.BlockSpec((1, SSM_INNER), vec),
            pl.BlockSpec((LANES, SSM_INNER), vec),
        ],
        out_specs=pl.BlockSpec((lc, SSM_INNER), rows),
        scratch_shapes=[pltpu.VMEM((SSM_GROUPS, SSM_STATE, SSM_GROUP_W), F32)],
        compiler_params=_cparams("arbitrary", "arbitrary"),
        name="ssd_scan",
    )(xbc, z, dt, adt, adtT, d_row, g_norm.reshape(1, -1), head_expand)


def kernel(x, c, positions, w_mod, b_mod, g_mix_norm, g_ffn_norm, mla_w_in, mla_g_q, mla_g_kv, mla_w_q_up, mla_w_kv_up, mla_w_out, ssm_w_in, ssm_conv_w, ssm_conv_b, ssm_dt_bias, ssm_a_log, ssm_d, ssm_g_norm, ssm_w_out, moe_w_router, moe_b_router, moe_w_gate_up, moe_b_gate_up, moe_w_down, moe_b_down, g_final):
    bsz, seqlen, d = x.shape
    depth = w_mod.shape[0]
    n = bsz * seqlen
    tm = min(512, seqlen)
    mod = _modulation(c, w_mod, b_mod)
    mod = mod.reshape(depth, 6, bsz, 1, d)
    pos_f = positions.astype(F32).reshape(n, 1)
    xc = x.reshape(n, d)
    for i in range(depth):
        sh1, sc1, gt1, sh2, sc2, gt2 = [mod[i, j] for j in range(6)]
        j = i // 2
        if i % 2 == 0:
            q, k, v = _mla_pre(xc, pos_f, sh1, sc1, g_mix_norm[i], mla_w_in[j], mla_g_q[j], mla_g_kv[j],
                               mla_w_q_up[j], mla_w_kv_up[j], bsz, seqlen, tm)
            o = _attention(q, k, v, tm).reshape(n, MLA_HEADS * V_HEAD)
            w_out = mla_w_out[j]
        else:
            tms = min(256, seqlen)
            z, xbc, dt, adt, adtT = _ssd_pre(xc, sh1, sc1, g_mix_norm[i], ssm_w_in[j], ssm_conv_w[j],
                                             ssm_conv_b[j], ssm_dt_bias[j], ssm_a_log[j], bsz, seqlen, tms)
            o = _ssd_scan(z, xbc, dt, adt, adtT, ssm_d[j], ssm_g_norm[j], bsz, seqlen, min(256, seqlen))
            w_out = ssm_w_out[j]
        x1, h2, meta, gates, counts = _post_mixer(xc, o, w_out, gt1, sh2, sc2, g_ffn_norm[i],
                                                  moe_w_router[i], moe_b_router[i], seqlen, tm)
        xc = _moe(x1, h2, meta, gates, counts, gt2, g_final, moe_w_gate_up, moe_b_gate_up,
                  moe_w_down, moe_b_down, i, seqlen, final=(i == depth - 1))
    return xc.reshape(bsz, seqlen, d)
```

```python
import functools
import math

import jax
import jax.numpy as jnp
from jax import lax
from jax.experimental import pallas as pl
from jax.experimental.pallas import tpu as pltpu

F32 = jnp.float32
BF16 = jnp.bfloat16
I32 = jnp.int32
U32 = jnp.uint32

NORM_EPS = 1e-6
MLA_HEADS = 8
Q_LORA = 256
KV_LORA = 256
QK_NOPE = 128
QK_ROPE = 64
V_HEAD = 128
QK_HEAD = QK_NOPE + QK_ROPE
ROPE_THETA = 10000.0
MLA_SCALE = QK_HEAD ** -0.5
SSM_HEAD_DIM = 64
SSM_GROUPS = 8
SSM_HPG = 4
SSM_HEADS = SSM_GROUPS * SSM_HPG
SSM_STATE = 128
SSM_CONV = 4
SSM_GROUP_W = SSM_HPG * SSM_HEAD_DIM
SSM_INNER = SSM_GROUPS * SSM_GROUP_W
SSM_BC = SSM_GROUPS * SSM_STATE
SSM_CONV_DIM = SSM_INNER + 2 * SSM_BC
N_EXPERTS = 32
TOP_K = 4
SWIGLU_LIMIT = 7.0
SWIGLU_ALPHA = 1.702
MOE_BLOCK = 256

LANES = 128
VMEM_LIMIT = 56 * 1024 * 1024


def _cparams(*sem):
    return pltpu.CompilerParams(dimension_semantics=tuple(sem), vmem_limit_bytes=VMEM_LIMIT)


def _dot(a, b):
    return jnp.dot(a, b, preferred_element_type=F32)


def _dot_nt(a, b):
    return lax.dot_general(a, b, (((1,), (1,)), ((), ())), preferred_element_type=F32)


def _dot_tn(a, b):
    return lax.dot_general(a, b, (((0,), (0,)), ((), ())), preferred_element_type=F32)


def _split3(a):
    hi = a.astype(BF16)
    r1 = a - hi.astype(F32)
    mid = r1.astype(BF16)
    lo = (r1 - mid.astype(F32)).astype(BF16)
    return hi, mid, lo


def _dot_f32ish(a, b):
    ah, am, _ = _split3(a)
    bh, bm, _ = _split3(b)
    return _dot(ah, bh) + (_dot(ah, bm) + _dot(am, bh))


def _pack_bf16_pair(lo, hi):
    lo_b = lax.bitcast_convert_type(lo.astype(BF16).astype(F32), U32)
    hi_b = lax.bitcast_convert_type(hi.astype(BF16).astype(F32), U32)
    return hi_b | lax.shift_right_logical(lo_b, jnp.uint32(16))


def _unpack_bf16_pair(w):
    lo = lax.bitcast_convert_type(lax.shift_left(w, jnp.uint32(16)), F32)
    hi = lax.bitcast_convert_type(w & jnp.uint32(0xFFFF0000), F32)
    return lo, hi


def _sigmoid(x):
    return 1.0 / (1.0 + jnp.exp(-x))


def _rms(x):
    return x * lax.rsqrt(jnp.mean(x * x, axis=-1, keepdims=True) + NORM_EPS)


def _mod_kernel(c_ref, w_ref, b_ref, o_ref):
    c = c_ref[...]
    cond = c * _sigmoid(c)
    o_ref[...] = _dot_f32ish(cond, w_ref[...]) + b_ref[...]


def _modulation(c, w_mod, b_mod):
    depth, d, d6 = w_mod.shape
    bsz = c.shape[0]
    nj = d6 // d
    return pl.pallas_call(
        _mod_kernel,
        out_shape=jax.ShapeDtypeStruct((depth, nj, bsz, d), F32),
        grid=(depth, nj),
        in_specs=[
            pl.BlockSpec((bsz, d), lambda l, j: (0, 0)),
            pl.BlockSpec((None, d, d), lambda l, j: (l, 0, j)),
            pl.BlockSpec((None, 1, d), lambda l, j: (l, 0, j)),
        ],
        out_specs=pl.BlockSpec((None, None, bsz, d), lambda l, j: (l, j, 0, 0)),
        compiler_params=_cparams("arbitrary", "arbitrary"),
        name="mod",
    )(c, w_mod, b_mod.reshape(depth, 1, d6))


def _mla_pre_kernel(x_ref, pos_ref, sh_ref, sc_ref, g_ref, win_ref, gq_ref, gkv_ref, wq_ref, wkv_ref,
                    invf_ref, q_ref, k_ref, v_ref):
    x = x_ref[...]
    h = _rms(x) * g_ref[...] * (1.0 + sc_ref[...]) + sh_ref[...]
    lat = _dot(h.astype(BF16), win_ref[...])
    q_lat = _rms(lat[:, :Q_LORA]) * gq_ref[...]
    kv_lat = _rms(lat[:, Q_LORA:Q_LORA + KV_LORA]) * gkv_ref[...]
    ang = pos_ref[...] * invf_ref[...]
    cs = jnp.cos(ang)
    sn = jnp.sin(ang)
    o = Q_LORA + KV_LORA
    k_rope = lat[:, o:o + LANES] * cs + lat[:, o + LANES:o + 2 * LANES] * sn
    qq = _dot(q_lat.astype(BF16), wq_ref[...])
    kv = _dot(kv_lat.astype(BF16), wkv_ref[...])
    rot0 = MLA_HEADS * 2 * LANES
    k_rope_b = k_rope[:, :QK_ROPE].astype(BF16)
    for hd in range(MLA_HEADS):
        c0 = hd * 2 * LANES
        q_nope = qq[:, c0:c0 + LANES] * MLA_SCALE
        q_rope = (qq[:, c0 + LANES:c0 + 2 * LANES] * cs
                  + qq[:, rot0 + hd * LANES:rot0 + (hd + 1) * LANES] * sn) * MLA_SCALE
        q_ref[hd, :, 0:QK_NOPE] = q_nope.astype(BF16)
        q_ref[hd, :, QK_NOPE:QK_HEAD] = q_rope[:, :QK_ROPE].astype(BF16)
        k_ref[hd, :, 0:QK_NOPE] = kv[:, c0:c0 + LANES].astype(BF16)
        k_ref[hd, :, QK_NOPE:QK_HEAD] = k_rope_b
        v_ref[hd] = kv[:, c0 + LANES:c0 + 2 * LANES].astype(BF16)


def _rot_half_cols(w):
    half = QK_ROPE // 2
    return jnp.concatenate([-w[..., half:], w[..., :half]], axis=-1)


def _mla_pre(x2d, pos_f, sh, sc, g, w_in, g_q, g_kv, w_q_up, w_kv_up, bsz, seqlen, tm):
    n, d = x2d.shape
    hh = MLA_HEADS
    o = Q_LORA + KV_LORA
    wr = w_in[:, o:o + QK_ROPE]
    zpad = jnp.zeros((d, LANES - QK_ROPE), F32)
    w_in_ext = jnp.concatenate([w_in[:, :o], wr, zpad, _rot_half_cols(wr), zpad], axis=1).astype(BF16)
    wq = w_q_up.reshape(Q_LORA, hh, QK_HEAD)
    zq = jnp.zeros((Q_LORA, hh, LANES - QK_ROPE), F32)
    wq_main = jnp.concatenate([wq, zq], axis=-1).reshape(Q_LORA, hh * 2 * LANES)
    wq_rot = jnp.concatenate([_rot_half_cols(wq[..., QK_NOPE:]), zq], axis=-1).reshape(Q_LORA, hh * LANES)
    wq_ext = jnp.concatenate([wq_main, wq_rot], axis=1).astype(BF16)
    inv_freq = 1.0 / (ROPE_THETA ** (jnp.arange(0, QK_ROPE, 2, dtype=F32) / QK_ROPE))
    invf = jnp.concatenate([inv_freq, inv_freq, jnp.zeros((LANES - QK_ROPE,), F32)]).reshape(1, LANES)
    tpb = seqlen // tm
    vec = lambda i: (0, 0)
    outs = pl.pallas_call(
        _mla_pre_kernel,
        out_shape=(
            jax.ShapeDtypeStruct((bsz, hh, seqlen, QK_HEAD), BF16),
            jax.ShapeDtypeStruct((bsz, hh, seqlen, QK_HEAD), BF16),
            jax.ShapeDtypeStruct((bsz, hh, seqlen, V_HEAD), BF16),
        ),
        grid=(n // tm,),
        in_specs=[
            pl.BlockSpec((tm, d), lambda i: (i, 0)),
            pl.BlockSpec((tm, 1), lambda i: (i, 0)),
            pl.BlockSpec((None, 1, d), lambda i: (i // tpb, 0, 0)),
            pl.BlockSpec((None, 1, d), lambda i: (i // tpb, 0, 0)),
            pl.BlockSpec((1, d), vec),
            pl.BlockSpec(w_in_ext.shape, vec),
            pl.BlockSpec((1, Q_LORA), vec),
            pl.BlockSpec((1, KV_LORA), vec),
            pl.BlockSpec(wq_ext.shape, vec),
            pl.BlockSpec((KV_LORA, hh * 2 * LANES), vec),
            pl.BlockSpec((1, LANES), vec),
        ],
        out_specs=(
            pl.BlockSpec((None, hh, tm, QK_HEAD), lambda i: (i // tpb, 0, i % tpb, 0)),
            pl.BlockSpec((None, hh, tm, QK_HEAD), lambda i: (i // tpb, 0, i % tpb, 0)),
            pl.BlockSpec((None, hh, tm, V_HEAD), lambda i: (i // tpb, 0, i % tpb, 0)),
        ),
        compiler_params=_cparams("arbitrary"),
        name="mla_pre",
    )(x2d, pos_f, sh, sc, g.reshape(1, d), w_in_ext, g_q.reshape(1, -1), g_kv.reshape(1, -1), wq_ext,
      w_kv_up.astype(BF16), invf)
    return outs


ATTN_HEADS_PER_STEP = 4


def _attn_kernel(q_ref, k_ref, v_ref, o_ref, m_scr, acc_scr, *, tq, hp):
    qi = pl.program_id(2)
    m_scr[...] = jnp.full(m_scr.shape, -jnp.inf, F32)
    acc_scr[...] = jnp.zeros(acc_scr.shape, F32)

    def block(hd, r0, tk, masked):
        k = k_ref[hd, pl.ds(r0, tk), :]
        v_ext = jnp.concatenate([v_ref[hd, pl.ds(r0, tk), :], jnp.ones((tk, V_HEAD), BF16)], axis=1)
        s = _dot_nt(q_ref[hd], k)
        if masked:
            row = lax.broadcasted_iota(I32, (tq, tk), 0)
            col = lax.broadcasted_iota(I32, (tq, tk), 1)
            s = jnp.where(row >= col, s, -jnp.inf)
        m_prev = m_scr[hd]
        m_new = jnp.maximum(m_prev, jnp.max(s, axis=-1, keepdims=True))
        alpha = jnp.exp(m_prev - m_new)
        p = jnp.exp(s - m_new).astype(BF16)
        acc_scr[hd] = alpha * acc_scr[hd] + _dot(p, v_ext)
        m_scr[hd] = m_new

    def body(j, carry):
        for hd in range(hp):
            block(hd, pl.multiple_of(j * (2 * tq), 2 * tq), 2 * tq, False)
        return carry

    lax.fori_loop(0, qi // 2, body, 0)

    @pl.when(qi % 2 == 1)
    def _():
        for hd in range(hp):
            block(hd, pl.multiple_of((qi - 1) * tq, tq), tq, False)

    for hd in range(hp):
        block(hd, pl.multiple_of(qi * tq, tq), tq, True)
    for hd in range(hp):
        acc = acc_scr[hd]
        o_ref[:, hd * V_HEAD:(hd + 1) * V_HEAD] = (acc[:, :V_HEAD] / acc[:, V_HEAD:]).astype(o_ref.dtype)


def _attention(q, k, v, tq):
    bsz, hh, seqlen, _ = q.shape
    hp = ATTN_HEADS_PER_STEP
    return pl.pallas_call(
        functools.partial(_attn_kernel, tq=tq, hp=hp),
        out_shape=jax.ShapeDtypeStruct((bsz, seqlen, hh * V_HEAD), BF16),
        grid=(bsz, hh // hp, seqlen // tq),
        in_specs=[
            pl.BlockSpec((None, hp, tq, QK_HEAD), lambda b, h, i: (b, h, i, 0)),
            pl.BlockSpec((None, hp, seqlen, QK_HEAD), lambda b, h, i: (b, h, 0, 0)),
            pl.BlockSpec((None, hp, seqlen, V_HEAD), lambda b, h, i: (b, h, 0, 0)),
        ],
        out_specs=pl.BlockSpec((None, tq, hp * V_HEAD), lambda b, h, i: (b, i, h)),
        scratch_shapes=[
            pltpu.VMEM((hp, tq, 1), F32),
            pltpu.VMEM((hp, tq, 2 * V_HEAD), F32),
        ],
        compiler_params=_cparams("arbitrary", "arbitrary", "arbitrary"),
        name="attn",
    )(q, k, v)


def _post_mixer_kernel(x_ref, o_ref, wout_ref, gt1_ref, sh_ref, sc_ref, g_ref, wr_ref, br_ref,
                       x1_ref, h2_ref, meta_ref, gate_ref, cnt_ref, carry_scr, *, tm):
    i = pl.program_id(0)

    @pl.when(i == 0)
    def _():
        carry_scr[...] = jnp.zeros(carry_scr.shape, F32)

    y = _dot(o_ref[...], wout_ref[...])
    x1 = x_ref[...] + gt1_ref[...] * y
    x1_ref[...] = x1
    h2 = _rms(x1) * g_ref[...] * (1.0 + sc_ref[...]) + sh_ref[...]
    half = h2.shape[1] // 2
    h2_ref[...] = _pack_bf16_pair(h2[:, :half], h2[:, half:])
    logits = _dot_f32ish(h2, wr_ref[...]) + br_ref[...]
    lane = lax.broadcasted_iota(I32, (tm, LANES), 1).astype(F32)
    work = logits
    idxs, vals = [], []
    for _ in range(TOP_K):
        mx = jnp.max(work, axis=-1, keepdims=True)
        idx = jnp.min(jnp.where(work == mx, lane, float(LANES)), axis=-1, keepdims=True)
        idxs.append(idx)
        vals.append(mx)
        work = jnp.where(lane == idx, -jnp.inf, work)
    exps = [jnp.exp(vk - vals[0]) for vk in vals]
    denom = exps[0] + exps[1] + exps[2] + exps[3]
    onehot = jnp.zeros((tm, LANES), F32)
    for idx in idxs:
        onehot = onehot + jnp.where(lane == idx, 1.0, 0.0)
    row = lax.broadcasted_iota(I32, (tm, tm), 0)
    col = lax.broadcasted_iota(I32, (tm, tm), 1)
    ltri = jnp.where(row > col, 1.0, 0.0).astype(BF16)
    cum = _dot(ltri, onehot.astype(BF16)) + carry_scr[...]
    meta = jnp.zeros((tm, LANES), F32)
    gates = jnp.zeros((tm, LANES), F32)
    for kk in range(TOP_K):
        rank = jnp.sum(jnp.where(lane == idxs[kk], cum, 0.0), axis=-1, keepdims=True)
        meta = jnp.where(lane == float(kk), idxs[kk], meta)
        meta = jnp.where(lane == float(TOP_K + kk), rank, meta)
        gates = jnp.where(lane == float(kk), exps[kk] / denom, gates)
    meta_ref[...] = jnp.transpose(meta)[:2 * TOP_K, :].astype(I32)
    gate_ref[...] = gates
    carry = carry_scr[...] + jnp.sum(onehot, axis=0, keepdims=True)
    carry_scr[...] = carry
    cnt_ref[...] = carry.astype(I32)


def _post_mixer(x2d, o2d, w_out, gt1, sh2, sc2, g_ffn, w_router, b_router, seqlen, tm):
    n, d = x2d.shape
    kdim = o2d.shape[1]
    tpb = seqlen // tm
    wr = jnp.concatenate([w_router, jnp.zeros((d, LANES - N_EXPERTS), F32)], axis=1)
    br = jnp.concatenate([b_router, jnp.full((LANES - N_EXPERTS,), -1e30, F32)]).reshape(1, LANES)
    vec = lambda i: (0, 0)
    bvec = lambda i: (i // tpb, 0, 0)
    return pl.pallas_call(
        functools.partial(_post_mixer_kernel, tm=tm),
        out_shape=(
            jax.ShapeDtypeStruct((n, d), F32),
            jax.ShapeDtypeStruct((n, d // 2), U32),
            jax.ShapeDtypeStruct((2 * TOP_K, n), I32),
            jax.ShapeDtypeStruct((n, LANES), F32),
            jax.ShapeDtypeStruct((1, LANES), I32),
        ),
        grid=(n // tm,),
        in_specs=[
            pl.BlockSpec((tm, d), lambda i: (i, 0)),
            pl.BlockSpec((tm, kdim), lambda i: (i, 0)),
            pl.BlockSpec((kdim, d), vec),
            pl.BlockSpec((None, 1, d), bvec),
            pl.BlockSpec((None, 1, d), bvec),
            pl.BlockSpec((None, 1, d), bvec),
            pl.BlockSpec((1, d), vec),
            pl.BlockSpec((d, LANES), vec),
            pl.BlockSpec((1, LANES), vec),
        ],
        out_specs=(
            pl.BlockSpec((tm, d), lambda i: (i, 0)),
            pl.BlockSpec((tm, d // 2), lambda i: (i, 0)),
            pl.BlockSpec((2 * TOP_K, tm), lambda i: (0, i)),
            pl.BlockSpec((tm, LANES), lambda i: (i, 0)),
            pl.BlockSpec((1, LANES), vec),
        ),
        scratch_shapes=[pltpu.VMEM((1, LANES), F32)],
        compiler_params=_cparams("arbitrary"),
        name="post_mixer",
    )(x2d, o2d, w_out.astype(BF16), gt1, sh2, sc2, g_ffn.reshape(1, d), wr, br)


ROW_UNROLL = 8


def _row_copy(src, s, dst, t, sem):
    return pltpu.make_async_copy(src.at[pl.ds(s, 1)], dst.at[pl.ds(t, 1)], sem)


def _dest_kernel(pstart_ref, meta_ref, dest_ref):
    e = meta_ref[0:TOP_K, :]
    dest = meta_ref[TOP_K:2 * TOP_K, :]
    for j in range(N_EXPERTS):
        dest = dest + jnp.where(e == j, pstart_ref[j], 0)
    dest_ref[...] = dest


def _dest_rows(meta_t, pstart):
    n = meta_t.shape[1]
    tn = min(4096, n)
    grid_spec = pltpu.PrefetchScalarGridSpec(
        num_scalar_prefetch=1,
        grid=(n // tn,),
        in_specs=[pl.BlockSpec((2 * TOP_K, tn), lambda i, ps: (0, i))],
        out_specs=pl.BlockSpec((TOP_K, tn), lambda i, ps: (0, i)),
    )
    return pl.pallas_call(
        _dest_kernel,
        out_shape=jax.ShapeDtypeStruct((TOP_K, n), I32),
        grid_spec=grid_spec,
        compiler_params=_cparams("arbitrary"),
        name="dest_rows",
    )(pstart, meta_t)


def _dispatch_kernel(pstart_ref, cnt_ref, dest_ref, h_ref, xs_hbm, zbuf, sem, zsem, *, tm):
    i = pl.program_id(0)

    @pl.when(i == 0)
    def _():
        zbuf[...] = jnp.zeros(zbuf.shape, U32)

        def per_expert(e, carry):
            lo = pstart_ref[e] + cnt_ref[e]
            hi = pstart_ref[e + 1]

            def start(j, c):
                _row_copy(zbuf, 0, xs_hbm, j, zsem).start()
                return c

            def wait(j, c):
                _row_copy(zbuf, 0, xs_hbm, j, zsem).wait()
                return c

            lax.fori_loop(lo, hi, start, 0)
            lax.fori_loop(lo, hi, wait, 0)
            return carry

        lax.fori_loop(0, N_EXPERTS, per_expert, 0)

    def start(tt, c):
        for u in range(ROW_UNROLL):
            t = tt * ROW_UNROLL + u
            for kk in range(TOP_K):
                _row_copy(h_ref, t, xs_hbm, dest_ref[kk, t], sem).start(priority=kk % 2)
        return c

    def wait(tt, c):
        for u in range(ROW_UNROLL * TOP_K):
            _row_copy(h_ref, 0, xs_hbm, 0, sem).wait()
        return c

    lax.fori_loop(0, tm // ROW_UNROLL, start, 0)
    lax.fori_loop(0, tm // ROW_UNROLL, wait, 0)


def _dispatch(h2p, dest_t, pstart, counts, n_rows, tm):
    n, dh = h2p.shape
    grid_spec = pltpu.PrefetchScalarGridSpec(
        num_scalar_prefetch=2,
        grid=(n // tm,),
        in_specs=[
            pl.BlockSpec((TOP_K, tm), lambda i, ps, cn: (0, i), memory_space=pltpu.SMEM),
            pl.BlockSpec((tm, dh), lambda i, ps, cn: (i, 0)),
        ],
        out_specs=pl.BlockSpec(memory_space=pl.ANY),
        scratch_shapes=[
            pltpu.VMEM((8, dh), U32),
            pltpu.SemaphoreType.DMA,
            pltpu.SemaphoreType.DMA,
        ],
    )
    return pl.pallas_call(
        functools.partial(_dispatch_kernel, tm=tm),
        out_shape=jax.ShapeDtypeStruct((n_rows, dh), U32),
        grid_spec=grid_spec,
        compiler_params=pltpu.CompilerParams(dimension_semantics=("arbitrary",), vmem_limit_bytes=VMEM_LIMIT,
                                             has_side_effects=True),
        name="dispatch",
    )(pstart, counts, dest_t, h2p)


def _ffn_kernel(be_ref, nu_ref, xs_ref, wgu_ref, bgu_ref, wd_ref, bd_ref, ys_ref, wgu_b, wd_b, *, ff):
    i = pl.program_id(0)
    prev = be_ref[jnp.maximum(i - 1, 0)]
    changed = jnp.logical_or(i == 0, be_ref[i] != prev)

    @pl.when(changed)
    def _():
        wgu_b[...] = wgu_ref[...].astype(BF16)
        wd_b[...] = wd_ref[...].astype(BF16)

    @pl.when(i < nu_ref[0])
    def _():
        x_lo, x_hi = _unpack_bf16_pair(xs_ref[...])
        dh = x_lo.shape[1]
        gu = (_dot(x_lo.astype(BF16), wgu_b[0:dh, :]) + _dot(x_hi.astype(BF16), wgu_b[dh:2 * dh, :])
              + bgu_ref[...])
        g = jnp.minimum(gu[:, :ff], SWIGLU_LIMIT)
        lin = jnp.clip(gu[:, ff:], -SWIGLU_LIMIT, SWIGLU_LIMIT)
        act = g * _sigmoid(SWIGLU_ALPHA * g) * (lin + 1.0)
        y = _dot(act.astype(BF16), wd_b[...]) + bd_ref[...]
        ys_ref[...] = _pack_bf16_pair(y[:, :dh], y[:, dh:])

    @pl.when(i >= nu_ref[0])
    def _():
        ys_ref[...] = jnp.zeros(ys_ref.shape, U32)


def _ffn(xs, blk_e, n_used, w_gate_up, b_gate_up, w_down, b_down, layer):
    n_rows, dh = xs.shape
    depth, ne, d, ff2 = w_gate_up.shape
    ff = ff2 // 2
    n_blk = n_rows // MOE_BLOCK
    grid_spec = pltpu.PrefetchScalarGridSpec(
        num_scalar_prefetch=2,
        grid=(n_blk,),
        in_specs=[
            pl.BlockSpec((MOE_BLOCK, dh), lambda i, be, nu: (jnp.minimum(i, nu[0] - 1), 0)),
            pl.BlockSpec((None, None, d, ff2), lambda i, be, nu: (layer, be[i], 0, 0)),
            pl.BlockSpec((None, None, 1, ff2), lambda i, be, nu: (layer, be[i], 0, 0)),
            pl.BlockSpec((None, None, ff, d), lambda i, be, nu: (layer, be[i], 0, 0)),
            pl.BlockSpec((None, None, 1, d), lambda i, be, nu: (layer, be[i], 0, 0)),
        ],
        out_specs=pl.BlockSpec((MOE_BLOCK, dh), lambda i, be, nu: (i, 0)),
        scratch_shapes=[pltpu.VMEM((d, ff2), BF16), pltpu.VMEM((ff, d), BF16)],
    )
    return pl.pallas_call(
        functools.partial(_ffn_kernel, ff=ff),
        out_shape=jax.ShapeDtypeStruct((n_rows, dh), U32),
        grid_spec=grid_spec,
        compiler_params=_cparams("arbitrary"),
        name="ffn",
    )(blk_e, n_used, xs, w_gate_up, b_gate_up.reshape(depth, ne, 1, ff2), w_down,
      b_down.reshape(depth, ne, 1, d))


def _combine_kernel(dest_ref, x1_ref, gate_ref, gt2_ref, gfin_ref, ys_hbm, o_ref, buf, sem, *, tm, final):
    def start(tt, c):
        for u in range(ROW_UNROLL):
            t = tt * ROW_UNROLL + u
            for kk in range(TOP_K):
                _row_copy(ys_hbm, dest_ref[kk, t], buf.at[kk], t, sem).start(priority=kk % 2)
        return c

    def wait(tt, c):
        for u in range(ROW_UNROLL * TOP_K):
            _row_copy(ys_hbm, 0, buf.at[0], 0, sem).wait()
        return c

    lax.fori_loop(0, tm // ROW_UNROLL, start, 0)
    lax.fori_loop(0, tm // ROW_UNROLL, wait, 0)
    gates = gate_ref[...]
    acc_lo = acc_hi = None
    for kk in range(TOP_K):
        lo, hi = _unpack_bf16_pair(buf[kk])
        gk = gates[:, kk:kk + 1]
        acc_lo = gk * lo if acc_lo is None else acc_lo + gk * lo
        acc_hi = gk * hi if acc_hi is None else acc_hi + gk * hi
    x2 = x1_ref[...] + gt2_ref[...] * jnp.concatenate([acc_lo, acc_hi], axis=1)
    if final:
        x2 = _rms(x2) * gfin_ref[...]
    o_ref[...] = x2


def _combine(x1, gates, gt2, g_final, ys, dest_t, seqlen, tm, final):
    n, d = x1.shape
    dh = ys.shape[1]
    tpb = seqlen // tm
    grid_spec = pltpu.PrefetchScalarGridSpec(
        num_scalar_prefetch=0,
        grid=(n // tm,),
        in_specs=[
            pl.BlockSpec((TOP_K, tm), lambda i: (0, i), memory_space=pltpu.SMEM),
            pl.BlockSpec((tm, d), lambda i: (i, 0)),
            pl.BlockSpec((tm, LANES), lambda i: (i, 0)),
            pl.BlockSpec((None, 1, d), lambda i: (i // tpb, 0, 0)),
            pl.BlockSpec((1, d), lambda i: (0, 0)),
            pl.BlockSpec(memory_space=pl.ANY),
        ],
        out_specs=pl.BlockSpec((tm, d), lambda i: (i, 0)),
        scratch_shapes=[pltpu.VMEM((TOP_K, tm, dh), U32), pltpu.SemaphoreType.DMA],
    )
    return pl.pallas_call(
        functools.partial(_combine_kernel, tm=tm, final=final),
        out_shape=jax.ShapeDtypeStruct((n, d), F32),
        grid_spec=grid_spec,
        compiler_params=_cparams("arbitrary"),
        name="combine",
    )(dest_t, x1, gates, gt2, g_final.reshape(1, d), ys)


def _moe(x1, h2p, meta_t, gates, counts, gt2, g_final, w_gate_up, b_gate_up, w_down, b_down, layer, seqlen,
         final):
    n, d = x1.shape
    n_pair = n * TOP_K
    n_rows = -(-n_pair // MOE_BLOCK) * MOE_BLOCK + N_EXPERTS * MOE_BLOCK
    n_blk = n_rows // MOE_BLOCK
    cnt = counts[0, :N_EXPERTS]
    padded = (cnt + MOE_BLOCK - 1) // MOE_BLOCK * MOE_BLOCK
    pad_end = jnp.cumsum(padded)
    pstart = jnp.concatenate([pad_end - padded, pad_end[-1:]]).astype(I32)
    n_used = (pad_end[-1:] // MOE_BLOCK).astype(I32)
    blk_start = jnp.arange(n_blk, dtype=I32) * MOE_BLOCK
    blk_e = jnp.minimum(jnp.sum(blk_start[:, None] >= pad_end[None, :], axis=1), N_EXPERTS - 1).astype(I32)
    tm = min(256, seqlen)
    dest_t = _dest_rows(meta_t, pstart)
    xs = _dispatch(h2p, dest_t, pstart, cnt, n_rows, tm)
    ys = _ffn(xs, blk_e, n_used, w_gate_up, b_gate_up, w_down, b_down, layer)
    return _combine(x1, gates, gt2, g_final, ys, dest_t, seqlen, tm, final)


def _softplus(x):
    return jnp.maximum(x, 0.0) + jnp.log(1.0 + jnp.exp(-jnp.abs(x)))


def _ssd_pre_kernel(x_ref, sh_ref, sc_ref, g_ref, wz_ref, wx_ref, wdt_ref, wdtT_ref, cw_ref, cb_ref,
                    dtb_ref, dtbT_ref, a_ref, aT_ref,
                    z_ref, xbc_ref, dt_ref, adt_ref, adtT_ref, ubuf, *, tm, tpb, cchunk):
    i = pl.program_id(0)
    x = x_ref[...]
    h = _rms(x) * g_ref[...] * (1.0 + sc_ref[...]) + sh_ref[...]
    hb = h.astype(BF16)
    z_ref[...] = _dot(hb, wz_ref[...]).astype(BF16)
    dt = _softplus(_dot(hb, wdt_ref[...]) + dtb_ref[...])
    dt_ref[...] = dt
    adt_ref[...] = dt * a_ref[...]
    dtT = _softplus(_dot_nt(wdtT_ref[...], hb) + dtbT_ref[...])
    adtT_ref[...] = dtT * aT_ref[...]

    @pl.when(i % tpb == 0)
    def _():
        ubuf[...] = jnp.zeros(ubuf.shape, F32)

    for c0 in range(0, SSM_CONV_DIM, cchunk):
        cols = slice(c0, c0 + cchunk)
        u = _dot(hb, wx_ref[:, cols])
        ext = jnp.concatenate([ubuf[:, cols], u], axis=0)
        acc = cb_ref[:, cols] + cw_ref[SSM_CONV - 1:SSM_CONV, cols] * u
        for j in range(1, SSM_CONV):
            shifted = pltpu.roll(ext, j, axis=0)[8:, :]
            acc = acc + cw_ref[SSM_CONV - 1 - j:SSM_CONV - j, cols] * shifted
        xbc_ref[:, cols] = (acc * _sigmoid(acc)).astype(BF16)
        ubuf[:, cols] = u[tm - 8:, :]


def _ssd_pre(x2d, sh, sc, g, w_in, conv_w, conv_b, dt_bias, a_log, bsz, seqlen, tm):
    n, d = x2d.shape
    tpb = seqlen // tm
    nh = SSM_HEADS
    wz = w_in[:, :SSM_INNER].astype(BF16)
    wx = w_in[:, SSM_INNER:SSM_INNER + SSM_CONV_DIM].astype(BF16)
    wdt_raw = w_in[:, SSM_INNER + SSM_CONV_DIM:]
    wdt = jnp.concatenate([wdt_raw, jnp.zeros((d, LANES - nh), F32)], axis=1).astype(BF16)
    wdtT = wdt_raw.T.astype(BF16)
    pad = jnp.zeros((LANES - nh,), F32)
    dtb = jnp.concatenate([dt_bias, pad]).reshape(1, LANES)
    a_neg = -jnp.exp(a_log.astype(F32))
    a_row = jnp.concatenate([a_neg, pad]).reshape(1, LANES)
    vec = lambda i: (0, 0)
    bvec = lambda i: (i // tpb, 0, 0)
    return pl.pallas_call(
        functools.partial(_ssd_pre_kernel, tm=tm, tpb=tpb, cchunk=1024),
        out_shape=(
            jax.ShapeDtypeStruct((n, SSM_INNER), BF16),
            jax.ShapeDtypeStruct((n, SSM_CONV_DIM), BF16),
            jax.ShapeDtypeStruct((n, LANES), F32),
            jax.ShapeDtypeStruct((n, LANES), F32),
            jax.ShapeDtypeStruct((bsz, nh, seqlen), F32),
        ),
        grid=(n // tm,),
        in_specs=[
            pl.BlockSpec((tm, d), lambda i: (i, 0)),
            pl.BlockSpec((None, 1, d), bvec),
            pl.BlockSpec((None, 1, d), bvec),
            pl.BlockSpec((1, d), vec),
            pl.BlockSpec((d, SSM_INNER), vec),
            pl.BlockSpec((d, SSM_CONV_DIM), vec),
            pl.BlockSpec((d, LANES), vec),
            pl.BlockSpec((nh, d), vec),
            pl.BlockSpec((SSM_CONV, SSM_CONV_DIM), vec),
            pl.BlockSpec((1, SSM_CONV_DIM), vec),
            pl.BlockSpec((1, LANES), vec),
            pl.BlockSpec((nh, 1), vec),
            pl.BlockSpec((1, LANES), vec),
            pl.BlockSpec((nh, 1), vec),
        ],
        out_specs=(
            pl.BlockSpec((tm, SSM_INNER), lambda i: (i, 0)),
            pl.BlockSpec((tm, SSM_CONV_DIM), lambda i: (i, 0)),
            pl.BlockSpec((tm, LANES), lambda i: (i, 0)),
            pl.BlockSpec((tm, LANES), lambda i: (i, 0)),
            pl.BlockSpec((None, nh, tm), lambda i: (i // tpb, 0, i % tpb)),
        ),
        scratch_shapes=[pltpu.VMEM((8, SSM_CONV_DIM), F32)],
        compiler_params=_cparams("arbitrary"),
        name="ssd_pre",
    )(x2d, sh, sc, g.reshape(1, d), wz, wx, wdt, wdtT, conv_w, conv_b.reshape(1, -1), dtb,
      dt_bias.reshape(nh, 1), a_row, a_neg.reshape(nh, 1))


def _expand_heads(v, g, rows):
    lane = lax.broadcasted_iota(I32, (rows, LANES), 1)
    lo = lane < SSM_HEAD_DIM
    parts = []
    for j in range(0, SSM_HPG, 2):
        h0 = g * SSM_HPG + j
        parts.append(jnp.where(lo, v[:, h0:h0 + 1], v[:, h0 + 1:h0 + 2]))
    return jnp.concatenate(parts, axis=1)


def _ssd_scan_kernel(xbc_ref, z_ref, dt_ref, adt_ref, adtT_ref, d_ref, gn_ref, hx_ref, yn_ref, state, *, lc):
    c = pl.program_id(1)

    @pl.when(c == 0)
    def _():
        state[...] = jnp.zeros(state.shape, F32)

    row = lax.broadcasted_iota(I32, (lc, lc), 0)
    col = lax.broadcasted_iota(I32, (lc, lc), 1)
    causal = row >= col
    tri = jnp.where(causal, 1.0, 0.0).astype(BF16)
    triT = jnp.where(row <= col, 1.0, 0.0).astype(BF16)
    ah, am, al = _split3(adt_ref[...])
    a_cum = _dot(tri, ah) + (_dot(tri, am) + _dot(tri, al))
    th, tmid, tl = _split3(adtT_ref[...])
    a_cumT = _dot(th, triT) + (_dot(tmid, triT) + _dot(tl, triT))
    dt = dt_ref[...]
    a_last = a_cum[lc - 1:lc, :]
    e_cum = jnp.exp(a_cum)
    d2e = jnp.exp(a_last - a_cum)
    cdec = jnp.exp(a_last)
    dskip = d_ref[...]
    hx = hx_ref[...]
    dt_x = _dot(dt.astype(BF16), hx)
    dtd2e_x = _dot((dt * d2e).astype(BF16), hx)
    ecum_x = _dot(e_cum.astype(BF16), hx)
    lane2 = lax.broadcasted_iota(I32, (lc, SSM_GROUP_W), 1)
    for g in range(SSM_GROUPS):
        xg = xbc_ref[:, g * SSM_GROUP_W:(g + 1) * SSM_GROUP_W].astype(F32)
        b0 = SSM_INNER + g * SSM_STATE
        c0 = SSM_INNER + SSM_BC + g * SSM_STATE
        bg = xbc_ref[:, b0:b0 + SSM_STATE]
        cg = xbc_ref[:, c0:c0 + SSM_STATE]
        gcols = slice(g * SSM_GROUP_W, (g + 1) * SSM_GROUP_W)
        xdt_b = (xg * dt_x[:, gcols]).astype(BF16)
        cb = _dot_nt(cg, bg)
        y = jnp.zeros((lc, SSM_GROUP_W), F32)
        for j in range(SSM_HPG):
            hd = g * SSM_HPG + j
            seg = a_cum[:, hd:hd + 1] - a_cumT[hd:hd + 1, :]
            decay = jnp.exp(jnp.where(causal, seg, -jnp.inf))
            m = (cb * decay).astype(BF16)
            in_head = (lane2 >= j * SSM_HEAD_DIM) & (lane2 < (j + 1) * SSM_HEAD_DIM)
            y = y + _dot(m, jnp.where(in_head, xdt_b, jnp.zeros_like(xdt_b)))
        st = state[g]
        y = y + _dot(cg, st.astype(BF16)) * ecum_x[:, gcols]
        xd2e = (xg * dtd2e_x[:, gcols]).astype(BF16)
        state[g] = st * _expand_heads(cdec, g, 1) + _dot_tn(bg, xd2e)
        y = y + _expand_heads(dskip, g, 1) * xg
        zg = z_ref[:, g * SSM_GROUP_W:(g + 1) * SSM_GROUP_W].astype(F32)
        yz = y * (zg * _sigmoid(zg))
        yn = _rms(yz) * gn_ref[:, g * SSM_GROUP_W:(g + 1) * SSM_GROUP_W]
        yn_ref[:, g * SSM_GROUP_W:(g + 1) * SSM_GROUP_W] = yn.astype(BF16)


def _ssd_scan(z, xbc, dt, adt, adtT, d_skip, g_norm, bsz, seqlen, lc):
    n = z.shape[0]
    nc = seqlen // lc
    pad = jnp.zeros((LANES - SSM_HEADS,), F32)
    d_row = jnp.concatenate([d_skip, pad]).reshape(1, LANES)
    rows = lambda b, c: (b * nc + c, 0)
    vec = lambda b, c: (0, 0)
    head_of_col = jnp.arange(SSM_INNER, dtype=I32) // SSM_HEAD_DIM
    head_expand = (jnp.arange(LANES, dtype=I32)[:, None] == head_of_col[None, :]).astype(BF16)
    return pl.pallas_call(
        functools.partial(_ssd_scan_kernel, lc=lc),
        out_shape=jax.ShapeDtypeStruct((n, SSM_INNER), BF16),
        grid=(bsz, nc),
        in_specs=[
            pl.BlockSpec((lc, SSM_CONV_DIM), rows),
            pl.BlockSpec((lc, SSM_INNER), rows),
            pl.BlockSpec((lc, LANES), rows),
            pl.BlockSpec((lc, LANES), rows),
            pl.BlockSpec((None, SSM_HEADS, lc), lambda b, c: (b, 0, c)),
            pl.BlockSpec((1, LANES), vec),
            pl.BlockSpec((1, SSM_INNER), vec),
            pl.BlockSpec((LANES, SSM_INNER), vec),
        ],
        out_specs=pl.BlockSpec((lc, SSM_INNER), rows),
        scratch_shapes=[pltpu.VMEM((SSM_GROUPS, SSM_STATE, SSM_GROUP_W), F32)],
        compiler_params=_cparams("arbitrary", "arbitrary"),
        name="ssd_scan",
    )(xbc, z, dt, adt, adtT, d_row, g_norm.reshape(1, -1), head_expand)


def kernel(x, c, positions, w_mod, b_mod, g_mix_norm, g_ffn_norm, mla_w_in, mla_g_q, mla_g_kv, mla_w_q_up, mla_w_kv_up, mla_w_out, ssm_w_in, ssm_conv_w, ssm_conv_b, ssm_dt_bias, ssm_a_log, ssm_d, ssm_g_norm, ssm_w_out, moe_w_router, moe_b_router, moe_w_gate_up, moe_b_gate_up, moe_w_down, moe_b_down, g_final):
    bsz, seqlen, d = x.shape
    depth = w_mod.shape[0]
    n = bsz * seqlen
    tm = min(512, seqlen)
    mod = _modulation(c, w_mod, b_mod)
    mod = mod.reshape(depth, 6, bsz, 1, d)
    pos_f = positions.astype(F32).reshape(n, 1)
    xc = x.reshape(n, d)
    for i in range(depth):
        sh1, sc1, gt1, sh2, sc2, gt2 = [mod[i, j] for j in range(6)]
        j = i // 2
        if i % 2 == 0:
            q, k, v = _mla_pre(xc, pos_f, sh1, sc1, g_mix_norm[i], mla_w_in[j], mla_g_q[j], mla_g_kv[j],
                               mla_w_q_up[j], mla_w_kv_up[j], bsz, seqlen, tm)
            o = _attention(q, k, v, tm).reshape(n, MLA_HEADS * V_HEAD)
            w_out = mla_w_out[j]
        else:
            tms = min(256, seqlen)
            z, xbc, dt, adt, adtT = _ssd_pre(xc, sh1, sc1, g_mix_norm[i], ssm_w_in[j], ssm_conv_w[j],
                                             ssm_conv_b[j], ssm_dt_bias[j], ssm_a_log[j], bsz, seqlen, tms)
            o = _ssd_scan(z, xbc, dt, adt, adtT, ssm_d[j], ssm_g_norm[j], bsz, seqlen, min(256, seqlen))
            w_out = ssm_w_out[j]
        x1, h2, meta, gates, counts = _post_mixer(xc, o, w_out, gt1, sh2, sc2, g_ffn_norm[i],
                                                  moe_w_router[i], moe_b_router[i], seqlen, tm)
        xc = _moe(x1, h2, meta, gates, counts, gt2, g_final, moe_w_gate_up, moe_b_gate_up,
                  moe_w_down, moe_b_down, i, seqlen, final=(i == depth - 1))
    return xc.reshape(bsz, seqlen, d)
```

```python
import functools
import math

import jax
import jax.numpy as jnp
from jax import lax
from jax.experimental import pallas as pl
from jax.experimental.pallas import tpu as pltpu

F32 = jnp.float32
BF16 = jnp.bfloat16
I32 = jnp.int32
U32 = jnp.uint32

NORM_EPS = 1e-6
MLA_HEADS = 8
Q_LORA = 256
KV_LORA = 256
QK_NOPE = 128
QK_ROPE = 64
V_HEAD = 128
QK_HEAD = QK_NOPE + QK_ROPE
ROPE_THETA = 10000.0
MLA_SCALE = QK_HEAD ** -0.5
SSM_HEAD_DIM = 64
SSM_GROUPS = 8
SSM_HPG = 4
SSM_HEADS = SSM_GROUPS * SSM_HPG
SSM_STATE = 128
SSM_CONV = 4
SSM_GROUP_W = SSM_HPG * SSM_HEAD_DIM
SSM_INNER = SSM_GROUPS * SSM_GROUP_W
SSM_BC = SSM_GROUPS * SSM_STATE
SSM_CONV_DIM = SSM_INNER + 2 * SSM_BC
N_EXPERTS = 32
TOP_K = 4
SWIGLU_LIMIT = 7.0
SWIGLU_ALPHA = 1.702
MOE_BLOCK = 256

LANES = 128
VMEM_LIMIT = 56 * 1024 * 1024


def _cparams(*sem):
    return pltpu.CompilerParams(dimension_semantics=tuple(sem), vmem_limit_bytes=VMEM_LIMIT)


def _dot(a, b):
    return jnp.dot(a, b, preferred_element_type=F32)


def _dot_nt(a, b):
    return lax.dot_general(a, b, (((1,), (1,)), ((), ())), preferred_element_type=F32)


def _dot_tn(a, b):
    return lax.dot_general(a, b, (((0,), (0,)), ((), ())), preferred_element_type=F32)


def _split3(a):
    hi = a.astype(BF16)
    r1 = a - hi.astype(F32)
    mid = r1.astype(BF16)
    lo = (r1 - mid.astype(F32)).astype(BF16)
    return hi, mid, lo


def _dot_f32ish(a, b):
    ah, am, _ = _split3(a)
    bh, bm, _ = _split3(b)
    return _dot(ah, bh) + (_dot(ah, bm) + _dot(am, bh))


def _pack_bf16_pair(lo, hi):
    lo_b = lax.bitcast_convert_type(lo.astype(BF16).astype(F32), U32)
    hi_b = lax.bitcast_convert_type(hi.astype(BF16).astype(F32), U32)
    return hi_b | lax.shift_right_logical(lo_b, jnp.uint32(16))


def _unpack_bf16_pair(w):
    lo = lax.bitcast_convert_type(lax.shift_left(w, jnp.uint32(16)), F32)
    hi = lax.bitcast_convert_type(w & jnp.uint32(0xFFFF0000), F32)
    return lo, hi


ROW_CHUNKS = 4


def _chunk_copies(buf2d, rows_hbm, r0, nrows, sem, *, to_hbm):
    cps = []
    for c in range(ROW_CHUNKS):
        v = buf2d.at[:, pl.ds(c * LANES, LANES)]
        h = rows_hbm.at[pl.ds(r0, nrows), c]
        cps.append(pltpu.make_async_copy(v, h, sem) if to_hbm else pltpu.make_async_copy(h, v, sem))
    return cps


def _sigmoid(x):
    return 1.0 / (1.0 + jnp.exp(-x))


def _rms(x):
    return x * lax.rsqrt(jnp.mean(x * x, axis=-1, keepdims=True) + NORM_EPS)


def _mod_kernel(c_ref, w_ref, b_ref, o_ref):
    c = c_ref[...]
    cond = c * _sigmoid(c)
    o_ref[...] = _dot_f32ish(cond, w_ref[...]) + b_ref[...]


def _modulation(c, w_mod, b_mod):
    depth, d, d6 = w_mod.shape
    bsz = c.shape[0]
    nj = d6 // d
    return pl.pallas_call(
        _mod_kernel,
        out_shape=jax.ShapeDtypeStruct((depth, nj, bsz, d), F32),
        grid=(depth, nj),
        in_specs=[
            pl.BlockSpec((bsz, d), lambda l, j: (0, 0)),
            pl.BlockSpec((None, d, d), lambda l, j: (l, 0, j)),
            pl.BlockSpec((None, 1, d), lambda l, j: (l, 0, j)),
        ],
        out_specs=pl.BlockSpec((None, None, bsz, d), lambda l, j: (l, j, 0, 0)),
        compiler_params=_cparams("arbitrary", "arbitrary"),
        name="mod",
    )(c, w_mod, b_mod.reshape(depth, 1, d6))


def _mla_pre_kernel(x_ref, pos_ref, sh_ref, sc_ref, g_ref, win_ref, gq_ref, gkv_ref, wq_ref, wkv_ref,
                    invf_ref, q_ref, k_ref, v_ref):
    x = x_ref[...]
    h = _rms(x) * g_ref[...] * (1.0 + sc_ref[...]) + sh_ref[...]
    lat = _dot(h.astype(BF16), win_ref[...])
    q_lat = _rms(lat[:, :Q_LORA]) * gq_ref[...]
    kv_lat = _rms(lat[:, Q_LORA:Q_LORA + KV_LORA]) * gkv_ref[...]
    ang = pos_ref[...] * invf_ref[...]
    cs = jnp.cos(ang)
    sn = jnp.sin(ang)
    o = Q_LORA + KV_LORA
    k_rope = lat[:, o:o + LANES] * cs + lat[:, o + LANES:o + 2 * LANES] * sn
    qq = _dot(q_lat.astype(BF16), wq_ref[...])
    kv = _dot(kv_lat.astype(BF16), wkv_ref[...])
    rot0 = MLA_HEADS * 2 * LANES
    k_rope_b = k_rope[:, :QK_ROPE].astype(BF16)
    for hd in range(MLA_HEADS):
        c0 = hd * 2 * LANES
        q_nope = qq[:, c0:c0 + LANES] * MLA_SCALE
        q_rope = (qq[:, c0 + LANES:c0 + 2 * LANES] * cs
                  + qq[:, rot0 + hd * LANES:rot0 + (hd + 1) * LANES] * sn) * MLA_SCALE
        q_ref[hd, :, 0:QK_NOPE] = q_nope.astype(BF16)
        q_ref[hd, :, QK_NOPE:QK_HEAD] = q_rope[:, :QK_ROPE].astype(BF16)
        k_ref[hd, :, 0:QK_NOPE] = kv[:, c0:c0 + LANES].astype(BF16)
        k_ref[hd, :, QK_NOPE:QK_HEAD] = k_rope_b
        v_ref[hd] = kv[:, c0 + LANES:c0 + 2 * LANES].astype(BF16)


def _rot_half_cols(w):
    half = QK_ROPE // 2
    return jnp.concatenate([-w[..., half:], w[..., :half]], axis=-1)


def _mla_pre(x2d, pos_f, sh, sc, g, w_in, g_q, g_kv, w_q_up, w_kv_up, bsz, seqlen, tm):
    n, d = x2d.shape
    hh = MLA_HEADS
    o = Q_LORA + KV_LORA
    wr = w_in[:, o:o + QK_ROPE]
    zpad = jnp.zeros((d, LANES - QK_ROPE), F32)
    w_in_ext = jnp.concatenate([w_in[:, :o], wr, zpad, _rot_half_cols(wr), zpad], axis=1).astype(BF16)
    wq = w_q_up.reshape(Q_LORA, hh, QK_HEAD)
    zq = jnp.zeros((Q_LORA, hh, LANES - QK_ROPE), F32)
    wq_main = jnp.concatenate([wq, zq], axis=-1).reshape(Q_LORA, hh * 2 * LANES)
    wq_rot = jnp.concatenate([_rot_half_cols(wq[..., QK_NOPE:]), zq], axis=-1).reshape(Q_LORA, hh * LANES)
    wq_ext = jnp.concatenate([wq_main, wq_rot], axis=1).astype(BF16)
    inv_freq = 1.0 / (ROPE_THETA ** (jnp.arange(0, QK_ROPE, 2, dtype=F32) / QK_ROPE))
    invf = jnp.concatenate([inv_freq, inv_freq, jnp.zeros((LANES - QK_ROPE,), F32)]).reshape(1, LANES)
    tpb = seqlen // tm
    vec = lambda i: (0, 0)
    outs = pl.pallas_call(
        _mla_pre_kernel,
        out_shape=(
            jax.ShapeDtypeStruct((bsz, hh, seqlen, QK_HEAD), BF16),
            jax.ShapeDtypeStruct((bsz, hh, seqlen, QK_HEAD), BF16),
            jax.ShapeDtypeStruct((bsz, hh, seqlen, V_HEAD), BF16),
        ),
        grid=(n // tm,),
        in_specs=[
            pl.BlockSpec((tm, d), lambda i: (i, 0)),
            pl.BlockSpec((tm, 1), lambda i: (i, 0)),
            pl.BlockSpec((None, 1, d), lambda i: (i // tpb, 0, 0)),
            pl.BlockSpec((None, 1, d), lambda i: (i // tpb, 0, 0)),
            pl.BlockSpec((1, d), vec),
            pl.BlockSpec(w_in_ext.shape, vec),
            pl.BlockSpec((1, Q_LORA), vec),
            pl.BlockSpec((1, KV_LORA), vec),
            pl.BlockSpec(wq_ext.shape, vec),
            pl.BlockSpec((KV_LORA, hh * 2 * LANES), vec),
            pl.BlockSpec((1, LANES), vec),
        ],
        out_specs=(
            pl.BlockSpec((None, hh, tm, QK_HEAD), lambda i: (i // tpb, 0, i % tpb, 0)),
            pl.BlockSpec((None, hh, tm, QK_HEAD), lambda i: (i // tpb, 0, i % tpb, 0)),
            pl.BlockSpec((None, hh, tm, V_HEAD), lambda i: (i // tpb, 0, i % tpb, 0)),
        ),
        compiler_params=_cparams("arbitrary"),
        name="mla_pre",
    )(x2d, pos_f, sh, sc, g.reshape(1, d), w_in_ext, g_q.reshape(1, -1), g_kv.reshape(1, -1), wq_ext,
      w_kv_up.astype(BF16), invf)
    return outs


ATTN_HEADS_PER_STEP = 4


def _attn_kernel(q_ref, k_ref, v_ref, o_ref, m_scr, acc_scr, *, tq, hp):
    qi = pl.program_id(2)
    m_scr[...] = jnp.full(m_scr.shape, -jnp.inf, F32)
    acc_scr[...] = jnp.zeros(acc_scr.shape, F32)

    def block(hd, r0, tk, masked):
        k = k_ref[hd, pl.ds(r0, tk), :]
        v_ext = jnp.concatenate([v_ref[hd, pl.ds(r0, tk), :], jnp.ones((tk, V_HEAD), BF16)], axis=1)
        s = _dot_nt(q_ref[hd], k)
        if masked:
            row = lax.broadcasted_iota(I32, (tq, tk), 0)
            col = lax.broadcasted_iota(I32, (tq, tk), 1)
            s = jnp.where(row >= col, s, -jnp.inf)
        m_prev = m_scr[hd]
        m_new = jnp.maximum(m_prev, jnp.max(s, axis=-1, keepdims=True))
        alpha = jnp.exp(m_prev - m_new)
        p = jnp.exp(s - m_new).astype(BF16)
        acc_scr[hd] = alpha * acc_scr[hd] + _dot(p, v_ext)
        m_scr[hd] = m_new

    def body(j, carry):
        for hd in range(hp):
            block(hd, pl.multiple_of(j * (2 * tq), 2 * tq), 2 * tq, False)
        return carry

    lax.fori_loop(0, qi // 2, body, 0)

    @pl.when(qi % 2 == 1)
    def _():
        for hd in range(hp):
            block(hd, pl.multiple_of((qi - 1) * tq, tq), tq, False)

    for hd in range(hp):
        block(hd, pl.multiple_of(qi * tq, tq), tq, True)
    for hd in range(hp):
        acc = acc_scr[hd]
        o_ref[:, hd * V_HEAD:(hd + 1) * V_HEAD] = (acc[:, :V_HEAD] / acc[:, V_HEAD:]).astype(o_ref.dtype)


def _attention(q, k, v, tq):
    bsz, hh, seqlen, _ = q.shape
    hp = ATTN_HEADS_PER_STEP
    return pl.pallas_call(
        functools.partial(_attn_kernel, tq=tq, hp=hp),
        out_shape=jax.ShapeDtypeStruct((bsz, seqlen, hh * V_HEAD), BF16),
        grid=(bsz, hh // hp, seqlen // tq),
        in_specs=[
            pl.BlockSpec((None, hp, tq, QK_HEAD), lambda b, h, i: (b, h, i, 0)),
            pl.BlockSpec((None, hp, seqlen, QK_HEAD), lambda b, h, i: (b, h, 0, 0)),
            pl.BlockSpec((None, hp, seqlen, V_HEAD), lambda b, h, i: (b, h, 0, 0)),
        ],
        out_specs=pl.BlockSpec((None, tq, hp * V_HEAD), lambda b, h, i: (b, i, h)),
        scratch_shapes=[
            pltpu.VMEM((hp, tq, 1), F32),
            pltpu.VMEM((hp, tq, 2 * V_HEAD), F32),
        ],
        compiler_params=_cparams("arbitrary", "arbitrary", "arbitrary"),
        name="attn",
    )(q, k, v)


def _post_mixer_kernel(x_ref, o_ref, wout_ref, gt1_ref, sh_ref, sc_ref, g_ref, wr_ref, br_ref,
                       x1_ref, h2_hbm, meta_ref, gate_ref, cnt_ref, carry_scr, pbuf, psem, *, tm):
    i = pl.program_id(0)

    @pl.when(i == 0)
    def _():
        carry_scr[...] = jnp.zeros(carry_scr.shape, F32)

    y = _dot(o_ref[...], wout_ref[...])
    x1 = x_ref[...] + gt1_ref[...] * y
    x1_ref[...] = x1
    h2 = _rms(x1) * g_ref[...] * (1.0 + sc_ref[...]) + sh_ref[...]
    half = h2.shape[1] // 2

    @pl.when(i > 0)
    def _():
        for cp in _chunk_copies(pbuf, h2_hbm, (i - 1) * tm, tm, psem, to_hbm=True):
            cp.wait()

    pbuf[...] = _pack_bf16_pair(h2[:, :half], h2[:, half:])
    for cp in _chunk_copies(pbuf, h2_hbm, i * tm, tm, psem, to_hbm=True):
        cp.start()

    logits = _dot_f32ish(h2, wr_ref[...]) + br_ref[...]
    lane = lax.broadcasted_iota(I32, (tm, LANES), 1).astype(F32)
    work = logits
    idxs, vals = [], []
    for _ in range(TOP_K):
        mx = jnp.max(work, axis=-1, keepdims=True)
        idx = jnp.min(jnp.where(work == mx, lane, float(LANES)), axis=-1, keepdims=True)
        idxs.append(idx)
        vals.append(mx)
        work = jnp.where(lane == idx, -jnp.inf, work)
    exps = [jnp.exp(vk - vals[0]) for vk in vals]
    denom = exps[0] + exps[1] + exps[2] + exps[3]
    onehot = jnp.zeros((tm, LANES), F32)
    for idx in idxs:
        onehot = onehot + jnp.where(lane == idx, 1.0, 0.0)
    row = lax.broadcasted_iota(I32, (tm, tm), 0)
    col = lax.broadcasted_iota(I32, (tm, tm), 1)
    ltri = jnp.where(row > col, 1.0, 0.0).astype(BF16)
    cum = _dot(ltri, onehot.astype(BF16)) + carry_scr[...]
    meta = jnp.zeros((tm, LANES), F32)
    gates = jnp.zeros((tm, LANES), F32)
    for kk in range(TOP_K):
        rank = jnp.sum(jnp.where(lane == idxs[kk], cum, 0.0), axis=-1, keepdims=True)
        meta = jnp.where(lane == float(kk), idxs[kk], meta)
        meta = jnp.where(lane == float(TOP_K + kk), rank, meta)
        gates = jnp.where(lane == float(kk), exps[kk] / denom, gates)
    meta_ref[...] = jnp.transpose(meta)[:2 * TOP_K, :].astype(I32)
    gate_ref[...] = gates
    carry = carry_scr[...] + jnp.sum(onehot, axis=0, keepdims=True)
    carry_scr[...] = carry
    cnt_ref[...] = carry.astype(I32)

    @pl.when(i == pl.num_programs(0) - 1)
    def _():
        for cp in _chunk_copies(pbuf, h2_hbm, i * tm, tm, psem, to_hbm=True):
            cp.wait()


def _post_mixer(x2d, o2d, w_out, gt1, sh2, sc2, g_ffn, w_router, b_router, seqlen, tm):
    n, d = x2d.shape
    kdim = o2d.shape[1]
    tpb = seqlen // tm
    wr = jnp.concatenate([w_router, jnp.zeros((d, LANES - N_EXPERTS), F32)], axis=1)
    br = jnp.concatenate([b_router, jnp.full((LANES - N_EXPERTS,), -1e30, F32)]).reshape(1, LANES)
    vec = lambda i: (0, 0)
    bvec = lambda i: (i // tpb, 0, 0)
    return pl.pallas_call(
        functools.partial(_post_mixer_kernel, tm=tm),
        out_shape=(
            jax.ShapeDtypeStruct((n, d), F32),
            jax.ShapeDtypeStruct((n, ROW_CHUNKS, LANES), U32),
            jax.ShapeDtypeStruct((2 * TOP_K, n), I32),
            jax.ShapeDtypeStruct((n, LANES), F32),
            jax.ShapeDtypeStruct((1, LANES), I32),
        ),
        grid=(n // tm,),
        in_specs=[
            pl.BlockSpec((tm, d), lambda i: (i, 0)),
            pl.BlockSpec((tm, kdim), lambda i: (i, 0)),
            pl.BlockSpec((kdim, d), vec),
            pl.BlockSpec((None, 1, d), bvec),
            pl.BlockSpec((None, 1, d), bvec),
            pl.BlockSpec((None, 1, d), bvec),
            pl.BlockSpec((1, d), vec),
            pl.BlockSpec((d, LANES), vec),
            pl.BlockSpec((1, LANES), vec),
        ],
        out_specs=(
            pl.BlockSpec((tm, d), lambda i: (i, 0)),
            pl.BlockSpec(memory_space=pl.ANY),
            pl.BlockSpec((2 * TOP_K, tm), lambda i: (0, i)),
            pl.BlockSpec((tm, LANES), lambda i: (i, 0)),
            pl.BlockSpec((1, LANES), vec),
        ),
        scratch_shapes=[pltpu.VMEM((1, LANES), F32), pltpu.VMEM((tm, d // 2), U32), pltpu.SemaphoreType.DMA],
        compiler_params=_cparams("arbitrary"),
        name="post_mixer",
    )(x2d, o2d, w_out.astype(BF16), gt1, sh2, sc2, g_ffn.reshape(1, d), wr, br)


ROW_UNROLL = 8


def _row_copy(src, s, dst, t, sem):
    return pltpu.make_async_copy(src.at[s], dst.at[t], sem)


def _dest_kernel(pstart_ref, meta_ref, dest_ref):
    e = meta_ref[0:TOP_K, :]
    dest = meta_ref[TOP_K:2 * TOP_K, :]
    for j in range(N_EXPERTS):
        dest = dest + jnp.where(e == j, pstart_ref[j], 0)
    dest_ref[...] = dest


def _dest_rows(meta_t, pstart):
    n = meta_t.shape[1]
    tn = min(4096, n)
    grid_spec = pltpu.PrefetchScalarGridSpec(
        num_scalar_prefetch=1,
        grid=(n // tn,),
        in_specs=[pl.BlockSpec((2 * TOP_K, tn), lambda i, ps: (0, i))],
        out_specs=pl.BlockSpec((TOP_K, tn), lambda i, ps: (0, i)),
    )
    return pl.pallas_call(
        _dest_kernel,
        out_shape=jax.ShapeDtypeStruct((TOP_K, n), I32),
        grid_spec=grid_spec,
        compiler_params=_cparams("arbitrary"),
        name="dest_rows",
    )(pstart, meta_t)


def _dispatch_kernel(pstart_ref, cnt_ref, dest_ref, h_ref, xs_hbm, zbuf, sem, zsem, *, tm):
    i = pl.program_id(0)

    @pl.when(i == 0)
    def _():
        zbuf[...] = jnp.zeros(zbuf.shape, U32)

        def per_expert(e, carry):
            lo = pstart_ref[e] + cnt_ref[e]
            hi = pstart_ref[e + 1]

            def start(j, c):
                _row_copy(zbuf, 0, xs_hbm, j, zsem).start()
                return c

            def wait(j, c):
                _row_copy(zbuf, 0, xs_hbm, j, zsem).wait()
                return c

            lax.fori_loop(lo, hi, start, 0)
            lax.fori_loop(lo, hi, wait, 0)
            return carry

        lax.fori_loop(0, N_EXPERTS, per_expert, 0)

    def start(tt, c):
        for u in range(ROW_UNROLL):
            t = tt * ROW_UNROLL + u
            for kk in range(TOP_K):
                _row_copy(h_ref, t, xs_hbm, dest_ref[kk, t], sem).start()
        return c

    def wait(tt, c):
        for u in range(ROW_UNROLL * TOP_K):
            _row_copy(h_ref, 0, xs_hbm, 0, sem).wait()
        return c

    lax.fori_loop(0, tm // ROW_UNROLL, start, 0)
    lax.fori_loop(0, tm // ROW_UNROLL, wait, 0)


def _dispatch(h2p, dest_t, pstart, counts, n_rows, tm):
    n = h2p.shape[0]
    grid_spec = pltpu.PrefetchScalarGridSpec(
        num_scalar_prefetch=2,
        grid=(n // tm,),
        in_specs=[
            pl.BlockSpec((TOP_K, tm), lambda i, ps, cn: (0, i), memory_space=pltpu.SMEM),
            pl.BlockSpec((tm, ROW_CHUNKS, LANES), lambda i, ps, cn: (i, 0, 0)),
        ],
        out_specs=pl.BlockSpec(memory_space=pl.ANY),
        scratch_shapes=[
            pltpu.VMEM((1, ROW_CHUNKS, LANES), U32),
            pltpu.SemaphoreType.DMA,
            pltpu.SemaphoreType.DMA,
        ],
    )
    return pl.pallas_call(
        functools.partial(_dispatch_kernel, tm=tm),
        out_shape=jax.ShapeDtypeStruct((n_rows, ROW_CHUNKS, LANES), U32),
        grid_spec=grid_spec,
        compiler_params=pltpu.CompilerParams(dimension_semantics=("arbitrary",), vmem_limit_bytes=VMEM_LIMIT,
                                             has_side_effects=True),
        name="dispatch",
    )(pstart, counts, dest_t, h2p)


def _ffn_kernel(be_ref, nu_ref, xs_hbm, wgu_ref, bgu_ref, wd_ref, bd_ref, ys_hbm, wgu_b, wd_b, xbuf, ybuf,
                xsem, ysem, *, ff):
    i = pl.program_id(0)
    nu = nu_ref[0]
    slot = i % 2
    prev = be_ref[jnp.maximum(i - 1, 0)]
    changed = jnp.logical_or(i == 0, be_ref[i] != prev)

    def x_copies(step, s):
        return _chunk_copies(xbuf.at[s], xs_hbm, step * MOE_BLOCK, MOE_BLOCK, xsem.at[s], to_hbm=False)

    def y_copies(step, s):
        return _chunk_copies(ybuf.at[s], ys_hbm, step * MOE_BLOCK, MOE_BLOCK, ysem.at[s], to_hbm=True)

    @pl.when(i == 0)
    def _():
        for cp in x_copies(0, 0):
            cp.start()

    @pl.when(i + 1 < nu)
    def _():
        for cp in x_copies(i + 1, 1 - slot):
            cp.start()

    @pl.when(changed)
    def _():
        wgu_b[...] = wgu_ref[...].astype(BF16)
        wd_b[...] = wd_ref[...].astype(BF16)

    @pl.when(i < nu)
    def _():
        for cp in x_copies(i, slot):
            cp.wait()
        x_lo, x_hi = _unpack_bf16_pair(xbuf[slot])
        dh = x_lo.shape[1]
        gu = (_dot(x_lo.astype(BF16), wgu_b[0:dh, :]) + _dot(x_hi.astype(BF16), wgu_b[dh:2 * dh, :])
              + bgu_ref[...])
        g = jnp.minimum(gu[:, :ff], SWIGLU_LIMIT)
        lin = jnp.clip(gu[:, ff:], -SWIGLU_LIMIT, SWIGLU_LIMIT)
        act = g * _sigmoid(SWIGLU_ALPHA * g) * (lin + 1.0)
        y = _dot(act.astype(BF16), wd_b[...]) + bd_ref[...]

        @pl.when(i >= 2)
        def _():
            for cp in y_copies(i - 2, slot):
                cp.wait()

        ybuf[slot] = _pack_bf16_pair(y[:, :dh], y[:, dh:])
        for cp in y_copies(i, slot):
            cp.start()

    @pl.when(i == nu - 1)
    def _():
        for cp in y_copies(i, slot):
            cp.wait()

        @pl.when(i >= 1)
        def _():
            for cp in y_copies(i - 1, 1 - slot):
                cp.wait()


def _ffn(xs, blk_e, n_used, w_gate_up, b_gate_up, w_down, b_down, layer):
    n_rows = xs.shape[0]
    depth, ne, d, ff2 = w_gate_up.shape
    ff = ff2 // 2
    n_blk = n_rows // MOE_BLOCK
    row_buf = pltpu.VMEM((2, MOE_BLOCK, ROW_CHUNKS * LANES), U32)
    grid_spec = pltpu.PrefetchScalarGridSpec(
        num_scalar_prefetch=2,
        grid=(n_blk,),
        in_specs=[
            pl.BlockSpec(memory_space=pl.ANY),
            pl.BlockSpec((None, None, d, ff2), lambda i, be, nu: (layer, be[i], 0, 0)),
            pl.BlockSpec((None, None, 1, ff2), lambda i, be, nu: (layer, be[i], 0, 0)),
            pl.BlockSpec((None, None, ff, d), lambda i, be, nu: (layer, be[i], 0, 0)),
            pl.BlockSpec((None, None, 1, d), lambda i, be, nu: (layer, be[i], 0, 0)),
        ],
        out_specs=pl.BlockSpec(memory_space=pl.ANY),
        scratch_shapes=[pltpu.VMEM((d, ff2), BF16), pltpu.VMEM((ff, d), BF16), row_buf, row_buf,
                        pltpu.SemaphoreType.DMA((2,)), pltpu.SemaphoreType.DMA((2,))],
    )
    return pl.pallas_call(
        functools.partial(_ffn_kernel, ff=ff),
        out_shape=jax.ShapeDtypeStruct((n_rows, ROW_CHUNKS, LANES), U32),
        grid_spec=grid_spec,
        compiler_params=_cparams("arbitrary"),
        name="ffn",
    )(blk_e, n_used, xs, w_gate_up, b_gate_up.reshape(depth, ne, 1, ff2), w_down,
      b_down.reshape(depth, ne, 1, d))


def _combine_kernel(dest_ref, x1_ref, gate_ref, gt2_ref, gfin_ref, ys_hbm, o_ref, buf, sem, *, tm, final):
    def start(tt, c):
        for u in range(ROW_UNROLL):
            for kk in range(TOP_K):
                src = ys_hbm.at[dest_ref[kk, tt * ROW_UNROLL + u]]
                pltpu.make_async_copy(src, buf.at[kk, tt, :, u], sem).start()
        return c

    def wait(tt, c):
        for u in range(ROW_UNROLL * TOP_K):
            pltpu.make_async_copy(ys_hbm.at[0], buf.at[0, 0, :, 0], sem).wait()
        return c

    lax.fori_loop(0, tm // ROW_UNROLL, start, 0)
    lax.fori_loop(0, tm // ROW_UNROLL, wait, 0)
    gates = gate_ref[...]
    acc_lo = acc_hi = None
    for kk in range(TOP_K):
        rows = jnp.concatenate([buf[kk, :, c, :, :].reshape(tm, LANES) for c in range(ROW_CHUNKS)], axis=1)
        lo, hi = _unpack_bf16_pair(rows)
        gk = gates[:, kk:kk + 1]
        acc_lo = gk * lo if acc_lo is None else acc_lo + gk * lo
        acc_hi = gk * hi if acc_hi is None else acc_hi + gk * hi
    x2 = x1_ref[...] + gt2_ref[...] * jnp.concatenate([acc_lo, acc_hi], axis=1)
    if final:
        x2 = _rms(x2) * gfin_ref[...]
    o_ref[...] = x2


def _combine(x1, gates, gt2, g_final, ys, dest_t, seqlen, tm, final):
    n, d = x1.shape
    tpb = seqlen // tm
    grid_spec = pltpu.PrefetchScalarGridSpec(
        num_scalar_prefetch=0,
        grid=(n // tm,),
        in_specs=[
            pl.BlockSpec((TOP_K, tm), lambda i: (0, i), memory_space=pltpu.SMEM),
            pl.BlockSpec((tm, d), lambda i: (i, 0)),
            pl.BlockSpec((tm, LANES), lambda i: (i, 0)),
            pl.BlockSpec((None, 1, d), lambda i: (i // tpb, 0, 0)),
            pl.BlockSpec((1, d), lambda i: (0, 0)),
            pl.BlockSpec(memory_space=pl.ANY),
        ],
        out_specs=pl.BlockSpec((tm, d), lambda i: (i, 0)),
        scratch_shapes=[pltpu.VMEM((TOP_K, tm // ROW_UNROLL, ROW_CHUNKS, ROW_UNROLL, LANES), U32),
                        pltpu.SemaphoreType.DMA],
    )
    return pl.pallas_call(
        functools.partial(_combine_kernel, tm=tm, final=final),
        out_shape=jax.ShapeDtypeStruct((n, d), F32),
        grid_spec=grid_spec,
        compiler_params=_cparams("arbitrary"),
        name="combine",
    )(dest_t, x1, gates, gt2, g_final.reshape(1, d), ys)


def _moe(x1, h2p, meta_t, gates, counts, gt2, g_final, w_gate_up, b_gate_up, w_down, b_down, layer, seqlen,
         final):
    n, d = x1.shape
    n_pair = n * TOP_K
    n_rows = -(-n_pair // MOE_BLOCK) * MOE_BLOCK + N_EXPERTS * MOE_BLOCK
    n_blk = n_rows // MOE_BLOCK
    cnt = counts[0, :N_EXPERTS]
    padded = (cnt + MOE_BLOCK - 1) // MOE_BLOCK * MOE_BLOCK
    pad_end = jnp.cumsum(padded)
    pstart = jnp.concatenate([pad_end - padded, pad_end[-1:]]).astype(I32)
    n_used = (pad_end[-1:] // MOE_BLOCK).astype(I32)
    blk_start = jnp.arange(n_blk, dtype=I32) * MOE_BLOCK
    blk_e = jnp.minimum(jnp.sum(blk_start[:, None] >= pad_end[None, :], axis=1), N_EXPERTS - 1).astype(I32)
    tm = min(256, seqlen)
    dest_t = _dest_rows(meta_t, pstart)
    xs = _dispatch(h2p, dest_t, pstart, cnt, n_rows, tm)
    ys = _ffn(xs, blk_e, n_used, w_gate_up, b_gate_up, w_down, b_down, layer)
    return _combine(x1, gates, gt2, g_final, ys, dest_t, seqlen, tm, final)


def _softplus(x):
    return jnp.maximum(x, 0.0) + jnp.log(1.0 + jnp.exp(-jnp.abs(x)))


def _ssd_pre_kernel(x_ref, sh_ref, sc_ref, g_ref, wz_ref, wx_ref, wdt_ref, wdtT_ref, cw_ref, cb_ref,
                    dtb_ref, dtbT_ref, a_ref, aT_ref,
                    z_ref, xbc_ref, dt_ref, adt_ref, adtT_ref, ubuf, *, tm, tpb, cchunk):
    i = pl.program_id(0)
    x = x_ref[...]
    h = _rms(x) * g_ref[...] * (1.0 + sc_ref[...]) + sh_ref[...]
    hb = h.astype(BF16)
    z_ref[...] = _dot(hb, wz_ref[...]).astype(BF16)
    dt = _softplus(_dot(hb, wdt_ref[...]) + dtb_ref[...])
    dt_ref[...] = dt
    adt_ref[...] = dt * a_ref[...]
    dtT = _softplus(_dot_nt(wdtT_ref[...], hb) + dtbT_ref[...])
    adtT_ref[...] = dtT * aT_ref[...]

    @pl.when(i % tpb == 0)
    def _():
        ubuf[...] = jnp.zeros(ubuf.shape, F32)

    for c0 in range(0, SSM_CONV_DIM, cchunk):
        cols = slice(c0, c0 + cchunk)
        u = _dot(hb, wx_ref[:, cols])
        ext = jnp.concatenate([ubuf[:, cols], u], axis=0)
        acc = cb_ref[:, cols] + cw_ref[SSM_CONV - 1:SSM_CONV, cols] * u
        for j in range(1, SSM_CONV):
            shifted = pltpu.roll(ext, j, axis=0)[8:, :]
            acc = acc + cw_ref[SSM_CONV - 1 - j:SSM_CONV - j, cols] * shifted
        xbc_ref[:, cols] = (acc * _sigmoid(acc)).astype(BF16)
        ubuf[:, cols] = u[tm - 8:, :]


def _ssd_pre(x2d, sh, sc, g, w_in, conv_w, conv_b, dt_bias, a_log, bsz, seqlen, tm):
    n, d = x2d.shape
    tpb = seqlen // tm
    nh = SSM_HEADS
    wz = w_in[:, :SSM_INNER].astype(BF16)
    wx = w_in[:, SSM_INNER:SSM_INNER + SSM_CONV_DIM].astype(BF16)
    wdt_raw = w_in[:, SSM_INNER + SSM_CONV_DIM:]
    wdt = jnp.concatenate([wdt_raw, jnp.zeros((d, LANES - nh), F32)], axis=1).astype(BF16)
    wdtT = wdt_raw.T.astype(BF16)
    pad = jnp.zeros((LANES - nh,), F32)
    dtb = jnp.concatenate([dt_bias, pad]).reshape(1, LANES)
    a_neg = -jnp.exp(a_log.astype(F32))
    a_row = jnp.concatenate([a_neg, pad]).reshape(1, LANES)
    vec = lambda i: (0, 0)
    bvec = lambda i: (i // tpb, 0, 0)
    return pl.pallas_call(
        functools.partial(_ssd_pre_kernel, tm=tm, tpb=tpb, cchunk=1024),
        out_shape=(
            jax.ShapeDtypeStruct((n, SSM_INNER), BF16),
            jax.ShapeDtypeStruct((n, SSM_CONV_DIM), BF16),
            jax.ShapeDtypeStruct((n, LANES), F32),
            jax.ShapeDtypeStruct((n, LANES), F32),
            jax.ShapeDtypeStruct((bsz, nh, seqlen), F32),
        ),
        grid=(n // tm,),
        in_specs=[
            pl.BlockSpec((tm, d), lambda i: (i, 0)),
            pl.BlockSpec((None, 1, d), bvec),
            pl.BlockSpec((None, 1, d), bvec),
            pl.BlockSpec((1, d), vec),
            pl.BlockSpec((d, SSM_INNER), vec),
            pl.BlockSpec((d, SSM_CONV_DIM), vec),
            pl.BlockSpec((d, LANES), vec),
            pl.BlockSpec((nh, d), vec),
            pl.BlockSpec((SSM_CONV, SSM_CONV_DIM), vec),
            pl.BlockSpec((1, SSM_CONV_DIM), vec),
            pl.BlockSpec((1, LANES), vec),
            pl.BlockSpec((nh, 1), vec),
            pl.BlockSpec((1, LANES), vec),
            pl.BlockSpec((nh, 1), vec),
        ],
        out_specs=(
            pl.BlockSpec((tm, SSM_INNER), lambda i: (i, 0)),
            pl.BlockSpec((tm, SSM_CONV_DIM), lambda i: (i, 0)),
            pl.BlockSpec((tm, LANES), lambda i: (i, 0)),
            pl.BlockSpec((tm, LANES), lambda i: (i, 0)),
            pl.BlockSpec((None, nh, tm), lambda i: (i // tpb, 0, i % tpb)),
        ),
        scratch_shapes=[pltpu.VMEM((8, SSM_CONV_DIM), F32)],
        compiler_params=_cparams("arbitrary"),
        name="ssd_pre",
    )(x2d, sh, sc, g.reshape(1, d), wz, wx, wdt, wdtT, conv_w, conv_b.reshape(1, -1), dtb,
      dt_bias.reshape(nh, 1), a_row, a_neg.reshape(nh, 1))


def _expand_heads(v, g, rows):
    lane = lax.broadcasted_iota(I32, (rows, LANES), 1)
    lo = lane < SSM_HEAD_DIM
    parts = []
    for j in range(0, SSM_HPG, 2):
        h0 = g * SSM_HPG + j
        parts.append(jnp.where(lo, v[:, h0:h0 + 1], v[:, h0 + 1:h0 + 2]))
    return jnp.concatenate(parts, axis=1)


def _ssd_scan_kernel(xbc_ref, z_ref, dt_ref, adt_ref, adtT_ref, d_ref, gn_ref, hx_ref, yn_ref, state, *, lc):
    c = pl.program_id(1)

    @pl.when(c == 0)
    def _():
        state[...] = jnp.zeros(state.shape, F32)

    row = lax.broadcasted_iota(I32, (lc, lc), 0)
    col = lax.broadcasted_iota(I32, (lc, lc), 1)
    causal = row >= col
    tri = jnp.where(causal, 1.0, 0.0).astype(BF16)
    triT = jnp.where(row <= col, 1.0, 0.0).astype(BF16)
    ah, am, al = _split3(adt_ref[...])
    a_cum = _dot(tri, ah) + (_dot(tri, am) + _dot(tri, al))
    th, tmid, tl = _split3(adtT_ref[...])
    a_cumT = _dot(th, triT) + (_dot(tmid, triT) + _dot(tl, triT))
    dt = dt_ref[...]
    a_last = a_cum[lc - 1:lc, :]
    e_cum = jnp.exp(a_cum)
    d2e = jnp.exp(a_last - a_cum)
    cdec = jnp.exp(a_last)
    dskip = d_ref[...]
    hx = hx_ref[...]
    dt_x = _dot(dt.astype(BF16), hx)
    dtd2e_x = _dot((dt * d2e).astype(BF16), hx)
    ecum_x = _dot(e_cum.astype(BF16), hx)
    lane2 = lax.broadcasted_iota(I32, (lc, SSM_GROUP_W), 1)
    for g in range(SSM_GROUPS):
        xg = xbc_ref[:, g * SSM_GROUP_W:(g + 1) * SSM_GROUP_W].astype(F32)
        b0 = SSM_INNER + g * SSM_STATE
        c0 = SSM_INNER + SSM_BC + g * SSM_STATE
        bg = xbc_ref[:, b0:b0 + SSM_STATE]
        cg = xbc_ref[:, c0:c0 + SSM_STATE]
        gcols = slice(g * SSM_GROUP_W, (g + 1) * SSM_GROUP_W)
        xdt_b = (xg * dt_x[:, gcols]).astype(BF16)
        cb = _dot_nt(cg, bg)
        y = jnp.zeros((lc, SSM_GROUP_W), F32)
        for j in range(SSM_HPG):
            hd = g * SSM_HPG + j
            seg = a_cum[:, hd:hd + 1] - a_cumT[hd:hd + 1, :]
            decay = jnp.exp(jnp.where(causal, seg, -jnp.inf))
            m = (cb * decay).astype(BF16)
            in_head = (lane2 >= j * SSM_HEAD_DIM) & (lane2 < (j + 1) * SSM_HEAD_DIM)
            y = y + _dot(m, jnp.where(in_head, xdt_b, jnp.zeros_like(xdt_b)))
        st = state[g]
        y = y + _dot(cg, st.astype(BF16)) * ecum_x[:, gcols]
        xd2e = (xg * dtd2e_x[:, gcols]).astype(BF16)
        state[g] = st * _expand_heads(cdec, g, 1) + _dot_tn(bg, xd2e)
        y = y + _expand_heads(dskip, g, 1) * xg
        zg = z_ref[:, g * SSM_GROUP_W:(g + 1) * SSM_GROUP_W].astype(F32)
        yz = y * (zg * _sigmoid(zg))
        yn = _rms(yz) * gn_ref[:, g * SSM_GROUP_W:(g + 1) * SSM_GROUP_W]
        yn_ref[:, g * SSM_GROUP_W:(g + 1) * SSM_GROUP_W] = yn.astype(BF16)


def _ssd_scan(z, xbc, dt, adt, adtT, d_skip, g_norm, bsz, seqlen, lc):
    n = z.shape[0]
    nc = seqlen // lc
    pad = jnp.zeros((LANES - SSM_HEADS,), F32)
    d_row = jnp.concatenate([d_skip, pad]).reshape(1, LANES)
    rows = lambda b, c: (b * nc + c, 0)
    vec = lambda b, c: (0, 0)
    head_of_col = jnp.arange(SSM_INNER, dtype=I32) // SSM_HEAD_DIM
    head_expand = (jnp.arange(LANES, dtype=I32)[:, None] == head_of_col[None, :]).astype(BF16)
    return pl.pallas_call(
        functools.partial(_ssd_scan_kernel, lc=lc),
        out_shape=jax.ShapeDtypeStruct((n, SSM_INNER), BF16),
        grid=(bsz, nc),
        in_specs=[
            pl.BlockSpec((lc, SSM_CONV_DIM), rows),
            pl.BlockSpec((lc, SSM_INNER), rows),
            pl.BlockSpec((lc, LANES), rows),
            pl.BlockSpec((lc, LANES), rows),
            pl.BlockSpec((None, SSM_HEADS, lc), lambda b, c: (b, 0, c)),
            pl.BlockSpec((1, LANES), vec),
            pl.BlockSpec((1, SSM_INNER), vec),
            pl.BlockSpec((LANES, SSM_INNER), vec),
        ],
        out_specs=pl.BlockSpec((lc, SSM_INNER), rows),
        scratch_shapes=[pltpu.VMEM((SSM_GROUPS, SSM_STATE, SSM_GROUP_W), F32)],
        compiler_params=_cparams("arbitrary", "arbitrary"),
        name="ssd_scan",
    )(xbc, z, dt, adt, adtT, d_row, g_norm.reshape(1, -1), head_expand)


def kernel(x, c, positions, w_mod, b_mod, g_mix_norm, g_ffn_norm, mla_w_in, mla_g_q, mla_g_kv, mla_w_q_up, mla_w_kv_up, mla_w_out, ssm_w_in, ssm_conv_w, ssm_conv_b, ssm_dt_bias, ssm_a_log, ssm_d, ssm_g_norm, ssm_w_out, moe_w_router, moe_b_router, moe_w_gate_up, moe_b_gate_up, moe_w_down, moe_b_down, g_final):
    bsz, seqlen, d = x.shape
    depth = w_mod.shape[0]
    n = bsz * seqlen
    tm = min(512, seqlen)
    mod = _modulation(c, w_mod, b_mod)
    mod = mod.reshape(depth, 6, bsz, 1, d)
    pos_f = positions.astype(F32).reshape(n, 1)
    xc = x.reshape(n, d)
    for i in range(depth):
        sh1, sc1, gt1, sh2, sc2, gt2 = [mod[i, j] for j in range(6)]
        j = i // 2
        if i % 2 == 0:
            q, k, v = _mla_pre(xc, pos_f, sh1, sc1, g_mix_norm[i], mla_w_in[j], mla_g_q[j], mla_g_kv[j],
                               mla_w_q_up[j], mla_w_kv_up[j], bsz, seqlen, tm)
            o = _attention(q, k, v, tm).reshape(n, MLA_HEADS * V_HEAD)
            w_out = mla_w_out[j]
        else:
            tms = min(256, seqlen)
            z, xbc, dt, adt, adtT = _ssd_pre(xc, sh1, sc1, g_mix_norm[i], ssm_w_in[j], ssm_conv_w[j],
                                             ssm_conv_b[j], ssm_dt_bias[j], ssm_a_log[j], bsz, seqlen, tms)
            o = _ssd_scan(z, xbc, dt, adt, adtT, ssm_d[j], ssm_g_norm[j], bsz, seqlen, min(256, seqlen))
            w_out = ssm_w_out[j]
        x1, h2, meta, gates, counts = _post_mixer(xc, o, w_out, gt1, sh2, sc2, g_ffn_norm[i],
                                                  moe_w_router[i], moe_b_router[i], seqlen, tm)
        xc = _moe(x1, h2, meta, gates, counts, gt2, g_final, moe_w_gate_up, moe_b_gate_up,
                  moe_w_down, moe_b_down, i, seqlen, final=(i == depth - 1))
    return xc.reshape(bsz, seqlen, d)
```

```python
import functools
import math

import jax
import jax.numpy as jnp
from jax import lax
from jax.experimental import pallas as pl
from jax.experimental.pallas import tpu as pltpu

F32 = jnp.float32
BF16 = jnp.bfloat16
I32 = jnp.int32
U32 = jnp.uint32

NORM_EPS = 1e-6
MLA_HEADS = 8
Q_LORA = 256
KV_LORA = 256
QK_NOPE = 128
QK_ROPE = 64
V_HEAD = 128
QK_HEAD = QK_NOPE + QK_ROPE
ROPE_THETA = 10000.0
MLA_SCALE = QK_HEAD ** -0.5
SSM_HEAD_DIM = 64
SSM_GROUPS = 8
SSM_HPG = 4
SSM_HEADS = SSM_GROUPS * SSM_HPG
SSM_STATE = 128
SSM_CONV = 4
SSM_GROUP_W = SSM_HPG * SSM_HEAD_DIM
SSM_INNER = SSM_GROUPS * SSM_GROUP_W
SSM_BC = SSM_GROUPS * SSM_STATE
SSM_CONV_DIM = SSM_INNER + 2 * SSM_BC
N_EXPERTS = 32
TOP_K = 4
SWIGLU_LIMIT = 7.0
SWIGLU_ALPHA = 1.702
MOE_BLOCK = 256

LANES = 128
VMEM_LIMIT = 56 * 1024 * 1024


def _cparams(*sem):
    return pltpu.CompilerParams(dimension_semantics=tuple(sem), vmem_limit_bytes=VMEM_LIMIT)


def _dot(a, b):
    return jnp.dot(a, b, preferred_element_type=F32)


def _dot_nt(a, b):
    return lax.dot_general(a, b, (((1,), (1,)), ((), ())), preferred_element_type=F32)


def _dot_tn(a, b):
    return lax.dot_general(a, b, (((0,), (0,)), ((), ())), preferred_element_type=F32)


def _split3(a):
    hi = a.astype(BF16)
    r1 = a - hi.astype(F32)
    mid = r1.astype(BF16)
    lo = (r1 - mid.astype(F32)).astype(BF16)
    return hi, mid, lo


def _dot_f32ish(a, b):
    ah, am, _ = _split3(a)
    bh, bm, _ = _split3(b)
    return _dot(ah, bh) + (_dot(ah, bm) + _dot(am, bh))


def _pack_bf16_pair(lo, hi):
    lo_b = lax.bitcast_convert_type(lo.astype(BF16).astype(F32), U32)
    hi_b = lax.bitcast_convert_type(hi.astype(BF16).astype(F32), U32)
    return hi_b | lax.shift_right_logical(lo_b, jnp.uint32(16))


def _unpack_bf16_pair(w):
    lo = lax.bitcast_convert_type(lax.shift_left(w, jnp.uint32(16)), F32)
    hi = lax.bitcast_convert_type(w & jnp.uint32(0xFFFF0000), F32)
    return lo, hi


ROW_CHUNKS = 4


def _chunk_copies(buf2d, rows_hbm, r0, nrows, sem, *, to_hbm):
    cps = []
    for c in range(ROW_CHUNKS):
        v = buf2d.at[:, pl.ds(c * LANES, LANES)]
        h = rows_hbm.at[pl.ds(r0, nrows), c]
        cps.append(pltpu.make_async_copy(v, h, sem) if to_hbm else pltpu.make_async_copy(h, v, sem))
    return cps


def _sigmoid(x):
    return 1.0 / (1.0 + jnp.exp(-x))


def _rms(x):
    return x * lax.rsqrt(jnp.mean(x * x, axis=-1, keepdims=True) + NORM_EPS)


def _mod_kernel(c_ref, w_ref, b_ref, o_ref):
    c = c_ref[...]
    cond = c * _sigmoid(c)
    o_ref[...] = _dot_f32ish(cond, w_ref[...]) + b_ref[...]


def _modulation(c, w_mod, b_mod):
    depth, d, d6 = w_mod.shape
    bsz = c.shape[0]
    nj = d6 // d
    return pl.pallas_call(
        _mod_kernel,
        out_shape=jax.ShapeDtypeStruct((depth, nj, bsz, d), F32),
        grid=(depth, nj),
        in_specs=[
            pl.BlockSpec((bsz, d), lambda l, j: (0, 0)),
            pl.BlockSpec((None, d, d), lambda l, j: (l, 0, j)),
            pl.BlockSpec((None, 1, d), lambda l, j: (l, 0, j)),
        ],
        out_specs=pl.BlockSpec((None, None, bsz, d), lambda l, j: (l, j, 0, 0)),
        compiler_params=_cparams("arbitrary", "arbitrary"),
        name="mod",
    )(c, w_mod, b_mod.reshape(depth, 1, d6))


def _mla_pre_kernel(x_ref, pos_ref, sh_ref, sc_ref, g_ref, win_ref, gq_ref, gkv_ref, wq_ref, wkv_ref,
                    invf_ref, q_ref, k_ref, v_ref):
    x = x_ref[...]
    h = _rms(x) * g_ref[...] * (1.0 + sc_ref[...]) + sh_ref[...]
    lat = _dot(h.astype(BF16), win_ref[...])
    q_lat = _rms(lat[:, :Q_LORA]) * gq_ref[...]
    kv_lat = _rms(lat[:, Q_LORA:Q_LORA + KV_LORA]) * gkv_ref[...]
    ang = pos_ref[...] * invf_ref[...]
    cs = jnp.cos(ang)
    sn = jnp.sin(ang)
    o = Q_LORA + KV_LORA
    k_rope = lat[:, o:o + LANES] * cs + lat[:, o + LANES:o + 2 * LANES] * sn
    qq = _dot(q_lat.astype(BF16), wq_ref[...])
    kv = _dot(kv_lat.astype(BF16), wkv_ref[...])
    rot0 = MLA_HEADS * 2 * LANES
    k_rope_b = k_rope[:, :QK_ROPE].astype(BF16)
    for hd in range(MLA_HEADS):
        c0 = hd * 2 * LANES
        q_nope = qq[:, c0:c0 + LANES] * MLA_SCALE
        q_rope = (qq[:, c0 + LANES:c0 + 2 * LANES] * cs
                  + qq[:, rot0 + hd * LANES:rot0 + (hd + 1) * LANES] * sn) * MLA_SCALE
        q_ref[hd, :, 0:QK_NOPE] = q_nope.astype(BF16)
        q_ref[hd, :, QK_NOPE:QK_HEAD] = q_rope[:, :QK_ROPE].astype(BF16)
        k_ref[hd, :, 0:QK_NOPE] = kv[:, c0:c0 + LANES].astype(BF16)
        k_ref[hd, :, QK_NOPE:QK_HEAD] = k_rope_b
        v_ref[hd] = kv[:, c0 + LANES:c0 + 2 * LANES].astype(BF16)


def _rot_half_cols(w):
    half = QK_ROPE // 2
    return jnp.concatenate([-w[..., half:], w[..., :half]], axis=-1)


def _mla_pre(x2d, pos_f, sh, sc, g, w_in, g_q, g_kv, w_q_up, w_kv_up, bsz, seqlen, tm):
    n, d = x2d.shape
    hh = MLA_HEADS
    o = Q_LORA + KV_LORA
    wr = w_in[:, o:o + QK_ROPE]
    zpad = jnp.zeros((d, LANES - QK_ROPE), F32)
    w_in_ext = jnp.concatenate([w_in[:, :o], wr, zpad, _rot_half_cols(wr), zpad], axis=1).astype(BF16)
    wq = w_q_up.reshape(Q_LORA, hh, QK_HEAD)
    zq = jnp.zeros((Q_LORA, hh, LANES - QK_ROPE), F32)
    wq_main = jnp.concatenate([wq, zq], axis=-1).reshape(Q_LORA, hh * 2 * LANES)
    wq_rot = jnp.concatenate([_rot_half_cols(wq[..., QK_NOPE:]), zq], axis=-1).reshape(Q_LORA, hh * LANES)
    wq_ext = jnp.concatenate([wq_main, wq_rot], axis=1).astype(BF16)
    inv_freq = 1.0 / (ROPE_THETA ** (jnp.arange(0, QK_ROPE, 2, dtype=F32) / QK_ROPE))
    invf = jnp.concatenate([inv_freq, inv_freq, jnp.zeros((LANES - QK_ROPE,), F32)]).reshape(1, LANES)
    tpb = seqlen // tm
    vec = lambda i: (0, 0)
    outs = pl.pallas_call(
        _mla_pre_kernel,
        out_shape=(
            jax.ShapeDtypeStruct((bsz, hh, seqlen, QK_HEAD), BF16),
            jax.ShapeDtypeStruct((bsz, hh, seqlen, QK_HEAD), BF16),
            jax.ShapeDtypeStruct((bsz, hh, seqlen, V_HEAD), BF16),
        ),
        grid=(n // tm,),
        in_specs=[
            pl.BlockSpec((tm, d), lambda i: (i, 0)),
            pl.BlockSpec((tm, 1), lambda i: (i, 0)),
            pl.BlockSpec((None, 1, d), lambda i: (i // tpb, 0, 0)),
            pl.BlockSpec((None, 1, d), lambda i: (i // tpb, 0, 0)),
            pl.BlockSpec((1, d), vec),
            pl.BlockSpec(w_in_ext.shape, vec),
            pl.BlockSpec((1, Q_LORA), vec),
            pl.BlockSpec((1, KV_LORA), vec),
            pl.BlockSpec(wq_ext.shape, vec),
            pl.BlockSpec((KV_LORA, hh * 2 * LANES), vec),
            pl.BlockSpec((1, LANES), vec),
        ],
        out_specs=(
            pl.BlockSpec((None, hh, tm, QK_HEAD), lambda i: (i // tpb, 0, i % tpb, 0)),
            pl.BlockSpec((None, hh, tm, QK_HEAD), lambda i: (i // tpb, 0, i % tpb, 0)),
            pl.BlockSpec((None, hh, tm, V_HEAD), lambda i: (i // tpb, 0, i % tpb, 0)),
        ),
        compiler_params=_cparams("arbitrary"),
        name="mla_pre",
    )(x2d, pos_f, sh, sc, g.reshape(1, d), w_in_ext, g_q.reshape(1, -1), g_kv.reshape(1, -1), wq_ext,
      w_kv_up.astype(BF16), invf)
    return outs


ATTN_HEADS_PER_STEP = 4


def _attn_kernel(q_ref, k_ref, v_ref, o_ref, m_scr, acc_scr, *, tq, hp):
    qi = pl.program_id(2)
    m_scr[...] = jnp.full(m_scr.shape, -jnp.inf, F32)
    acc_scr[...] = jnp.zeros(acc_scr.shape, F32)

    def block(hd, r0, tk, masked):
        k = k_ref[hd, pl.ds(r0, tk), :]
        v_ext = jnp.concatenate([v_ref[hd, pl.ds(r0, tk), :], jnp.ones((tk, V_HEAD), BF16)], axis=1)
        s = _dot_nt(q_ref[hd], k)
        if masked:
            row = lax.broadcasted_iota(I32, (tq, tk), 0)
            col = lax.broadcasted_iota(I32, (tq, tk), 1)
            s = jnp.where(row >= col, s, -jnp.inf)
        m_prev = m_scr[hd]
        m_new = jnp.maximum(m_prev, jnp.max(s, axis=-1, keepdims=True))
        alpha = jnp.exp(m_prev - m_new)
        p = jnp.exp(s - m_new).astype(BF16)
        acc_scr[hd] = alpha * acc_scr[hd] + _dot(p, v_ext)
        m_scr[hd] = m_new

    def body(j, carry):
        for hd in range(hp):
            block(hd, pl.multiple_of(j * (2 * tq), 2 * tq), 2 * tq, False)
        return carry

    lax.fori_loop(0, qi // 2, body, 0)

    @pl.when(qi % 2 == 1)
    def _():
        for hd in range(hp):
            block(hd, pl.multiple_of((qi - 1) * tq, tq), tq, False)

    for hd in range(hp):
        block(hd, pl.multiple_of(qi * tq, tq), tq, True)
    for hd in range(hp):
        acc = acc_scr[hd]
        o_ref[:, hd * V_HEAD:(hd + 1) * V_HEAD] = (acc[:, :V_HEAD] / acc[:, V_HEAD:]).astype(o_ref.dtype)


def _attention(q, k, v, tq):
    bsz, hh, seqlen, _ = q.shape
    hp = ATTN_HEADS_PER_STEP
    return pl.pallas_call(
        functools.partial(_attn_kernel, tq=tq, hp=hp),
        out_shape=jax.ShapeDtypeStruct((bsz, seqlen, hh * V_HEAD), BF16),
        grid=(bsz, hh // hp, seqlen // tq),
        in_specs=[
            pl.BlockSpec((None, hp, tq, QK_HEAD), lambda b, h, i: (b, h, i, 0)),
            pl.BlockSpec((None, hp, seqlen, QK_HEAD), lambda b, h, i: (b, h, 0, 0)),
            pl.BlockSpec((None, hp, seqlen, V_HEAD), lambda b, h, i: (b, h, 0, 0)),
        ],
        out_specs=pl.BlockSpec((None, tq, hp * V_HEAD), lambda b, h, i: (b, i, h)),
        scratch_shapes=[
            pltpu.VMEM((hp, tq, 1), F32),
            pltpu.VMEM((hp, tq, 2 * V_HEAD), F32),
        ],
        compiler_params=_cparams("arbitrary", "arbitrary", "arbitrary"),
        name="attn",
    )(q, k, v)


def _post_mixer_kernel(x_ref, o_ref, wout_ref, gt1_ref, sh_ref, sc_ref, g_ref, wr_ref, br_ref,
                       x1_ref, h2_hbm, meta_ref, gate_ref, cnt_ref, carry_scr, pbuf, psem, *, tm):
    i = pl.program_id(0)

    @pl.when(i == 0)
    def _():
        carry_scr[...] = jnp.zeros(carry_scr.shape, F32)

    y = _dot(o_ref[...], wout_ref[...])
    x1 = x_ref[...] + gt1_ref[...] * y
    x1_ref[...] = x1
    h2 = _rms(x1) * g_ref[...] * (1.0 + sc_ref[...]) + sh_ref[...]
    half = h2.shape[1] // 2

    @pl.when(i > 0)
    def _():
        for cp in _chunk_copies(pbuf, h2_hbm, (i - 1) * tm, tm, psem, to_hbm=True):
            cp.wait()

    pbuf[...] = _pack_bf16_pair(h2[:, :half], h2[:, half:])
    for cp in _chunk_copies(pbuf, h2_hbm, i * tm, tm, psem, to_hbm=True):
        cp.start()

    logits = _dot_f32ish(h2, wr_ref[...]) + br_ref[...]
    lane = lax.broadcasted_iota(I32, (tm, LANES), 1).astype(F32)
    work = logits
    idxs, vals = [], []
    for _ in range(TOP_K):
        mx = jnp.max(work, axis=-1, keepdims=True)
        idx = jnp.min(jnp.where(work == mx, lane, float(LANES)), axis=-1, keepdims=True)
        idxs.append(idx)
        vals.append(mx)
        work = jnp.where(lane == idx, -jnp.inf, work)
    exps = [jnp.exp(vk - vals[0]) for vk in vals]
    denom = exps[0] + exps[1] + exps[2] + exps[3]
    onehot = jnp.zeros((tm, LANES), F32)
    for idx in idxs:
        onehot = onehot + jnp.where(lane == idx, 1.0, 0.0)
    row = lax.broadcasted_iota(I32, (tm, tm), 0)
    col = lax.broadcasted_iota(I32, (tm, tm), 1)
    ltri = jnp.where(row > col, 1.0, 0.0).astype(BF16)
    cum = _dot(ltri, onehot.astype(BF16)) + carry_scr[...]
    meta = jnp.zeros((tm, LANES), F32)
    gates = jnp.zeros((tm, LANES), F32)
    for kk in range(TOP_K):
        rank = jnp.sum(jnp.where(lane == idxs[kk], cum, 0.0), axis=-1, keepdims=True)
        meta = jnp.where(lane == float(kk), idxs[kk], meta)
        meta = jnp.where(lane == float(TOP_K + kk), rank, meta)
        gates = jnp.where(lane == float(kk), exps[kk] / denom, gates)
    meta_ref[...] = jnp.transpose(meta)[:2 * TOP_K, :].astype(I32)
    gate_ref[...] = gates
    carry = carry_scr[...] + jnp.sum(onehot, axis=0, keepdims=True)
    carry_scr[...] = carry
    cnt_ref[...] = carry.astype(I32)

    @pl.when(i == pl.num_programs(0) - 1)
    def _():
        for cp in _chunk_copies(pbuf, h2_hbm, i * tm, tm, psem, to_hbm=True):
            cp.wait()


def _post_mixer(x2d, o2d, w_out, gt1, sh2, sc2, g_ffn, w_router, b_router, seqlen, tm):
    n, d = x2d.shape
    kdim = o2d.shape[1]
    tpb = seqlen // tm
    wr = jnp.concatenate([w_router, jnp.zeros((d, LANES - N_EXPERTS), F32)], axis=1)
    br = jnp.concatenate([b_router, jnp.full((LANES - N_EXPERTS,), -1e30, F32)]).reshape(1, LANES)
    vec = lambda i: (0, 0)
    bvec = lambda i: (i // tpb, 0, 0)
    return pl.pallas_call(
        functools.partial(_post_mixer_kernel, tm=tm),
        out_shape=(
            jax.ShapeDtypeStruct((n, d), F32),
            jax.ShapeDtypeStruct((n, ROW_CHUNKS, LANES), U32),
            jax.ShapeDtypeStruct((2 * TOP_K, n), I32),
            jax.ShapeDtypeStruct((n, LANES), F32),
            jax.ShapeDtypeStruct((1, LANES), I32),
        ),
        grid=(n // tm,),
        in_specs=[
            pl.BlockSpec((tm, d), lambda i: (i, 0)),
            pl.BlockSpec((tm, kdim), lambda i: (i, 0)),
            pl.BlockSpec((kdim, d), vec),
            pl.BlockSpec((None, 1, d), bvec),
            pl.BlockSpec((None, 1, d), bvec),
            pl.BlockSpec((None, 1, d), bvec),
            pl.BlockSpec((1, d), vec),
            pl.BlockSpec((d, LANES), vec),
            pl.BlockSpec((1, LANES), vec),
        ],
        out_specs=(
            pl.BlockSpec((tm, d), lambda i: (i, 0)),
            pl.BlockSpec(memory_space=pl.ANY),
            pl.BlockSpec((2 * TOP_K, tm), lambda i: (0, i)),
            pl.BlockSpec((tm, LANES), lambda i: (i, 0)),
            pl.BlockSpec((1, LANES), vec),
        ),
        scratch_shapes=[pltpu.VMEM((1, LANES), F32), pltpu.VMEM((tm, d // 2), U32), pltpu.SemaphoreType.DMA],
        compiler_params=_cparams("arbitrary"),
        name="post_mixer",
    )(x2d, o2d, w_out.astype(BF16), gt1, sh2, sc2, g_ffn.reshape(1, d), wr, br)


ROW_UNROLL = 8


def _row_copy(src, s, dst, t, sem):
    return pltpu.make_async_copy(src.at[s], dst.at[t], sem)


def _dest_kernel(pstart_ref, meta_ref, dest_ref):
    e = meta_ref[0:TOP_K, :]
    dest = meta_ref[TOP_K:2 * TOP_K, :]
    for j in range(N_EXPERTS):
        dest = dest + jnp.where(e == j, pstart_ref[j], 0)
    dest_ref[...] = dest


def _dest_rows(meta_t, pstart):
    n = meta_t.shape[1]
    tn = min(4096, n)
    grid_spec = pltpu.PrefetchScalarGridSpec(
        num_scalar_prefetch=1,
        grid=(n // tn,),
        in_specs=[pl.BlockSpec((2 * TOP_K, tn), lambda i, ps: (0, i))],
        out_specs=pl.BlockSpec((TOP_K, tn), lambda i, ps: (0, i)),
    )
    return pl.pallas_call(
        _dest_kernel,
        out_shape=jax.ShapeDtypeStruct((TOP_K, n), I32),
        grid_spec=grid_spec,
        compiler_params=_cparams("arbitrary"),
        name="dest_rows",
    )(pstart, meta_t)


def _dispatch_kernel(pstart_ref, cnt_ref, dest_ref, h_ref, xs_hbm, zbuf, sem, zsem, *, tm):
    i = pl.program_id(0)

    @pl.when(i == 0)
    def _():
        zbuf[...] = jnp.zeros(zbuf.shape, U32)

        def per_expert(e, carry):
            lo = pstart_ref[e] + cnt_ref[e]
            hi = pstart_ref[e + 1]

            def start(j, c):
                _row_copy(zbuf, 0, xs_hbm, j, zsem).start()
                return c

            def wait(j, c):
                _row_copy(zbuf, 0, xs_hbm, j, zsem).wait()
                return c

            lax.fori_loop(lo, hi, start, 0)
            lax.fori_loop(lo, hi, wait, 0)
            return carry

        lax.fori_loop(0, N_EXPERTS, per_expert, 0)

    def start(tt, c):
        for u in range(ROW_UNROLL):
            t = tt * ROW_UNROLL + u
            for kk in range(TOP_K):
                _row_copy(h_ref, t, xs_hbm, dest_ref[kk, t], sem).start(priority=kk % 2)
        return c

    def wait(tt, c):
        for u in range(ROW_UNROLL * TOP_K):
            _row_copy(h_ref, 0, xs_hbm, 0, sem).wait()
        return c

    lax.fori_loop(0, tm // ROW_UNROLL, start, 0)
    lax.fori_loop(0, tm // ROW_UNROLL, wait, 0)


def _dispatch(h2p, dest_t, pstart, counts, n_rows, tm):
    n = h2p.shape[0]
    grid_spec = pltpu.PrefetchScalarGridSpec(
        num_scalar_prefetch=2,
        grid=(n // tm,),
        in_specs=[
            pl.BlockSpec((TOP_K, tm), lambda i, ps, cn: (0, i), memory_space=pltpu.SMEM),
            pl.BlockSpec((tm, ROW_CHUNKS, LANES), lambda i, ps, cn: (i, 0, 0)),
        ],
        out_specs=pl.BlockSpec(memory_space=pl.ANY),
        scratch_shapes=[
            pltpu.VMEM((1, ROW_CHUNKS, LANES), U32),
            pltpu.SemaphoreType.DMA,
            pltpu.SemaphoreType.DMA,
        ],
    )
    return pl.pallas_call(
        functools.partial(_dispatch_kernel, tm=tm),
        out_shape=jax.ShapeDtypeStruct((n_rows, ROW_CHUNKS, LANES), U32),
        grid_spec=grid_spec,
        compiler_params=pltpu.CompilerParams(dimension_semantics=("arbitrary",), vmem_limit_bytes=VMEM_LIMIT,
                                             has_side_effects=True),
        name="dispatch",
    )(pstart, counts, dest_t, h2p)


def _ffn_kernel(be_ref, nu_ref, xs_hbm, wgu_ref, bgu_ref, wd_ref, bd_ref, ys_hbm, wgu_b, wd_b, xbuf, ybuf,
                xsem, ysem, *, ff):
    i = pl.program_id(0)
    nu = nu_ref[0]
    slot = i % 2
    prev = be_ref[jnp.maximum(i - 1, 0)]
    changed = jnp.logical_or(i == 0, be_ref[i] != prev)

    def x_copies(step, s):
        return _chunk_copies(xbuf.at[s], xs_hbm, step * MOE_BLOCK, MOE_BLOCK, xsem.at[s], to_hbm=False)

    def y_copies(step, s):
        return _chunk_copies(ybuf.at[s], ys_hbm, step * MOE_BLOCK, MOE_BLOCK, ysem.at[s], to_hbm=True)

    @pl.when(i == 0)
    def _():
        for cp in x_copies(0, 0):
            cp.start()

    @pl.when(i + 1 < nu)
    def _():
        for cp in x_copies(i + 1, 1 - slot):
            cp.start()

    @pl.when(changed)
    def _():
        wgu_b[...] = wgu_ref[...].astype(BF16)
        wd_b[...] = wd_ref[...].astype(BF16)

    @pl.when(i < nu)
    def _():
        for cp in x_copies(i, slot):
            cp.wait()
        x_lo, x_hi = _unpack_bf16_pair(xbuf[slot])
        dh = x_lo.shape[1]
        gu = (_dot(x_lo.astype(BF16), wgu_b[0:dh, :]) + _dot(x_hi.astype(BF16), wgu_b[dh:2 * dh, :])
              + bgu_ref[...])
        g = jnp.minimum(gu[:, :ff], SWIGLU_LIMIT)
        lin = jnp.clip(gu[:, ff:], -SWIGLU_LIMIT, SWIGLU_LIMIT)
        act = g * _sigmoid(SWIGLU_ALPHA * g) * (lin + 1.0)
        y = _dot(act.astype(BF16), wd_b[...]) + bd_ref[...]

        @pl.when(i >= 2)
        def _():
            for cp in y_copies(i - 2, slot):
                cp.wait()

        ybuf[slot] = _pack_bf16_pair(y[:, :dh], y[:, dh:])
        for cp in y_copies(i, slot):
            cp.start()

    @pl.when(i == nu - 1)
    def _():
        for cp in y_copies(i, slot):
            cp.wait()

        @pl.when(i >= 1)
        def _():
            for cp in y_copies(i - 1, 1 - slot):
                cp.wait()


def _ffn(xs, blk_e, n_used, w_gate_up, b_gate_up, w_down, b_down, layer):
    n_rows = xs.shape[0]
    depth, ne, d, ff2 = w_gate_up.shape
    ff = ff2 // 2
    n_blk = n_rows // MOE_BLOCK
    row_buf = pltpu.VMEM((2, MOE_BLOCK, ROW_CHUNKS * LANES), U32)
    grid_spec = pltpu.PrefetchScalarGridSpec(
        num_scalar_prefetch=2,
        grid=(n_blk,),
        in_specs=[
            pl.BlockSpec(memory_space=pl.ANY),
            pl.BlockSpec((None, None, d, ff2), lambda i, be, nu: (layer, be[i], 0, 0)),
            pl.BlockSpec((None, None, 1, ff2), lambda i, be, nu: (layer, be[i], 0, 0)),
            pl.BlockSpec((None, None, ff, d), lambda i, be, nu: (layer, be[i], 0, 0)),
            pl.BlockSpec((None, None, 1, d), lambda i, be, nu: (layer, be[i], 0, 0)),
        ],
        out_specs=pl.BlockSpec(memory_space=pl.ANY),
        scratch_shapes=[pltpu.VMEM((d, ff2), BF16), pltpu.VMEM((ff, d), BF16), row_buf, row_buf,
                        pltpu.SemaphoreType.DMA((2,)), pltpu.SemaphoreType.DMA((2,))],
    )
    return pl.pallas_call(
        functools.partial(_ffn_kernel, ff=ff),
        out_shape=jax.ShapeDtypeStruct((n_rows, ROW_CHUNKS, LANES), U32),
        grid_spec=grid_spec,
        compiler_params=_cparams("arbitrary"),
        name="ffn",
    )(blk_e, n_used, xs, w_gate_up, b_gate_up.reshape(depth, ne, 1, ff2), w_down,
      b_down.reshape(depth, ne, 1, d))


def _combine_kernel(dest_ref, x1_ref, gate_ref, gt2_ref, gfin_ref, ys_hbm, o_ref, buf, sem, *, tm, final):
    def start(tt, c):
        for u in range(ROW_UNROLL):
            for kk in range(TOP_K):
                src = ys_hbm.at[dest_ref[kk, tt * ROW_UNROLL + u]]
                pltpu.make_async_copy(src, buf.at[kk, tt, :, u], sem).start(priority=kk % 2)
        return c

    def wait(tt, c):
        for u in range(ROW_UNROLL * TOP_K):
            pltpu.make_async_copy(ys_hbm.at[0], buf.at[0, 0, :, 0], sem).wait()
        return c

    lax.fori_loop(0, tm // ROW_UNROLL, start, 0)
    lax.fori_loop(0, tm // ROW_UNROLL, wait, 0)
    gates = gate_ref[...]
    acc_lo = acc_hi = None
    for kk in range(TOP_K):
        rows = jnp.concatenate([buf[kk, :, c, :, :].reshape(tm, LANES) for c in range(ROW_CHUNKS)], axis=1)
        lo, hi = _unpack_bf16_pair(rows)
        gk = gates[:, kk:kk + 1]
        acc_lo = gk * lo if acc_lo is None else acc_lo + gk * lo
        acc_hi = gk * hi if acc_hi is None else acc_hi + gk * hi
    x2 = x1_ref[...] + gt2_ref[...] * jnp.concatenate([acc_lo, acc_hi], axis=1)
    if final:
        x2 = _rms(x2) * gfin_ref[...]
    o_ref[...] = x2


def _combine(x1, gates, gt2, g_final, ys, dest_t, seqlen, tm, final):
    n, d = x1.shape
    tpb = seqlen // tm
    grid_spec = pltpu.PrefetchScalarGridSpec(
        num_scalar_prefetch=0,
        grid=(n // tm,),
        in_specs=[
            pl.BlockSpec((TOP_K, tm), lambda i: (0, i), memory_space=pltpu.SMEM),
            pl.BlockSpec((tm, d), lambda i: (i, 0)),
            pl.BlockSpec((tm, LANES), lambda i: (i, 0)),
            pl.BlockSpec((None, 1, d), lambda i: (i // tpb, 0, 0)),
            pl.BlockSpec((1, d), lambda i: (0, 0)),
            pl.BlockSpec(memory_space=pl.ANY),
        ],
        out_specs=pl.BlockSpec((tm, d), lambda i: (i, 0)),
        scratch_shapes=[pltpu.VMEM((TOP_K, tm // ROW_UNROLL, ROW_CHUNKS, ROW_UNROLL, LANES), U32),
                        pltpu.SemaphoreType.DMA],
    )
    return pl.pallas_call(
        functools.partial(_combine_kernel, tm=tm, final=final),
        out_shape=jax.ShapeDtypeStruct((n, d), F32),
        grid_spec=grid_spec,
        compiler_params=_cparams("arbitrary"),
        name="combine",
    )(dest_t, x1, gates, gt2, g_final.reshape(1, d), ys)


def _moe(x1, h2p, meta_t, gates, counts, gt2, g_final, w_gate_up, b_gate_up, w_down, b_down, layer, seqlen,
         final):
    n, d = x1.shape
    n_pair = n * TOP_K
    n_rows = -(-n_pair // MOE_BLOCK) * MOE_BLOCK + N_EXPERTS * MOE_BLOCK
    n_blk = n_rows // MOE_BLOCK
    cnt = counts[0, :N_EXPERTS]
    padded = (cnt + MOE_BLOCK - 1) // MOE_BLOCK * MOE_BLOCK
    pad_end = jnp.cumsum(padded)
    pstart = jnp.concatenate([pad_end - padded, pad_end[-1:]]).astype(I32)
    n_used = (pad_end[-1:] // MOE_BLOCK).astype(I32)
    blk_start = jnp.arange(n_blk, dtype=I32) * MOE_BLOCK
    blk_e = jnp.minimum(jnp.sum(blk_start[:, None] >= pad_end[None, :], axis=1), N_EXPERTS - 1).astype(I32)
    tm = min(256, seqlen)
    dest_t = _dest_rows(meta_t, pstart)
    xs = _dispatch(h2p, dest_t, pstart, cnt, n_rows, tm)
    ys = _ffn(xs, blk_e, n_used, w_gate_up, b_gate_up, w_down, b_down, layer)
    return _combine(x1, gates, gt2, g_final, ys, dest_t, seqlen, tm, final)


def _softplus(x):
    return jnp.maximum(x, 0.0) + jnp.log(1.0 + jnp.exp(-jnp.abs(x)))


def _ssd_pre_kernel(x_ref, sh_ref, sc_ref, g_ref, wz_ref, wx_ref, wdt_ref, wdtT_ref, cw_ref, cb_ref,
                    dtb_ref, dtbT_ref, a_ref, aT_ref,
                    z_ref, xbc_ref, dt_ref, adt_ref, adtT_ref, ubuf, *, tm, tpb, cchunk):
    i = pl.program_id(0)
    x = x_ref[...]
    h = _rms(x) * g_ref[...] * (1.0 + sc_ref[...]) + sh_ref[...]
    hb = h.astype(BF16)
    z_ref[...] = _dot(hb, wz_ref[...]).astype(BF16)
    dt = _softplus(_dot(hb, wdt_ref[...]) + dtb_ref[...])
    dt_ref[...] = dt
    adt_ref[...] = dt * a_ref[...]
    dtT = _softplus(_dot_nt(wdtT_ref[...], hb) + dtbT_ref[...])
    adtT_ref[...] = dtT * aT_ref[...]

    @pl.when(i % tpb == 0)
    def _():
        ubuf[...] = jnp.zeros(ubuf.shape, F32)

    for c0 in range(0, SSM_CONV_DIM, cchunk):
        cols = slice(c0, c0 + cchunk)
        u = _dot(hb, wx_ref[:, cols])
        ext = jnp.concatenate([ubuf[:, cols], u], axis=0)
        acc = cb_ref[:, cols] + cw_ref[SSM_CONV - 1:SSM_CONV, cols] * u
        for j in range(1, SSM_CONV):
            shifted = pltpu.roll(ext, j, axis=0)[8:, :]
            acc = acc + cw_ref[SSM_CONV - 1 - j:SSM_CONV - j, cols] * shifted
        xbc_ref[:, cols] = (acc * _sigmoid(acc)).astype(BF16)
        ubuf[:, cols] = u[tm - 8:, :]


def _ssd_pre(x2d, sh, sc, g, w_in, conv_w, conv_b, dt_bias, a_log, bsz, seqlen, tm):
    n, d = x2d.shape
    tpb = seqlen // tm
    nh = SSM_HEADS
    wz = w_in[:, :SSM_INNER].astype(BF16)
    wx = w_in[:, SSM_INNER:SSM_INNER + SSM_CONV_DIM].astype(BF16)
    wdt_raw = w_in[:, SSM_INNER + SSM_CONV_DIM:]
    wdt = jnp.concatenate([wdt_raw, jnp.zeros((d, LANES - nh), F32)], axis=1).astype(BF16)
    wdtT = wdt_raw.T.astype(BF16)
    pad = jnp.zeros((LANES - nh,), F32)
    dtb = jnp.concatenate([dt_bias, pad]).reshape(1, LANES)
    a_neg = -jnp.exp(a_log.astype(F32))
    a_row = jnp.concatenate([a_neg, pad]).reshape(1, LANES)
    vec = lambda i: (0, 0)
    bvec = lambda i: (i // tpb, 0, 0)
    return pl.pallas_call(
        functools.partial(_ssd_pre_kernel, tm=tm, tpb=tpb, cchunk=1024),
        out_shape=(
            jax.ShapeDtypeStruct((n, SSM_INNER), BF16),
            jax.ShapeDtypeStruct((n, SSM_CONV_DIM), BF16),
            jax.ShapeDtypeStruct((n, LANES), F32),
            jax.ShapeDtypeStruct((n, LANES), F32),
            jax.ShapeDtypeStruct((bsz, nh, seqlen), F32),
        ),
        grid=(n // tm,),
        in_specs=[
            pl.BlockSpec((tm, d), lambda i: (i, 0)),
            pl.BlockSpec((None, 1, d), bvec),
            pl.BlockSpec((None, 1, d), bvec),
            pl.BlockSpec((1, d), vec),
            pl.BlockSpec((d, SSM_INNER), vec),
            pl.BlockSpec((d, SSM_CONV_DIM), vec),
            pl.BlockSpec((d, LANES), vec),
            pl.BlockSpec((nh, d), vec),
            pl.BlockSpec((SSM_CONV, SSM_CONV_DIM), vec),
            pl.BlockSpec((1, SSM_CONV_DIM), vec),
            pl.BlockSpec((1, LANES), vec),
            pl.BlockSpec((nh, 1), vec),
            pl.BlockSpec((1, LANES), vec),
            pl.BlockSpec((nh, 1), vec),
        ],
        out_specs=(
            pl.BlockSpec((tm, SSM_INNER), lambda i: (i, 0)),
            pl.BlockSpec((tm, SSM_CONV_DIM), lambda i: (i, 0)),
            pl.BlockSpec((tm, LANES), lambda i: (i, 0)),
            pl.BlockSpec((tm, LANES), lambda i: (i, 0)),
            pl.BlockSpec((None, nh, tm), lambda i: (i // tpb, 0, i % tpb)),
        ),
        scratch_shapes=[pltpu.VMEM((8, SSM_CONV_DIM), F32)],
        compiler_params=_cparams("arbitrary"),
        name="ssd_pre",
    )(x2d, sh, sc, g.reshape(1, d), wz, wx, wdt, wdtT, conv_w, conv_b.reshape(1, -1), dtb,
      dt_bias.reshape(nh, 1), a_row, a_neg.reshape(nh, 1))


def _expand_heads(v, g, rows):
    lane = lax.broadcasted_iota(I32, (rows, LANES), 1)
    lo = lane < SSM_HEAD_DIM
    parts = []
    for j in range(0, SSM_HPG, 2):
        h0 = g * SSM_HPG + j
        parts.append(jnp.where(lo, v[:, h0:h0 + 1], v[:, h0 + 1:h0 + 2]))
    return jnp.concatenate(parts, axis=1)


def _ssd_scan_kernel(xbc_ref, z_ref, dt_ref, adt_ref, adtT_ref, d_ref, gn_ref, hx_ref, yn_ref, state, *, lc):
    c = pl.program_id(1)

    @pl.when(c == 0)
    def _():
        state[...] = jnp.zeros(state.shape, F32)

    row = lax.broadcasted_iota(I32, (lc, lc), 0)
    col = lax.broadcasted_iota(I32, (lc, lc), 1)
    causal = row >= col
    tri = jnp.where(causal, 1.0, 0.0).astype(BF16)
    triT = jnp.where(row <= col, 1.0, 0.0).astype(BF16)
    ah, am, al = _split3(adt_ref[...])
    a_cum = _dot(tri, ah) + (_dot(tri, am) + _dot(tri, al))
    th, tmid, tl = _split3(adtT_ref[...])
    a_cumT = _dot(th, triT) + (_dot(tmid, triT) + _dot(tl, triT))
    dt = dt_ref[...]
    a_last = a_cum[lc - 1:lc, :]
    e_cum = jnp.exp(a_cum)
    d2e = jnp.exp(a_last - a_cum)
    cdec = jnp.exp(a_last)
    dskip = d_ref[...]
    hx = hx_ref[...]
    dt_x = _dot(dt.astype(BF16), hx)
    dtd2e_x = _dot((dt * d2e).astype(BF16), hx)
    ecum_x = _dot(e_cum.astype(BF16), hx)
    lane2 = lax.broadcasted_iota(I32, (lc, SSM_GROUP_W), 1)
    for g in range(SSM_GROUPS):
        xg = xbc_ref[:, g * SSM_GROUP_W:(g + 1) * SSM_GROUP_W].astype(F32)
        b0 = SSM_INNER + g * SSM_STATE
        c0 = SSM_INNER + SSM_BC + g * SSM_STATE
        bg = xbc_ref[:, b0:b0 + SSM_STATE]
        cg = xbc_ref[:, c0:c0 + SSM_STATE]
        gcols = slice(g * SSM_GROUP_W, (g + 1) * SSM_GROUP_W)
        xdt_b = (xg * dt_x[:, gcols]).astype(BF16)
        cb = _dot_nt(cg, bg)
        y = jnp.zeros((lc, SSM_GROUP_W), F32)
        for j in range(SSM_HPG):
            hd = g * SSM_HPG + j
            seg = a_cum[:, hd:hd + 1] - a_cumT[hd:hd + 1, :]
            decay = jnp.exp(jnp.where(causal, seg, -jnp.inf))
            m = (cb * decay).astype(BF16)
            in_head = (lane2 >= j * SSM_HEAD_DIM) & (lane2 < (j + 1) * SSM_HEAD_DIM)
            y = y + _dot(m, jnp.where(in_head, xdt_b, jnp.zeros_like(xdt_b)))
        st = state[g]
        y = y + _dot(cg, st.astype(BF16)) * ecum_x[:, gcols]
        xd2e = (xg * dtd2e_x[:, gcols]).astype(BF16)
        state[g] = st * _expand_heads(cdec, g, 1) + _dot_tn(bg, xd2e)
        y = y + _expand_heads(dskip, g, 1) * xg
        zg = z_ref[:, g * SSM_GROUP_W:(g + 1) * SSM_GROUP_W].astype(F32)
        yz = y * (zg * _sigmoid(zg))
        yn = _rms(yz) * gn_ref[:, g * SSM_GROUP_W:(g + 1) * SSM_GROUP_W]
        yn_ref[:, g * SSM_GROUP_W:(g + 1) * SSM_GROUP_W] = yn.astype(BF16)


def _ssd_scan(z, xbc, dt, adt, adtT, d_skip, g_norm, bsz, seqlen, lc):
    n = z.shape[0]
    nc = seqlen // lc
    pad = jnp.zeros((LANES - SSM_HEADS,), F32)
    d_row = jnp.concatenate([d_skip, pad]).reshape(1, LANES)
    rows = lambda b, c: (b * nc + c, 0)
    vec = lambda b, c: (0, 0)
    head_of_col = jnp.arange(SSM_INNER, dtype=I32) // SSM_HEAD_DIM
    head_expand = (jnp.arange(LANES, dtype=I32)[:, None] == head_of_col[None, :]).astype(BF16)
    return pl.pallas_call(
        functools.partial(_ssd_scan_kernel, lc=lc),
        out_shape=jax.ShapeDtypeStruct((n, SSM_INNER), BF16),
        grid=(bsz, nc),
        in_specs=[
            pl.BlockSpec((lc, SSM_CONV_DIM), rows),
            pl.BlockSpec((lc, SSM_INNER), rows),
            pl.BlockSpec((lc, LANES), rows),
            pl.BlockSpec((lc, LANES), rows),
            pl.BlockSpec((None, SSM_HEADS, lc), lambda b, c: (b, 0, c)),
            pl.BlockSpec((1, LANES), vec),
            pl.BlockSpec((1, SSM_INNER), vec),
            pl.BlockSpec((LANES, SSM_INNER), vec),
        ],
        out_specs=pl.BlockSpec((lc, SSM_INNER), rows),
        scratch_shapes=[pltpu.VMEM((SSM_GROUPS, SSM_STATE, SSM_GROUP_W), F32)],
        compiler_params=_cparams("arbitrary", "arbitrary"),
        name="ssd_scan",
    )(xbc, z, dt, adt, adtT, d_row, g_norm.reshape(1, -1), head_expand)


def kernel(x, c, positions, w_mod, b_mod, g_mix_norm, g_ffn_norm, mla_w_in, mla_g_q, mla_g_kv, mla_w_q_up, mla_w_kv_up, mla_w_out, ssm_w_in, ssm_conv_w, ssm_conv_b, ssm_dt_bias, ssm_a_log, ssm_d, ssm_g_norm, ssm_w_out, moe_w_router, moe_b_router, moe_w_gate_up, moe_b_gate_up, moe_w_down, moe_b_down, g_final):
    bsz, seqlen, d = x.shape
    depth = w_mod.shape[0]
    n = bsz * seqlen
    tm = min(512, seqlen)
    mod = _modulation(c, w_mod, b_mod)
    mod = mod.reshape(depth, 6, bsz, 1, d)
    pos_f = positions.astype(F32).reshape(n, 1)
    xc = x.reshape(n, d)
    for i in range(depth):
        sh1, sc1, gt1, sh2, sc2, gt2 = [mod[i, j] for j in range(6)]
        j = i // 2
        if i % 2 == 0:
            q, k, v = _mla_pre(xc, pos_f, sh1, sc1, g_mix_norm[i], mla_w_in[j], mla_g_q[j], mla_g_kv[j],
                               mla_w_q_up[j], mla_w_kv_up[j], bsz, seqlen, tm)
            o = _attention(q, k, v, tm).reshape(n, MLA_HEADS * V_HEAD)
            w_out = mla_w_out[j]
        else:
            tms = min(256, seqlen)
            z, xbc, dt, adt, adtT = _ssd_pre(xc, sh1, sc1, g_mix_norm[i], ssm_w_in[j], ssm_conv_w[j],
                                             ssm_conv_b[j], ssm_dt_bias[j], ssm_a_log[j], bsz, seqlen, tms)
            o = _ssd_scan(z, xbc, dt, adt, adtT, ssm_d[j], ssm_g_norm[j], bsz, seqlen, min(256, seqlen))
            w_out = ssm_w_out[j]
        x1, h2, meta, gates, counts = _post_mixer(xc, o, w_out, gt1, sh2, sc2, g_ffn_norm[i],
                                                  moe_w_router[i], moe_b_router[i], seqlen, tm)
        xc = _moe(x1, h2, meta, gates, counts, gt2, g_final, moe_w_gate_up, moe_b_gate_up,
                  moe_w_down, moe_b_down, i, seqlen, final=(i == depth - 1))
    return xc.reshape(bsz, seqlen, d)
```

```python
import functools
import math

import jax
import jax.numpy as jnp
from jax import lax
from jax.experimental import pallas as pl
from jax.experimental.pallas import tpu as pltpu

F32 = jnp.float32
BF16 = jnp.bfloat16
I32 = jnp.int32
U32 = jnp.uint32

NORM_EPS = 1e-6
MLA_HEADS = 8
Q_LORA = 256
KV_LORA = 256
QK_NOPE = 128
QK_ROPE = 64
V_HEAD = 128
QK_HEAD = QK_NOPE + QK_ROPE
ROPE_THETA = 10000.0
MLA_SCALE = QK_HEAD ** -0.5
SSM_HEAD_DIM = 64
SSM_GROUPS = 8
SSM_HPG = 4
SSM_HEADS = SSM_GROUPS * SSM_HPG
SSM_STATE = 128
SSM_CONV = 4
SSM_GROUP_W = SSM_HPG * SSM_HEAD_DIM
SSM_INNER = SSM_GROUPS * SSM_GROUP_W
SSM_BC = SSM_GROUPS * SSM_STATE
SSM_CONV_DIM = SSM_INNER + 2 * SSM_BC
N_EXPERTS = 32
TOP_K = 4
SWIGLU_LIMIT = 7.0
SWIGLU_ALPHA = 1.702
MOE_BLOCK = 256

LANES = 128
VMEM_LIMIT = 56 * 1024 * 1024


def _cparams(*sem):
    return pltpu.CompilerParams(dimension_semantics=tuple(sem), vmem_limit_bytes=VMEM_LIMIT)


def _dot(a, b):
    return jnp.dot(a, b, preferred_element_type=F32)


def _dot_nt(a, b):
    return lax.dot_general(a, b, (((1,), (1,)), ((), ())), preferred_element_type=F32)


def _dot_tn(a, b):
    return lax.dot_general(a, b, (((0,), (0,)), ((), ())), preferred_element_type=F32)


def _split3(a):
    hi = a.astype(BF16)
    r1 = a - hi.astype(F32)
    mid = r1.astype(BF16)
    lo = (r1 - mid.astype(F32)).astype(BF16)
    return hi, mid, lo


def _dot_f32ish(a, b):
    ah, am, _ = _split3(a)
    bh, bm, _ = _split3(b)
    return _dot(ah, bh) + (_dot(ah, bm) + _dot(am, bh))


def _pack_bf16_pair(lo, hi):
    lo_b = lax.bitcast_convert_type(lo.astype(BF16).astype(F32), U32)
    hi_b = lax.bitcast_convert_type(hi.astype(BF16).astype(F32), U32)
    return hi_b | lax.shift_right_logical(lo_b, jnp.uint32(16))


def _unpack_bf16_pair(w):
    lo = lax.bitcast_convert_type(lax.shift_left(w, jnp.uint32(16)), F32)
    hi = lax.bitcast_convert_type(w & jnp.uint32(0xFFFF0000), F32)
    return lo, hi


ROW_CHUNKS = 4


def _chunk_copies(buf2d, rows_hbm, r0, nrows, sem, *, to_hbm):
    cps = []
    for c in range(ROW_CHUNKS):
        v = buf2d.at[:, pl.ds(c * LANES, LANES)]
        h = rows_hbm.at[pl.ds(r0, nrows), c]
        cps.append(pltpu.make_async_copy(v, h, sem) if to_hbm else pltpu.make_async_copy(h, v, sem))
    return cps


def _sigmoid(x):
    return 1.0 / (1.0 + jnp.exp(-x))


def _rms(x):
    return x * lax.rsqrt(jnp.mean(x * x, axis=-1, keepdims=True) + NORM_EPS)


def _mod_kernel(c_ref, w_ref, b_ref, o_ref):
    c = c_ref[...]
    cond = c * _sigmoid(c)
    o_ref[...] = _dot_f32ish(cond, w_ref[...]) + b_ref[...]


def _modulation(c, w_mod, b_mod):
    depth, d, d6 = w_mod.shape
    bsz = c.shape[0]
    nj = d6 // d
    return pl.pallas_call(
        _mod_kernel,
        out_shape=jax.ShapeDtypeStruct((depth, nj, bsz, d), F32),
        grid=(depth, nj),
        in_specs=[
            pl.BlockSpec((bsz, d), lambda l, j: (0, 0)),
            pl.BlockSpec((None, d, d), lambda l, j: (l, 0, j)),
            pl.BlockSpec((None, 1, d), lambda l, j: (l, 0, j)),
        ],
        out_specs=pl.BlockSpec((None, None, bsz, d), lambda l, j: (l, j, 0, 0)),
        compiler_params=_cparams("arbitrary", "arbitrary"),
        name="mod",
    )(c, w_mod, b_mod.reshape(depth, 1, d6))


def _mla_pre_kernel(x_ref, pos_ref, sh_ref, sc_ref, g_ref, win_ref, gq_ref, gkv_ref, wq_ref, wkv_ref,
                    invf_ref, q_ref, k_ref, v_ref):
    x = x_ref[...]
    h = _rms(x) * g_ref[...] * (1.0 + sc_ref[...]) + sh_ref[...]
    lat = _dot(h.astype(BF16), win_ref[...])
    q_lat = _rms(lat[:, :Q_LORA]) * gq_ref[...]
    kv_lat = _rms(lat[:, Q_LORA:Q_LORA + KV_LORA]) * gkv_ref[...]
    ang = pos_ref[...] * invf_ref[...]
    cs = jnp.cos(ang)
    sn = jnp.sin(ang)
    o = Q_LORA + KV_LORA
    k_rope = lat[:, o:o + LANES] * cs + lat[:, o + LANES:o + 2 * LANES] * sn
    qq = _dot(q_lat.astype(BF16), wq_ref[...])
    kv = _dot(kv_lat.astype(BF16), wkv_ref[...])
    rot0 = MLA_HEADS * 2 * LANES
    k_rope_b = k_rope[:, :QK_ROPE].astype(BF16)
    for hd in range(MLA_HEADS):
        c0 = hd * 2 * LANES
        q_nope = qq[:, c0:c0 + LANES] * MLA_SCALE
        q_rope = (qq[:, c0 + LANES:c0 + 2 * LANES] * cs
                  + qq[:, rot0 + hd * LANES:rot0 + (hd + 1) * LANES] * sn) * MLA_SCALE
        q_ref[hd, :, 0:QK_NOPE] = q_nope.astype(BF16)
        q_ref[hd, :, QK_NOPE:QK_HEAD] = q_rope[:, :QK_ROPE].astype(BF16)
        k_ref[hd, :, 0:QK_NOPE] = kv[:, c0:c0 + LANES].astype(BF16)
        k_ref[hd, :, QK_NOPE:QK_HEAD] = k_rope_b
        v_ref[hd] = kv[:, c0 + LANES:c0 + 2 * LANES].astype(BF16)


def _rot_half_cols(w):
    half = QK_ROPE // 2
    return jnp.concatenate([-w[..., half:], w[..., :half]], axis=-1)


def _mla_pre(x2d, pos_f, sh, sc, g, w_in, g_q, g_kv, w_q_up, w_kv_up, bsz, seqlen, tm):
    n, d = x2d.shape
    hh = MLA_HEADS
    o = Q_LORA + KV_LORA
    wr = w_in[:, o:o + QK_ROPE]
    zpad = jnp.zeros((d, LANES - QK_ROPE), F32)
    w_in_ext = jnp.concatenate([w_in[:, :o], wr, zpad, _rot_half_cols(wr), zpad], axis=1).astype(BF16)
    wq = w_q_up.reshape(Q_LORA, hh, QK_HEAD)
    zq = jnp.zeros((Q_LORA, hh, LANES - QK_ROPE), F32)
    wq_main = jnp.concatenate([wq, zq], axis=-1).reshape(Q_LORA, hh * 2 * LANES)
    wq_rot = jnp.concatenate([_rot_half_cols(wq[..., QK_NOPE:]), zq], axis=-1).reshape(Q_LORA, hh * LANES)
    wq_ext = jnp.concatenate([wq_main, wq_rot], axis=1).astype(BF16)
    inv_freq = 1.0 / (ROPE_THETA ** (jnp.arange(0, QK_ROPE, 2, dtype=F32) / QK_ROPE))
    invf = jnp.concatenate([inv_freq, inv_freq, jnp.zeros((LANES - QK_ROPE,), F32)]).reshape(1, LANES)
    tpb = seqlen // tm
    vec = lambda i: (0, 0)
    outs = pl.pallas_call(
        _mla_pre_kernel,
        out_shape=(
            jax.ShapeDtypeStruct((bsz, hh, seqlen, QK_HEAD), BF16),
            jax.ShapeDtypeStruct((bsz, hh, seqlen, QK_HEAD), BF16),
            jax.ShapeDtypeStruct((bsz, hh, seqlen, V_HEAD), BF16),
        ),
        grid=(n // tm,),
        in_specs=[
            pl.BlockSpec((tm, d), lambda i: (i, 0)),
            pl.BlockSpec((tm, 1), lambda i: (i, 0)),
            pl.BlockSpec((None, 1, d), lambda i: (i // tpb, 0, 0)),
            pl.BlockSpec((None, 1, d), lambda i: (i // tpb, 0, 0)),
            pl.BlockSpec((1, d), vec),
            pl.BlockSpec(w_in_ext.shape, vec),
            pl.BlockSpec((1, Q_LORA), vec),
            pl.BlockSpec((1, KV_LORA), vec),
            pl.BlockSpec(wq_ext.shape, vec),
            pl.BlockSpec((KV_LORA, hh * 2 * LANES), vec),
            pl.BlockSpec((1, LANES), vec),
        ],
        out_specs=(
            pl.BlockSpec((None, hh, tm, QK_HEAD), lambda i: (i // tpb, 0, i % tpb, 0)),
            pl.BlockSpec((None, hh, tm, QK_HEAD), lambda i: (i // tpb, 0, i % tpb, 0)),
            pl.BlockSpec((None, hh, tm, V_HEAD), lambda i: (i // tpb, 0, i % tpb, 0)),
        ),
        compiler_params=_cparams("arbitrary"),
        name="mla_pre",
    )(x2d, pos_f, sh, sc, g.reshape(1, d), w_in_ext, g_q.reshape(1, -1), g_kv.reshape(1, -1), wq_ext,
      w_kv_up.astype(BF16), invf)
    return outs


ATTN_HEADS_PER_STEP = 4


def _attn_kernel(q_ref, k_ref, v_ref, o_ref, m_scr, acc_scr, *, tq, hp):
    qi = pl.program_id(2)
    m_scr[...] = jnp.full(m_scr.shape, -jnp.inf, F32)
    acc_scr[...] = jnp.zeros(acc_scr.shape, F32)

    def block(hd, r0, tk, masked):
        k = k_ref[hd, pl.ds(r0, tk), :]
        v_ext = jnp.concatenate([v_ref[hd, pl.ds(r0, tk), :], jnp.ones((tk, V_HEAD), BF16)], axis=1)
        s = _dot_nt(q_ref[hd], k)
        if masked:
            row = lax.broadcasted_iota(I32, (tq, tk), 0)
            col = lax.broadcasted_iota(I32, (tq, tk), 1)
            s = jnp.where(row >= col, s, -jnp.inf)
        m_prev = m_scr[hd]
        m_new = jnp.maximum(m_prev, jnp.max(s, axis=-1, keepdims=True))
        alpha = jnp.exp(m_prev - m_new)
        p = jnp.exp(s - m_new).astype(BF16)
        acc_scr[hd] = alpha * acc_scr[hd] + _dot(p, v_ext)
        m_scr[hd] = m_new

    def body(j, carry):
        for hd in range(hp):
            block(hd, pl.multiple_of(j * (2 * tq), 2 * tq), 2 * tq, False)
        return carry

    lax.fori_loop(0, qi // 2, body, 0)

    @pl.when(qi % 2 == 1)
    def _():
        for hd in range(hp):
            block(hd, pl.multiple_of((qi - 1) * tq, tq), tq, False)

    for hd in range(hp):
        block(hd, pl.multiple_of(qi * tq, tq), tq, True)
    for hd in range(hp):
        acc = acc_scr[hd]
        o_ref[:, hd * V_HEAD:(hd + 1) * V_HEAD] = (acc[:, :V_HEAD] / acc[:, V_HEAD:]).astype(o_ref.dtype)


def _attention(q, k, v, tq):
    bsz, hh, seqlen, _ = q.shape
    hp = ATTN_HEADS_PER_STEP
    return pl.pallas_call(
        functools.partial(_attn_kernel, tq=tq, hp=hp),
        out_shape=jax.ShapeDtypeStruct((bsz, seqlen, hh * V_HEAD), BF16),
        grid=(bsz, hh // hp, seqlen // tq),
        in_specs=[
            pl.BlockSpec((None, hp, tq, QK_HEAD), lambda b, h, i: (b, h, i, 0)),
            pl.BlockSpec((None, hp, seqlen, QK_HEAD), lambda b, h, i: (b, h, 0, 0)),
            pl.BlockSpec((None, hp, seqlen, V_HEAD), lambda b, h, i: (b, h, 0, 0)),
        ],
        out_specs=pl.BlockSpec((None, tq, hp * V_HEAD), lambda b, h, i: (b, i, h)),
        scratch_shapes=[
            pltpu.VMEM((hp, tq, 1), F32),
            pltpu.VMEM((hp, tq, 2 * V_HEAD), F32),
        ],
        compiler_params=_cparams("arbitrary", "arbitrary", "arbitrary"),
        name="attn",
    )(q, k, v)


def _post_mixer_kernel(x_ref, o_ref, wout_ref, gt1_ref, sh_ref, sc_ref, g_ref, wr_ref, br_ref,
                       x1_ref, h2_hbm, meta_ref, gate_ref, cnt_ref, carry_scr, pbuf, psem, *, tm):
    i = pl.program_id(0)

    @pl.when(i == 0)
    def _():
        carry_scr[...] = jnp.zeros(carry_scr.shape, F32)

    y = _dot(o_ref[...], wout_ref[...])
    x1 = x_ref[...] + gt1_ref[...] * y
    x1_ref[...] = x1
    h2 = _rms(x1) * g_ref[...] * (1.0 + sc_ref[...]) + sh_ref[...]
    half = h2.shape[1] // 2

    @pl.when(i > 0)
    def _():
        for cp in _chunk_copies(pbuf, h2_hbm, (i - 1) * tm, tm, psem, to_hbm=True):
            cp.wait()

    pbuf[...] = _pack_bf16_pair(h2[:, :half], h2[:, half:])
    for cp in _chunk_copies(pbuf, h2_hbm, i * tm, tm, psem, to_hbm=True):
        cp.start()

    logits = _dot_f32ish(h2, wr_ref[...]) + br_ref[...]
    lane = lax.broadcasted_iota(I32, (tm, LANES), 1).astype(F32)
    work = logits
    idxs, vals = [], []
    for _ in range(TOP_K):
        mx = jnp.max(work, axis=-1, keepdims=True)
        idx = jnp.min(jnp.where(work == mx, lane, float(LANES)), axis=-1, keepdims=True)
        idxs.append(idx)
        vals.append(mx)
        work = jnp.where(lane == idx, -jnp.inf, work)
    exps = [jnp.exp(vk - vals[0]) for vk in vals]
    denom = exps[0] + exps[1] + exps[2] + exps[3]
    onehot = jnp.zeros((tm, LANES), F32)
    for idx in idxs:
        onehot = onehot + jnp.where(lane == idx, 1.0, 0.0)
    row = lax.broadcasted_iota(I32, (tm, tm), 0)
    col = lax.broadcasted_iota(I32, (tm, tm), 1)
    ltri = jnp.where(row > col, 1.0, 0.0).astype(BF16)
    cum = _dot(ltri, onehot.astype(BF16)) + carry_scr[...]
    meta = jnp.zeros((tm, LANES), F32)
    gates = jnp.zeros((tm, LANES), F32)
    for kk in range(TOP_K):
        rank = jnp.sum(jnp.where(lane == idxs[kk], cum, 0.0), axis=-1, keepdims=True)
        meta = jnp.where(lane == float(kk), idxs[kk], meta)
        meta = jnp.where(lane == float(TOP_K + kk), rank, meta)
        gates = jnp.where(lane == float(kk), exps[kk] / denom, gates)
    meta_ref[...] = jnp.transpose(meta)[:2 * TOP_K, :].astype(I32)
    gate_ref[...] = gates
    carry = carry_scr[...] + jnp.sum(onehot, axis=0, keepdims=True)
    carry_scr[...] = carry
    cnt_ref[...] = carry.astype(I32)

    @pl.when(i == pl.num_programs(0) - 1)
    def _():
        for cp in _chunk_copies(pbuf, h2_hbm, i * tm, tm, psem, to_hbm=True):
            cp.wait()


def _post_mixer(x2d, o2d, w_out, gt1, sh2, sc2, g_ffn, w_router, b_router, seqlen, tm):
    n, d = x2d.shape
    kdim = o2d.shape[1]
    tpb = seqlen // tm
    wr = jnp.concatenate([w_router, jnp.zeros((d, LANES - N_EXPERTS), F32)], axis=1)
    br = jnp.concatenate([b_router, jnp.full((LANES - N_EXPERTS,), -1e30, F32)]).reshape(1, LANES)
    vec = lambda i: (0, 0)
    bvec = lambda i: (i // tpb, 0, 0)
    return pl.pallas_call(
        functools.partial(_post_mixer_kernel, tm=tm),
        out_shape=(
            jax.ShapeDtypeStruct((n, d), F32),
            jax.ShapeDtypeStruct((n, ROW_CHUNKS, LANES), U32),
            jax.ShapeDtypeStruct((2 * TOP_K, n), I32),
            jax.ShapeDtypeStruct((n, LANES), F32),
            jax.ShapeDtypeStruct((1, LANES), I32),
        ),
        grid=(n // tm,),
        in_specs=[
            pl.BlockSpec((tm, d), lambda i: (i, 0)),
            pl.BlockSpec((tm, kdim), lambda i: (i, 0)),
            pl.BlockSpec((kdim, d), vec),
            pl.BlockSpec((None, 1, d), bvec),
            pl.BlockSpec((None, 1, d), bvec),
            pl.BlockSpec((None, 1, d), bvec),
            pl.BlockSpec((1, d), vec),
            pl.BlockSpec((d, LANES), vec),
            pl.BlockSpec((1, LANES), vec),
        ],
        out_specs=(
            pl.BlockSpec((tm, d), lambda i: (i, 0)),
            pl.BlockSpec(memory_space=pl.ANY),
            pl.BlockSpec((2 * TOP_K, tm), lambda i: (0, i)),
            pl.BlockSpec((tm, LANES), lambda i: (i, 0)),
            pl.BlockSpec((1, LANES), vec),
        ),
        scratch_shapes=[pltpu.VMEM((1, LANES), F32), pltpu.VMEM((tm, d // 2), U32), pltpu.SemaphoreType.DMA],
        compiler_params=_cparams("arbitrary"),
        name="post_mixer",
    )(x2d, o2d, w_out.astype(BF16), gt1, sh2, sc2, g_ffn.reshape(1, d), wr, br)


ROW_UNROLL = 8


def _row_copy(src, s, dst, t, sem):
    return pltpu.make_async_copy(src.at[s], dst.at[t], sem)


def _dest_kernel(pstart_ref, meta_ref, dest_ref):
    e = meta_ref[0:TOP_K, :]
    dest = meta_ref[TOP_K:2 * TOP_K, :]
    for j in range(N_EXPERTS):
        dest = dest + jnp.where(e == j, pstart_ref[j], 0)
    dest_ref[...] = dest


def _dest_rows(meta_t, pstart):
    n = meta_t.shape[1]
    tn = min(4096, n)
    grid_spec = pltpu.PrefetchScalarGridSpec(
        num_scalar_prefetch=1,
        grid=(n // tn,),
        in_specs=[pl.BlockSpec((2 * TOP_K, tn), lambda i, ps: (0, i))],
        out_specs=pl.BlockSpec((TOP_K, tn), lambda i, ps: (0, i)),
    )
    return pl.pallas_call(
        _dest_kernel,
        out_shape=jax.ShapeDtypeStruct((TOP_K, n), I32),
        grid_spec=grid_spec,
        compiler_params=_cparams("arbitrary"),
        name="dest_rows",
    )(pstart, meta_t)


def _dispatch_kernel(pstart_ref, cnt_ref, dest_ref, h_ref, xs_hbm, zbuf, sem, zsem, *, tm):
    i = pl.program_id(0)

    @pl.when(i == 0)
    def _():
        zbuf[...] = jnp.zeros(zbuf.shape, U32)

        def per_expert(e, carry):
            lo = pstart_ref[e] + cnt_ref[e]
            hi = pstart_ref[e + 1]

            def start(j, c):
                _row_copy(zbuf, 0, xs_hbm, j, zsem).start()
                return c

            def wait(j, c):
                _row_copy(zbuf, 0, xs_hbm, j, zsem).wait()
                return c

            lax.fori_loop(lo, hi, start, 0)
            lax.fori_loop(lo, hi, wait, 0)
            return carry

        lax.fori_loop(0, N_EXPERTS, per_expert, 0)

    def start(tt, c):
        for u in range(ROW_UNROLL):
            t = tt * ROW_UNROLL + u
            for kk in range(TOP_K):
                _row_copy(h_ref, t, xs_hbm, dest_ref[kk, t], sem).start(priority=kk % 2)
        return c

    def wait(tt, c):
        for u in range(ROW_UNROLL * TOP_K):
            _row_copy(h_ref, 0, xs_hbm, 0, sem).wait()
        return c

    lax.fori_loop(0, tm // ROW_UNROLL, start, 0)
    lax.fori_loop(0, tm // ROW_UNROLL, wait, 0)


def _dispatch(h2p, dest_t, pstart, counts, n_rows, tm):
    n = h2p.shape[0]
    grid_spec = pltpu.PrefetchScalarGridSpec(
        num_scalar_prefetch=2,
        grid=(n // tm,),
        in_specs=[
            pl.BlockSpec((TOP_K, tm), lambda i, ps, cn: (0, i), memory_space=pltpu.SMEM),
            pl.BlockSpec((tm, ROW_CHUNKS, LANES), lambda i, ps, cn: (i, 0, 0)),
        ],
        out_specs=pl.BlockSpec(memory_space=pl.ANY),
        scratch_shapes=[
            pltpu.VMEM((1, ROW_CHUNKS, LANES), U32),
            pltpu.SemaphoreType.DMA,
            pltpu.SemaphoreType.DMA,
        ],
    )
    return pl.pallas_call(
        functools.partial(_dispatch_kernel, tm=tm),
        out_shape=jax.ShapeDtypeStruct((n_rows, ROW_CHUNKS, LANES), U32),
        grid_spec=grid_spec,
        compiler_params=pltpu.CompilerParams(dimension_semantics=("arbitrary",), vmem_limit_bytes=VMEM_LIMIT,
                                             has_side_effects=True),
        name="dispatch",
    )(pstart, counts, dest_t, h2p)


def _ffn_kernel(be_ref, nu_ref, xs_hbm, wgu_ref, bgu_ref, wd_ref, bd_ref, ys_hbm, wgu_b, wd_b, xbuf, ybuf,
                xsem, ysem, *, ff):
    i = pl.program_id(0)
    nu = nu_ref[0]
    slot = i % 2
    prev = be_ref[jnp.maximum(i - 1, 0)]
    changed = jnp.logical_or(i == 0, be_ref[i] != prev)

    def x_copies(step, s):
        return _chunk_copies(xbuf.at[s], xs_hbm, step * MOE_BLOCK, MOE_BLOCK, xsem.at[s], to_hbm=False)

    def y_copies(step, s):
        return _chunk_copies(ybuf.at[s], ys_hbm, step * MOE_BLOCK, MOE_BLOCK, ysem.at[s], to_hbm=True)

    @pl.when(i == 0)
    def _():
        for cp in x_copies(0, 0):
            cp.start()

    @pl.when(i + 1 < nu)
    def _():
        for cp in x_copies(i + 1, 1 - slot):
            cp.start()

    @pl.when(changed)
    def _():
        wgu_b[...] = wgu_ref[...].astype(BF16)
        wd_b[...] = wd_ref[...].astype(BF16)

    @pl.when(i < nu)
    def _():
        for cp in x_copies(i, slot):
            cp.wait()
        x_lo, x_hi = _unpack_bf16_pair(xbuf[slot])
        dh = x_lo.shape[1]
        gu = (_dot(x_lo.astype(BF16), wgu_b[0:dh, :]) + _dot(x_hi.astype(BF16), wgu_b[dh:2 * dh, :])
              + bgu_ref[...])
        g = jnp.minimum(gu[:, :ff], SWIGLU_LIMIT)
        lin = jnp.clip(gu[:, ff:], -SWIGLU_LIMIT, SWIGLU_LIMIT)
        act = g * _sigmoid(SWIGLU_ALPHA * g) * (lin + 1.0)
        y = _dot(act.astype(BF16), wd_b[...]) + bd_ref[...]

        @pl.when(i >= 2)
        def _():
            for cp in y_copies(i - 2, slot):
                cp.wait()

        ybuf[slot] = _pack_bf16_pair(y[:, :dh], y[:, dh:])
        for cp in y_copies(i, slot):
            cp.start()

    @pl.when(i == nu - 1)
    def _():
        for cp in y_copies(i, slot):
            cp.wait()

        @pl.when(i >= 1)
        def _():
            for cp in y_copies(i - 1, 1 - slot):
                cp.wait()


def _ffn(xs, blk_e, n_used, w_gate_up, b_gate_up, w_down, b_down, layer):
    n_rows = xs.shape[0]
    depth, ne, d, ff2 = w_gate_up.shape
    ff = ff2 // 2
    n_blk = n_rows // MOE_BLOCK
    row_buf = pltpu.VMEM((2, MOE_BLOCK, ROW_CHUNKS * LANES), U32)
    grid_spec = pltpu.PrefetchScalarGridSpec(
        num_scalar_prefetch=2,
        grid=(n_blk,),
        in_specs=[
            pl.BlockSpec(memory_space=pl.ANY),
            pl.BlockSpec((None, None, d, ff2), lambda i, be, nu: (layer, be[i], 0, 0)),
            pl.BlockSpec((None, None, 1, ff2), lambda i, be, nu: (layer, be[i], 0, 0)),
            pl.BlockSpec((None, None, ff, d), lambda i, be, nu: (layer, be[i], 0, 0)),
            pl.BlockSpec((None, None, 1, d), lambda i, be, nu: (layer, be[i], 0, 0)),
        ],
        out_specs=pl.BlockSpec(memory_space=pl.ANY),
        scratch_shapes=[pltpu.VMEM((d, ff2), BF16), pltpu.VMEM((ff, d), BF16), row_buf, row_buf,
                        pltpu.SemaphoreType.DMA((2,)), pltpu.SemaphoreType.DMA((2,))],
    )
    return pl.pallas_call(
        functools.partial(_ffn_kernel, ff=ff),
        out_shape=jax.ShapeDtypeStruct((n_rows, ROW_CHUNKS, LANES), U32),
        grid_spec=grid_spec,
        compiler_params=_cparams("arbitrary"),
        name="ffn",
    )(blk_e, n_used, xs, w_gate_up, b_gate_up.reshape(depth, ne, 1, ff2), w_down,
      b_down.reshape(depth, ne, 1, d))


def _gather_scratch(tm):
    return [pltpu.VMEM((2, TOP_K, tm // ROW_UNROLL, ROW_CHUNKS, ROW_UNROLL, LANES), U32),
            pltpu.SemaphoreType.DMA((2,))]


def _gather_start(dest_ref, ys_hbm, buf, sem, tm):
    def body(tt, c):
        for u in range(ROW_UNROLL):
            for kk in range(TOP_K):
                src = ys_hbm.at[dest_ref[kk, tt * ROW_UNROLL + u]]
                pltpu.make_async_copy(src, buf.at[kk, tt, :, u], sem).start(priority=kk % 2)
        return c

    lax.fori_loop(0, tm // ROW_UNROLL, body, 0)


def _gather_wait(ys_hbm, buf, sem, tm):
    def body(tt, c):
        for u in range(ROW_UNROLL * TOP_K):
            pltpu.make_async_copy(ys_hbm.at[0], buf.at[0, 0, :, 0], sem).wait()
        return c

    lax.fori_loop(0, tm // ROW_UNROLL, body, 0)


def _combine_rows(dest0_ref, dest_next_ref, gate_ref, ys_hbm, buf, sem, tm):
    i = pl.program_id(0)
    slot = i % 2

    @pl.when(i == 0)
    def _():
        _gather_start(dest0_ref, ys_hbm, buf.at[0], sem.at[0], tm)

    @pl.when(i + 1 < pl.num_programs(0))
    def _():
        _gather_start(dest_next_ref, ys_hbm, buf.at[1 - slot], sem.at[1 - slot], tm)

    _gather_wait(ys_hbm, buf.at[slot], sem.at[slot], tm)
    gates = gate_ref[...]
    acc_lo = acc_hi = None
    for kk in range(TOP_K):
        rows = jnp.concatenate([buf[slot, kk, :, c, :, :].reshape(tm, LANES) for c in range(ROW_CHUNKS)], axis=1)
        lo, hi = _unpack_bf16_pair(rows)
        gk = gates[:, kk:kk + 1]
        acc_lo = gk * lo if acc_lo is None else acc_lo + gk * lo
        acc_hi = gk * hi if acc_hi is None else acc_hi + gk * hi
    return jnp.concatenate([acc_lo, acc_hi], axis=1)


def _combine_specs(tm, nsteps, idx):
    return [pl.BlockSpec((TOP_K, tm), lambda *a: (0, 0), memory_space=pltpu.SMEM),
            pl.BlockSpec((TOP_K, tm), lambda *a: (0, jnp.minimum(idx(*a) + 1, nsteps - 1)), memory_space=pltpu.SMEM)]


def _combine_kernel(dest0_ref, dest_next_ref, x1_ref, gate_ref, gt2_ref, gfin_ref, ys_hbm, o_ref, buf, sem,
                    *, tm, final):
    y = _combine_rows(dest0_ref, dest_next_ref, gate_ref, ys_hbm, buf, sem, tm)
    x2 = x1_ref[...] + gt2_ref[...] * y
    if final:
        x2 = _rms(x2) * gfin_ref[...]
    o_ref[...] = x2


def _combine(x1, gates, gt2, g_final, ys, dest_t, seqlen, tm, final):
    n, d = x1.shape
    tpb = seqlen // tm
    grid_spec = pltpu.PrefetchScalarGridSpec(
        num_scalar_prefetch=0,
        grid=(n // tm,),
        in_specs=_combine_specs(tm, n // tm, lambda i: i) + [
            pl.BlockSpec((tm, d), lambda i: (i, 0)),
            pl.BlockSpec((tm, LANES), lambda i: (i, 0)),
            pl.BlockSpec((None, 1, d), lambda i: (i // tpb, 0, 0)),
            pl.BlockSpec((1, d), lambda i: (0, 0)),
            pl.BlockSpec(memory_space=pl.ANY),
        ],
        out_specs=pl.BlockSpec((tm, d), lambda i: (i, 0)),
        scratch_shapes=_gather_scratch(tm),
    )
    return pl.pallas_call(
        functools.partial(_combine_kernel, tm=tm, final=final),
        out_shape=jax.ShapeDtypeStruct((n, d), F32),
        grid_spec=grid_spec,
        compiler_params=_cparams("arbitrary"),
        name="combine",
    )(dest_t, dest_t, x1, gates, gt2, g_final.reshape(1, d), ys)


def _moe(x1, h2p, meta_t, gates, counts, gt2, g_final, w_gate_up, b_gate_up, w_down, b_down, layer, seqlen,
         final, defer_combine):
    n, d = x1.shape
    n_pair = n * TOP_K
    n_rows = -(-n_pair // MOE_BLOCK) * MOE_BLOCK + N_EXPERTS * MOE_BLOCK
    n_blk = n_rows // MOE_BLOCK
    cnt = counts[0, :N_EXPERTS]
    padded = (cnt + MOE_BLOCK - 1) // MOE_BLOCK * MOE_BLOCK
    pad_end = jnp.cumsum(padded)
    pstart = jnp.concatenate([pad_end - padded, pad_end[-1:]]).astype(I32)
    n_used = (pad_end[-1:] // MOE_BLOCK).astype(I32)
    blk_start = jnp.arange(n_blk, dtype=I32) * MOE_BLOCK
    blk_e = jnp.minimum(jnp.sum(blk_start[:, None] >= pad_end[None, :], axis=1), N_EXPERTS - 1).astype(I32)
    tm = min(256, seqlen)
    dest_t = _dest_rows(meta_t, pstart)
    xs = _dispatch(h2p, dest_t, pstart, cnt, n_rows, tm)
    ys = _ffn(xs, blk_e, n_used, w_gate_up, b_gate_up, w_down, b_down, layer)
    if defer_combine:
        return x1, gates, gt2, ys, dest_t
    return _combine(x1, gates, gt2, g_final, ys, dest_t, seqlen, tm, final)


def _softplus(x):
    return jnp.maximum(x, 0.0) + jnp.log(1.0 + jnp.exp(-jnp.abs(x)))


SSD_TILE = 256


def _ssd_pre_kernel(dest0_ref, dest_next_ref, x1_ref, gate_ref, gt2_ref, ys_hbm,
                    sh_ref, sc_ref, g_ref, wz_ref, wx_ref, wdt_ref, wdtT_ref, cw_ref, cb_ref,
                    dtb_ref, dtbT_ref, a_ref, aT_ref,
                    x_out_ref, z_ref, xbc_ref, dt_ref, adt_ref, adtT_ref, ubuf, gbuf, gsem, *, tm, tpb, cchunk):
    i = pl.program_id(0)
    x = x1_ref[...] + gt2_ref[...] * _combine_rows(dest0_ref, dest_next_ref, gate_ref, ys_hbm, gbuf, gsem, tm)
    x_out_ref[...] = x
    h = _rms(x) * g_ref[...] * (1.0 + sc_ref[...]) + sh_ref[...]
    hb = h.astype(BF16)
    z_ref[...] = _dot(hb, wz_ref[...]).astype(BF16)
    dt = _softplus(_dot(hb, wdt_ref[...]) + dtb_ref[...])
    dt_ref[...] = dt
    adt_ref[...] = dt * a_ref[...]
    dtT = _softplus(_dot_nt(wdtT_ref[...], hb) + dtbT_ref[...])
    adtT_ref[...] = dtT * aT_ref[...]

    @pl.when(i % tpb == 0)
    def _():
        ubuf[...] = jnp.zeros(ubuf.shape, F32)

    for c0 in range(0, SSM_CONV_DIM, cchunk):
        cols = slice(c0, c0 + cchunk)
        u = _dot(hb, wx_ref[:, cols])
        ext = jnp.concatenate([ubuf[:, cols], u], axis=0)
        acc = cb_ref[:, cols] + cw_ref[SSM_CONV - 1:SSM_CONV, cols] * u
        for j in range(1, SSM_CONV):
            shifted = pltpu.roll(ext, j, axis=0)[8:, :]
            acc = acc + cw_ref[SSM_CONV - 1 - j:SSM_CONV - j, cols] * shifted
        xbc_ref[:, cols] = (acc * _sigmoid(acc)).astype(BF16)
        ubuf[:, cols] = u[tm - 8:, :]


def _ssd_pre(moe_out, sh, sc, g, w_in, conv_w, conv_b, dt_bias, a_log, bsz, seqlen, tm):
    x1, gates, gt2, ys, dest_t = moe_out
    n, d = x1.shape
    tm = min(tm, seqlen)
    tpb = seqlen // tm
    nh = SSM_HEADS
    wz = w_in[:, :SSM_INNER].astype(BF16)
    wx = w_in[:, SSM_INNER:SSM_INNER + SSM_CONV_DIM].astype(BF16)
    wdt_raw = w_in[:, SSM_INNER + SSM_CONV_DIM:]
    wdt = jnp.concatenate([wdt_raw, jnp.zeros((d, LANES - nh), F32)], axis=1).astype(BF16)
    wdtT = wdt_raw.T.astype(BF16)
    pad = jnp.zeros((LANES - nh,), F32)
    dtb = jnp.concatenate([dt_bias, pad]).reshape(1, LANES)
    a_neg = -jnp.exp(a_log.astype(F32))
    a_row = jnp.concatenate([a_neg, pad]).reshape(1, LANES)
    vec = lambda i: (0, 0)
    bvec = lambda i: (i // tpb, 0, 0)
    return pl.pallas_call(
        functools.partial(_ssd_pre_kernel, tm=tm, tpb=tpb, cchunk=1024),
        out_shape=(
            jax.ShapeDtypeStruct((n, d), F32),
            jax.ShapeDtypeStruct((n, SSM_INNER), BF16),
            jax.ShapeDtypeStruct((n, SSM_CONV_DIM), BF16),
            jax.ShapeDtypeStruct((n, LANES), F32),
            jax.ShapeDtypeStruct((n, LANES), F32),
            jax.ShapeDtypeStruct((bsz, nh, seqlen), F32),
        ),
        grid=(n // tm,),
        in_specs=_combine_specs(tm, n // tm, lambda i: i) + [
            pl.BlockSpec((tm, d), lambda i: (i, 0)),
            pl.BlockSpec((tm, LANES), lambda i: (i, 0)),
            pl.BlockSpec((None, 1, d), bvec),
            pl.BlockSpec(memory_space=pl.ANY),
            pl.BlockSpec((None, 1, d), bvec),
            pl.BlockSpec((None, 1, d), bvec),
            pl.BlockSpec((1, d), vec),
            pl.BlockSpec((d, SSM_INNER), vec),
            pl.BlockSpec((d, SSM_CONV_DIM), vec),
            pl.BlockSpec((d, LANES), vec),
            pl.BlockSpec((nh, d), vec),
            pl.BlockSpec((SSM_CONV, SSM_CONV_DIM), vec),
            pl.BlockSpec((1, SSM_CONV_DIM), vec),
            pl.BlockSpec((1, LANES), vec),
            pl.BlockSpec((nh, 1), vec),
            pl.BlockSpec((1, LANES), vec),
            pl.BlockSpec((nh, 1), vec),
        ],
        out_specs=(
            pl.BlockSpec((tm, d), lambda i: (i, 0)),
            pl.BlockSpec((tm, SSM_INNER), lambda i: (i, 0)),
            pl.BlockSpec((tm, SSM_CONV_DIM), lambda i: (i, 0)),
            pl.BlockSpec((tm, LANES), lambda i: (i, 0)),
            pl.BlockSpec((tm, LANES), lambda i: (i, 0)),
            pl.BlockSpec((None, nh, tm), lambda i: (i // tpb, 0, i % tpb)),
        ),
        scratch_shapes=[pltpu.VMEM((8, SSM_CONV_DIM), F32)] + _gather_scratch(tm),
        compiler_params=_cparams("arbitrary"),
        name="ssd_pre",
    )(dest_t, dest_t, x1, gates, gt2, ys, sh, sc, g.reshape(1, d), wz, wx, wdt, wdtT, conv_w,
      conv_b.reshape(1, -1), dtb, dt_bias.reshape(nh, 1), a_row, a_neg.reshape(nh, 1))


def _expand_heads(v, g, rows):
    lane = lax.broadcasted_iota(I32, (rows, LANES), 1)
    lo = lane < SSM_HEAD_DIM
    parts = []
    for j in range(0, SSM_HPG, 2):
        h0 = g * SSM_HPG + j
        parts.append(jnp.where(lo, v[:, h0:h0 + 1], v[:, h0 + 1:h0 + 2]))
    return jnp.concatenate(parts, axis=1)


def _ssd_scan_kernel(xbc_ref, z_ref, dt_ref, adt_ref, adtT_ref, d_ref, gn_ref, hx_ref, yn_ref, state, *, lc):
    c = pl.program_id(1)

    @pl.when(c == 0)
    def _():
        state[...] = jnp.zeros(state.shape, F32)

    row = lax.broadcasted_iota(I32, (lc, lc), 0)
    col = lax.broadcasted_iota(I32, (lc, lc), 1)
    causal = row >= col
    tri = jnp.where(causal, 1.0, 0.0).astype(BF16)
    triT = jnp.where(row <= col, 1.0, 0.0).astype(BF16)
    ah, am, al = _split3(adt_ref[...])
    a_cum = _dot(tri, ah) + (_dot(tri, am) + _dot(tri, al))
    th, tmid, tl = _split3(adtT_ref[...])
    a_cumT = _dot(th, triT) + (_dot(tmid, triT) + _dot(tl, triT))
    dt = dt_ref[...]
    a_last = a_cum[lc - 1:lc, :]
    e_cum = jnp.exp(a_cum)
    d2e = jnp.exp(a_last - a_cum)
    cdec = jnp.exp(a_last)
    dskip = d_ref[...]
    hx = hx_ref[...]
    dt_x = _dot(dt.astype(BF16), hx)
    dtd2e_x = _dot((dt * d2e).astype(BF16), hx)
    ecum_x = _dot(e_cum.astype(BF16), hx)
    lane2 = lax.broadcasted_iota(I32, (lc, SSM_GROUP_W), 1)
    for g in range(SSM_GROUPS):
        xg = xbc_ref[:, g * SSM_GROUP_W:(g + 1) * SSM_GROUP_W].astype(F32)
        b0 = SSM_INNER + g * SSM_STATE
        c0 = SSM_INNER + SSM_BC + g * SSM_STATE
        bg = xbc_ref[:, b0:b0 + SSM_STATE]
        cg = xbc_ref[:, c0:c0 + SSM_STATE]
        gcols = slice(g * SSM_GROUP_W, (g + 1) * SSM_GROUP_W)
        xdt_b = (xg * dt_x[:, gcols]).astype(BF16)
        cb = _dot_nt(cg, bg)
        y = jnp.zeros((lc, SSM_GROUP_W), F32)
        for j in range(SSM_HPG):
            hd = g * SSM_HPG + j
            seg = a_cum[:, hd:hd + 1] - a_cumT[hd:hd + 1, :]
            decay = jnp.exp(jnp.where(causal, seg, -jnp.inf))
            m = (cb * decay).astype(BF16)
            in_head = (lane2 >= j * SSM_HEAD_DIM) & (lane2 < (j + 1) * SSM_HEAD_DIM)
            y = y + _dot(m, jnp.where(in_head, xdt_b, jnp.zeros_like(xdt_b)))
        st = state[g]
        y = y + _dot(cg, st.astype(BF16)) * ecum_x[:, gcols]
        xd2e = (xg * dtd2e_x[:, gcols]).astype(BF16)
        state[g] = st * _expand_heads(cdec, g, 1) + _dot_tn(bg, xd2e)
        y = y + _expand_heads(dskip, g, 1) * xg
        zg = z_ref[:, g * SSM_GROUP_W:(g + 1) * SSM_GROUP_W].astype(F32)
        yz = y * (zg * _sigmoid(zg))
        yn = _rms(yz) * gn_ref[:, g * SSM_GROUP_W:(g + 1) * SSM_GROUP_W]
        yn_ref[:, g * SSM_GROUP_W:(g + 1) * SSM_GROUP_W] = yn.astype(BF16)


def _ssd_scan(z, xbc, dt, adt, adtT, d_skip, g_norm, bsz, seqlen, lc):
    n = z.shape[0]
    nc = seqlen // lc
    pad = jnp.zeros((LANES - SSM_HEADS,), F32)
    d_row = jnp.concatenate([d_skip, pad]).reshape(1, LANES)
    rows = lambda b, c: (b * nc + c, 0)
    vec = lambda b, c: (0, 0)
    head_of_col = jnp.arange(SSM_INNER, dtype=I32) // SSM_HEAD_DIM
    head_expand = (jnp.arange(LANES, dtype=I32)[:, None] == head_of_col[None, :]).astype(BF16)
    return pl.pallas_call(
        functools.partial(_ssd_scan_kernel, lc=lc),
        out_shape=jax.ShapeDtypeStruct((n, SSM_INNER), BF16),
        grid=(bsz, nc),
        in_specs=[
            pl.BlockSpec((lc, SSM_CONV_DIM), rows),
            pl.BlockSpec((lc, SSM_INNER), rows),
            pl.BlockSpec((lc, LANES), rows),
            pl.BlockSpec((lc, LANES), rows),
            pl.BlockSpec((None, SSM_HEADS, lc), lambda b, c: (b, 0, c)),
            pl.BlockSpec((1, LANES), vec),
            pl.BlockSpec((1, SSM_INNER), vec),
            pl.BlockSpec((LANES, SSM_INNER), vec),
        ],
        out_specs=pl.BlockSpec((lc, SSM_INNER), rows),
        scratch_shapes=[pltpu.VMEM((SSM_GROUPS, SSM_STATE, SSM_GROUP_W), F32)],
        compiler_params=_cparams("arbitrary", "arbitrary"),
        name="ssd_scan",
    )(xbc, z, dt, adt, adtT, d_row, g_norm.reshape(1, -1), head_expand)


def kernel(x, c, positions, w_mod, b_mod, g_mix_norm, g_ffn_norm, mla_w_in, mla_g_q, mla_g_kv, mla_w_q_up, mla_w_kv_up, mla_w_out, ssm_w_in, ssm_conv_w, ssm_conv_b, ssm_dt_bias, ssm_a_log, ssm_d, ssm_g_norm, ssm_w_out, moe_w_router, moe_b_router, moe_w_gate_up, moe_b_gate_up, moe_w_down, moe_b_down, g_final):
    bsz, seqlen, d = x.shape
    depth = w_mod.shape[0]
    n = bsz * seqlen
    tm = min(512, seqlen)
    mod = _modulation(c, w_mod, b_mod)
    mod = mod.reshape(depth, 6, bsz, 1, d)
    pos_f = positions.astype(F32).reshape(n, 1)
    xc = x.reshape(n, d)
    pending = None
    for i in range(depth):
        sh1, sc1, gt1, sh2, sc2, gt2 = [mod[i, j] for j in range(6)]
        j = i // 2
        if i % 2 == 0:
            q, k, v = _mla_pre(xc, pos_f, sh1, sc1, g_mix_norm[i], mla_w_in[j], mla_g_q[j], mla_g_kv[j],
                               mla_w_q_up[j], mla_w_kv_up[j], bsz, seqlen, tm)
            o = _attention(q, k, v, tm).reshape(n, MLA_HEADS * V_HEAD)
            w_out = mla_w_out[j]
        else:
            xc, z, xbc, dt, adt, adtT = _ssd_pre(pending, sh1, sc1, g_mix_norm[i], ssm_w_in[j], ssm_conv_w[j],
                                                 ssm_conv_b[j], ssm_dt_bias[j], ssm_a_log[j], bsz, seqlen,
                                                 SSD_TILE)
            o = _ssd_scan(z, xbc, dt, adt, adtT, ssm_d[j], ssm_g_norm[j], bsz, seqlen, min(256, seqlen))
            w_out = ssm_w_out[j]
        x1, h2, meta, gates, counts = _post_mixer(xc, o, w_out, gt1, sh2, sc2, g_ffn_norm[i],
                                                  moe_w_router[i], moe_b_router[i], seqlen, tm)
        defer = i + 1 < depth and (i + 1) % 2 == 1
        out = _moe(x1, h2, meta, gates, counts, gt2, g_final, moe_w_gate_up, moe_b_gate_up,
                   moe_w_down, moe_b_down, i, seqlen, final=(i == depth - 1), defer_combine=defer)
        if defer:
            pending = out
        else:
            xc = out
    return xc.reshape(bsz, seqlen, d)
```

```python
import functools
import math

import jax
import jax.numpy as jnp
from jax import lax
from jax.experimental import pallas as pl
from jax.experimental.pallas import tpu as pltpu

F32 = jnp.float32
BF16 = jnp.bfloat16
I32 = jnp.int32
U32 = jnp.uint32

NORM_EPS = 1e-6
MLA_HEADS = 8
Q_LORA = 256
KV_LORA = 256
QK_NOPE = 128
QK_ROPE = 64
V_HEAD = 128
QK_HEAD = QK_NOPE + QK_ROPE
ROPE_THETA = 10000.0
MLA_SCALE = QK_HEAD ** -0.5
SSM_HEAD_DIM = 64
SSM_GROUPS = 8
SSM_HPG = 4
SSM_HEADS = SSM_GROUPS * SSM_HPG
SSM_STATE = 128
SSM_CONV = 4
SSM_GROUP_W = SSM_HPG * SSM_HEAD_DIM
SSM_INNER = SSM_GROUPS * SSM_GROUP_W
SSM_BC = SSM_GROUPS * SSM_STATE
SSM_CONV_DIM = SSM_INNER + 2 * SSM_BC
N_EXPERTS = 32
TOP_K = 4
SWIGLU_LIMIT = 7.0
SWIGLU_ALPHA = 1.702
MOE_BLOCK = 256

LANES = 128
VMEM_LIMIT = 56 * 1024 * 1024


def _cparams(*sem):
    return pltpu.CompilerParams(dimension_semantics=tuple(sem), vmem_limit_bytes=VMEM_LIMIT)


def _dot(a, b):
    return jnp.dot(a, b, preferred_element_type=F32)


def _dot_nt(a, b):
    return lax.dot_general(a, b, (((1,), (1,)), ((), ())), preferred_element_type=F32)


def _dot_tn(a, b):
    return lax.dot_general(a, b, (((0,), (0,)), ((), ())), preferred_element_type=F32)


def _split3(a):
    hi = a.astype(BF16)
    r1 = a - hi.astype(F32)
    mid = r1.astype(BF16)
    lo = (r1 - mid.astype(F32)).astype(BF16)
    return hi, mid, lo


def _dot_f32ish(a, b):
    ah, am, _ = _split3(a)
    bh, bm, _ = _split3(b)
    return _dot(ah, bh) + (_dot(ah, bm) + _dot(am, bh))


def _pack_bf16_pair(lo, hi):
    lo_b = lax.bitcast_convert_type(lo.astype(BF16).astype(F32), U32)
    hi_b = lax.bitcast_convert_type(hi.astype(BF16).astype(F32), U32)
    return hi_b | lax.shift_right_logical(lo_b, jnp.uint32(16))


def _unpack_bf16_pair(w):
    lo = lax.bitcast_convert_type(lax.shift_left(w, jnp.uint32(16)), F32)
    hi = lax.bitcast_convert_type(w & jnp.uint32(0xFFFF0000), F32)
    return lo, hi


ROW_CHUNKS = 4


def _chunk_copies(buf2d, rows_hbm, r0, nrows, sem, *, to_hbm):
    cps = []
    for c in range(ROW_CHUNKS):
        v = buf2d.at[:, pl.ds(c * LANES, LANES)]
        h = rows_hbm.at[pl.ds(r0, nrows), c]
        cps.append(pltpu.make_async_copy(v, h, sem) if to_hbm else pltpu.make_async_copy(h, v, sem))
    return cps


def _sigmoid(x):
    return 1.0 / (1.0 + jnp.exp(-x))


def _rms(x):
    return x * lax.rsqrt(jnp.mean(x * x, axis=-1, keepdims=True) + NORM_EPS)


def _mod_kernel(c_ref, w_ref, b_ref, o_ref):
    c = c_ref[...]
    cond = c * _sigmoid(c)
    o_ref[...] = _dot_f32ish(cond, w_ref[...]) + b_ref[...]


def _modulation(c, w_mod, b_mod):
    depth, d, d6 = w_mod.shape
    bsz = c.shape[0]
    nj = d6 // d
    return pl.pallas_call(
        _mod_kernel,
        out_shape=jax.ShapeDtypeStruct((depth, nj, bsz, d), F32),
        grid=(depth, nj),
        in_specs=[
            pl.BlockSpec((bsz, d), lambda l, j: (0, 0)),
            pl.BlockSpec((None, d, d), lambda l, j: (l, 0, j)),
            pl.BlockSpec((None, 1, d), lambda l, j: (l, 0, j)),
        ],
        out_specs=pl.BlockSpec((None, None, bsz, d), lambda l, j: (l, j, 0, 0)),
        compiler_params=_cparams("arbitrary", "arbitrary"),
        name="mod",
    )(c, w_mod, b_mod.reshape(depth, 1, d6))


def _mla_pre_kernel(x_ref, pos_ref, sh_ref, sc_ref, g_ref, win_ref, gq_ref, gkv_ref, wq_ref, wkv_ref,
                    invf_ref, q_ref, k_ref, v_ref):
    x = x_ref[...]
    h = _rms(x) * g_ref[...] * (1.0 + sc_ref[...]) + sh_ref[...]
    lat = _dot(h.astype(BF16), win_ref[...])
    q_lat = _rms(lat[:, :Q_LORA]) * gq_ref[...]
    kv_lat = _rms(lat[:, Q_LORA:Q_LORA + KV_LORA]) * gkv_ref[...]
    ang = pos_ref[...] * invf_ref[...]
    cs = jnp.cos(ang)
    sn = jnp.sin(ang)
    o = Q_LORA + KV_LORA
    k_rope = lat[:, o:o + LANES] * cs + lat[:, o + LANES:o + 2 * LANES] * sn
    qq = _dot(q_lat.astype(BF16), wq_ref[...])
    kv = _dot(kv_lat.astype(BF16), wkv_ref[...])
    rot0 = MLA_HEADS * 2 * LANES
    k_rope_b = k_rope[:, :QK_ROPE].astype(BF16)
    for hd in range(MLA_HEADS):
        c0 = hd * 2 * LANES
        q_nope = qq[:, c0:c0 + LANES] * MLA_SCALE
        q_rope = (qq[:, c0 + LANES:c0 + 2 * LANES] * cs
                  + qq[:, rot0 + hd * LANES:rot0 + (hd + 1) * LANES] * sn) * MLA_SCALE
        q_ref[hd, :, 0:QK_NOPE] = q_nope.astype(BF16)
        q_ref[hd, :, QK_NOPE:QK_HEAD] = q_rope[:, :QK_ROPE].astype(BF16)
        k_ref[hd, :, 0:QK_NOPE] = kv[:, c0:c0 + LANES].astype(BF16)
        k_ref[hd, :, QK_NOPE:QK_HEAD] = k_rope_b
        v_ref[hd] = kv[:, c0 + LANES:c0 + 2 * LANES].astype(BF16)


def _rot_half_cols(w):
    half = QK_ROPE // 2
    return jnp.concatenate([-w[..., half:], w[..., :half]], axis=-1)


def _mla_pre(x2d, pos_f, sh, sc, g, w_in, g_q, g_kv, w_q_up, w_kv_up, bsz, seqlen, tm):
    n, d = x2d.shape
    hh = MLA_HEADS
    o = Q_LORA + KV_LORA
    wr = w_in[:, o:o + QK_ROPE]
    zpad = jnp.zeros((d, LANES - QK_ROPE), F32)
    w_in_ext = jnp.concatenate([w_in[:, :o], wr, zpad, _rot_half_cols(wr), zpad], axis=1).astype(BF16)
    wq = w_q_up.reshape(Q_LORA, hh, QK_HEAD)
    zq = jnp.zeros((Q_LORA, hh, LANES - QK_ROPE), F32)
    wq_main = jnp.concatenate([wq, zq], axis=-1).reshape(Q_LORA, hh * 2 * LANES)
    wq_rot = jnp.concatenate([_rot_half_cols(wq[..., QK_NOPE:]), zq], axis=-1).reshape(Q_LORA, hh * LANES)
    wq_ext = jnp.concatenate([wq_main, wq_rot], axis=1).astype(BF16)
    inv_freq = 1.0 / (ROPE_THETA ** (jnp.arange(0, QK_ROPE, 2, dtype=F32) / QK_ROPE))
    invf = jnp.concatenate([inv_freq, inv_freq, jnp.zeros((LANES - QK_ROPE,), F32)]).reshape(1, LANES)
    tpb = seqlen // tm
    vec = lambda i: (0, 0)
    outs = pl.pallas_call(
        _mla_pre_kernel,
        out_shape=(
            jax.ShapeDtypeStruct((bsz, hh, seqlen, QK_HEAD), BF16),
            jax.ShapeDtypeStruct((bsz, hh, seqlen, QK_HEAD), BF16),
            jax.ShapeDtypeStruct((bsz, hh, seqlen, V_HEAD), BF16),
        ),
        grid=(n // tm,),
        in_specs=[
            pl.BlockSpec((tm, d), lambda i: (i, 0)),
            pl.BlockSpec((tm, 1), lambda i: (i, 0)),
            pl.BlockSpec((None, 1, d), lambda i: (i // tpb, 0, 0)),
            pl.BlockSpec((None, 1, d), lambda i: (i // tpb, 0, 0)),
            pl.BlockSpec((1, d), vec),
            pl.BlockSpec(w_in_ext.shape, vec),
            pl.BlockSpec((1, Q_LORA), vec),
            pl.BlockSpec((1, KV_LORA), vec),
            pl.BlockSpec(wq_ext.shape, vec),
            pl.BlockSpec((KV_LORA, hh * 2 * LANES), vec),
            pl.BlockSpec((1, LANES), vec),
        ],
        out_specs=(
            pl.BlockSpec((None, hh, tm, QK_HEAD), lambda i: (i // tpb, 0, i % tpb, 0)),
            pl.BlockSpec((None, hh, tm, QK_HEAD), lambda i: (i // tpb, 0, i % tpb, 0)),
            pl.BlockSpec((None, hh, tm, V_HEAD), lambda i: (i // tpb, 0, i % tpb, 0)),
        ),
        compiler_params=_cparams("arbitrary"),
        name="mla_pre",
    )(x2d, pos_f, sh, sc, g.reshape(1, d), w_in_ext, g_q.reshape(1, -1), g_kv.reshape(1, -1), wq_ext,
      w_kv_up.astype(BF16), invf)
    return outs


ATTN_HEADS_PER_STEP = 4


def _attn_kernel(q_ref, k_ref, v_ref, o_ref, m_scr, acc_scr, *, tq, hp):
    qi = pl.program_id(2)
    m_scr[...] = jnp.full(m_scr.shape, -jnp.inf, F32)
    acc_scr[...] = jnp.zeros(acc_scr.shape, F32)

    def block(hd, r0, tk, masked):
        k = k_ref[hd, pl.ds(r0, tk), :]
        v_ext = jnp.concatenate([v_ref[hd, pl.ds(r0, tk), :], jnp.ones((tk, V_HEAD), BF16)], axis=1)
        s = _dot_nt(q_ref[hd], k)
        if masked:
            row = lax.broadcasted_iota(I32, (tq, tk), 0)
            col = lax.broadcasted_iota(I32, (tq, tk), 1)
            s = jnp.where(row >= col, s, -jnp.inf)
        m_prev = m_scr[hd]
        m_new = jnp.maximum(m_prev, jnp.max(s, axis=-1, keepdims=True))
        alpha = jnp.exp(m_prev - m_new)
        p = jnp.exp(s - m_new).astype(BF16)
        acc_scr[hd] = alpha * acc_scr[hd] + _dot(p, v_ext)
        m_scr[hd] = m_new

    def body(j, carry):
        for hd in range(hp):
            block(hd, pl.multiple_of(j * (2 * tq), 2 * tq), 2 * tq, False)
        return carry

    lax.fori_loop(0, qi // 2, body, 0)

    @pl.when(qi % 2 == 1)
    def _():
        for hd in range(hp):
            block(hd, pl.multiple_of((qi - 1) * tq, tq), tq, False)

    for hd in range(hp):
        block(hd, pl.multiple_of(qi * tq, tq), tq, True)
    for hd in range(hp):
        acc = acc_scr[hd]
        o_ref[:, hd * V_HEAD:(hd + 1) * V_HEAD] = (acc[:, :V_HEAD] / acc[:, V_HEAD:]).astype(o_ref.dtype)


def _attention(q, k, v, tq):
    bsz, hh, seqlen, _ = q.shape
    hp = ATTN_HEADS_PER_STEP
    return pl.pallas_call(
        functools.partial(_attn_kernel, tq=tq, hp=hp),
        out_shape=jax.ShapeDtypeStruct((bsz, seqlen, hh * V_HEAD), BF16),
        grid=(bsz, hh // hp, seqlen // tq),
        in_specs=[
            pl.BlockSpec((None, hp, tq, QK_HEAD), lambda b, h, i: (b, h, i, 0)),
            pl.BlockSpec((None, hp, seqlen, QK_HEAD), lambda b, h, i: (b, h, 0, 0)),
            pl.BlockSpec((None, hp, seqlen, V_HEAD), lambda b, h, i: (b, h, 0, 0)),
        ],
        out_specs=pl.BlockSpec((None, tq, hp * V_HEAD), lambda b, h, i: (b, i, h)),
        scratch_shapes=[
            pltpu.VMEM((hp, tq, 1), F32),
            pltpu.VMEM((hp, tq, 2 * V_HEAD), F32),
        ],
        compiler_params=_cparams("arbitrary", "arbitrary", "arbitrary"),
        name="attn",
    )(q, k, v)


def _post_mixer_kernel(x_ref, o_ref, wout_ref, gt1_ref, sh_ref, sc_ref, g_ref, wr_ref, br_ref,
                       x1_ref, h2_hbm, meta_ref, gate_ref, cnt_ref, carry_scr, pbuf, psem, *, tm):
    i = pl.program_id(0)

    @pl.when(i == 0)
    def _():
        carry_scr[...] = jnp.zeros(carry_scr.shape, F32)

    y = _dot(o_ref[...], wout_ref[...])
    x1 = x_ref[...] + gt1_ref[...] * y
    x1_ref[...] = x1
    h2 = _rms(x1) * g_ref[...] * (1.0 + sc_ref[...]) + sh_ref[...]
    half = h2.shape[1] // 2

    @pl.when(i > 0)
    def _():
        for cp in _chunk_copies(pbuf, h2_hbm, (i - 1) * tm, tm, psem, to_hbm=True):
            cp.wait()

    pbuf[...] = _pack_bf16_pair(h2[:, :half], h2[:, half:])
    for cp in _chunk_copies(pbuf, h2_hbm, i * tm, tm, psem, to_hbm=True):
        cp.start()

    logits = _dot_f32ish(h2, wr_ref[...]) + br_ref[...]
    lane = lax.broadcasted_iota(I32, (tm, LANES), 1).astype(F32)
    work = logits
    idxs, vals = [], []
    for _ in range(TOP_K):
        mx = jnp.max(work, axis=-1, keepdims=True)
        idx = jnp.min(jnp.where(work == mx, lane, float(LANES)), axis=-1, keepdims=True)
        idxs.append(idx)
        vals.append(mx)
        work = jnp.where(lane == idx, -jnp.inf, work)
    exps = [jnp.exp(vk - vals[0]) for vk in vals]
    denom = exps[0] + exps[1] + exps[2] + exps[3]
    onehot = jnp.zeros((tm, LANES), F32)
    for idx in idxs:
        onehot = onehot + jnp.where(lane == idx, 1.0, 0.0)
    row = lax.broadcasted_iota(I32, (tm, tm), 0)
    col = lax.broadcasted_iota(I32, (tm, tm), 1)
    ltri = jnp.where(row > col, 1.0, 0.0).astype(BF16)
    cum = _dot(ltri, onehot.astype(BF16)) + carry_scr[...]
    meta = jnp.zeros((tm, LANES), F32)
    gates = jnp.zeros((tm, LANES), F32)
    for kk in range(TOP_K):
        rank = jnp.sum(jnp.where(lane == idxs[kk], cum, 0.0), axis=-1, keepdims=True)
        meta = jnp.where(lane == float(kk), idxs[kk], meta)
        meta = jnp.where(lane == float(TOP_K + kk), rank, meta)
        gates = jnp.where(lane == float(kk), exps[kk] / denom, gates)
    meta_ref[...] = jnp.transpose(meta)[:2 * TOP_K, :].astype(I32)
    gate_ref[...] = gates
    carry = carry_scr[...] + jnp.sum(onehot, axis=0, keepdims=True)
    carry_scr[...] = carry
    cnt_ref[...] = carry.astype(I32)

    @pl.when(i == pl.num_programs(0) - 1)
    def _():
        for cp in _chunk_copies(pbuf, h2_hbm, i * tm, tm, psem, to_hbm=True):
            cp.wait()


def _post_mixer(x2d, o2d, w_out, gt1, sh2, sc2, g_ffn, w_router, b_router, seqlen, tm):
    n, d = x2d.shape
    kdim = o2d.shape[1]
    tpb = seqlen // tm
    wr = jnp.concatenate([w_router, jnp.zeros((d, LANES - N_EXPERTS), F32)], axis=1)
    br = jnp.concatenate([b_router, jnp.full((LANES - N_EXPERTS,), -1e30, F32)]).reshape(1, LANES)
    vec = lambda i: (0, 0)
    bvec = lambda i: (i // tpb, 0, 0)
    return pl.pallas_call(
        functools.partial(_post_mixer_kernel, tm=tm),
        out_shape=(
            jax.ShapeDtypeStruct((n, d), F32),
            jax.ShapeDtypeStruct((n, ROW_CHUNKS, LANES), U32),
            jax.ShapeDtypeStruct((2 * TOP_K, n), I32),
            jax.ShapeDtypeStruct((n, LANES), F32),
            jax.ShapeDtypeStruct((1, LANES), I32),
        ),
        grid=(n // tm,),
        in_specs=[
            pl.BlockSpec((tm, d), lambda i: (i, 0)),
            pl.BlockSpec((tm, kdim), lambda i: (i, 0)),
            pl.BlockSpec((kdim, d), vec),
            pl.BlockSpec((None, 1, d), bvec),
            pl.BlockSpec((None, 1, d), bvec),
            pl.BlockSpec((None, 1, d), bvec),
            pl.BlockSpec((1, d), vec),
            pl.BlockSpec((d, LANES), vec),
            pl.BlockSpec((1, LANES), vec),
        ],
        out_specs=(
            pl.BlockSpec((tm, d), lambda i: (i, 0)),
            pl.BlockSpec(memory_space=pl.ANY),
            pl.BlockSpec((2 * TOP_K, tm), lambda i: (0, i)),
            pl.BlockSpec((tm, LANES), lambda i: (i, 0)),
            pl.BlockSpec((1, LANES), vec),
        ),
        scratch_shapes=[pltpu.VMEM((1, LANES), F32), pltpu.VMEM((tm, d // 2), U32), pltpu.SemaphoreType.DMA],
        compiler_params=_cparams("arbitrary"),
        name="post_mixer",
    )(x2d, o2d, w_out.astype(BF16), gt1, sh2, sc2, g_ffn.reshape(1, d), wr, br)


ROW_UNROLL = 8


def _row_copy(src, s, dst, t, sem):
    return pltpu.make_async_copy(src.at[s], dst.at[t], sem)


def _dest_kernel(pstart_ref, meta_ref, dest_ref):
    e = meta_ref[0:TOP_K, :]
    dest = meta_ref[TOP_K:2 * TOP_K, :]
    for j in range(N_EXPERTS):
        dest = dest + jnp.where(e == j, pstart_ref[j], 0)
    dest_ref[...] = dest


def _dest_rows(meta_t, pstart):
    n = meta_t.shape[1]
    tn = min(4096, n)
    grid_spec = pltpu.PrefetchScalarGridSpec(
        num_scalar_prefetch=1,
        grid=(n // tn,),
        in_specs=[pl.BlockSpec((2 * TOP_K, tn), lambda i, ps: (0, i))],
        out_specs=pl.BlockSpec((TOP_K, tn), lambda i, ps: (0, i)),
    )
    return pl.pallas_call(
        _dest_kernel,
        out_shape=jax.ShapeDtypeStruct((TOP_K, n), I32),
        grid_spec=grid_spec,
        compiler_params=_cparams("arbitrary"),
        name="dest_rows",
    )(pstart, meta_t)


def _dispatch_kernel(pstart_ref, cnt_ref, dest_ref, h_ref, xs_hbm, zbuf, sem, zsem, *, tm):
    i = pl.program_id(0)

    @pl.when(i == 0)
    def _():
        zbuf[...] = jnp.zeros(zbuf.shape, U32)

        def per_expert(e, carry):
            lo = pstart_ref[e] + cnt_ref[e]
            hi = pstart_ref[e + 1]

            def start(j, c):
                _row_copy(zbuf, 0, xs_hbm, j, zsem).start()
                return c

            def wait(j, c):
                _row_copy(zbuf, 0, xs_hbm, j, zsem).wait()
                return c

            lax.fori_loop(lo, hi, start, 0)
            lax.fori_loop(lo, hi, wait, 0)
            return carry

        lax.fori_loop(0, N_EXPERTS, per_expert, 0)

    def start(tt, c):
        for u in range(ROW_UNROLL):
            t = tt * ROW_UNROLL + u
            for kk in range(TOP_K):
                _row_copy(h_ref, t, xs_hbm, dest_ref[kk, t], sem).start(priority=kk % 2)
        return c

    def wait(tt, c):
        for u in range(ROW_UNROLL * TOP_K):
            _row_copy(h_ref, 0, xs_hbm, 0, sem).wait()
        return c

    lax.fori_loop(0, tm // ROW_UNROLL, start, 0)
    lax.fori_loop(0, tm // ROW_UNROLL, wait, 0)


def _dispatch(h2p, dest_t, pstart, counts, n_rows, tm):
    n = h2p.shape[0]
    grid_spec = pltpu.PrefetchScalarGridSpec(
        num_scalar_prefetch=2,
        grid=(n // tm,),
        in_specs=[
            pl.BlockSpec((TOP_K, tm), lambda i, ps, cn: (0, i), memory_space=pltpu.SMEM),
            pl.BlockSpec((tm, ROW_CHUNKS, LANES), lambda i, ps, cn: (i, 0, 0)),
        ],
        out_specs=pl.BlockSpec(memory_space=pl.ANY),
        scratch_shapes=[
            pltpu.VMEM((1, ROW_CHUNKS, LANES), U32),
            pltpu.SemaphoreType.DMA,
            pltpu.SemaphoreType.DMA,
        ],
    )
    return pl.pallas_call(
        functools.partial(_dispatch_kernel, tm=tm),
        out_shape=jax.ShapeDtypeStruct((n_rows, ROW_CHUNKS, LANES), U32),
        grid_spec=grid_spec,
        compiler_params=pltpu.CompilerParams(dimension_semantics=("arbitrary",), vmem_limit_bytes=VMEM_LIMIT,
                                             has_side_effects=True),
        name="dispatch",
    )(pstart, counts, dest_t, h2p)


def _ffn_kernel(be_ref, nu_ref, first_ref, wslot_ref, nxt_ref, xs_hbm, wgu_hbm, bgu_ref, wd_hbm, bd_ref, ys_hbm,
                wgu_b, wd_b, wgu_f, wd_f, xbuf, ybuf, xsem, ysem, wsem, *, ff, layer):
    i = pl.program_id(0)
    nu = nu_ref[0]
    slot = i % 2

    def w_copies(e, s):
        return [pltpu.make_async_copy(wgu_hbm.at[layer, e], wgu_f.at[s], wsem.at[s]),
                pltpu.make_async_copy(wd_hbm.at[layer, e], wd_f.at[s], wsem.at[s])]

    @pl.when(i == 0)
    def _():
        for cp in w_copies(be_ref[0], 0):
            cp.start()

    def x_copies(step, s):
        return _chunk_copies(xbuf.at[s], xs_hbm, step * MOE_BLOCK, MOE_BLOCK, xsem.at[s], to_hbm=False)

    def y_copies(step, s):
        return _chunk_copies(ybuf.at[s], ys_hbm, step * MOE_BLOCK, MOE_BLOCK, ysem.at[s], to_hbm=True)

    @pl.when(i == 0)
    def _():
        for cp in x_copies(0, 0):
            cp.start()

    @pl.when(i + 1 < nu)
    def _():
        for cp in x_copies(i + 1, 1 - slot):
            cp.start()

    @pl.when(first_ref[i] == 1)
    def _():
        ws = wslot_ref[i]
        for cp in w_copies(be_ref[i], ws):
            cp.wait()

        @pl.when(nxt_ref[i] >= 0)
        def _():
            for cp in w_copies(nxt_ref[i], 1 - ws):
                cp.start(priority=1)

        wgu_b[...] = wgu_f[ws].astype(BF16)
        wd_b[...] = wd_f[ws].astype(BF16)

    @pl.when(i < nu)
    def _():
        for cp in x_copies(i, slot):
            cp.wait()
        x_lo, x_hi = _unpack_bf16_pair(xbuf[slot])
        dh = x_lo.shape[1]
        gu = (_dot(x_lo.astype(BF16), wgu_b[0:dh, :]) + _dot(x_hi.astype(BF16), wgu_b[dh:2 * dh, :])
              + bgu_ref[...])
        g = jnp.minimum(gu[:, :ff], SWIGLU_LIMIT)
        lin = jnp.clip(gu[:, ff:], -SWIGLU_LIMIT, SWIGLU_LIMIT)
        act = g * _sigmoid(SWIGLU_ALPHA * g) * (lin + 1.0)
        y = _dot(act.astype(BF16), wd_b[...]) + bd_ref[...]

        @pl.when(i >= 2)
        def _():
            for cp in y_copies(i - 2, slot):
                cp.wait()

        ybuf[slot] = _pack_bf16_pair(y[:, :dh], y[:, dh:])
        for cp in y_copies(i, slot):
            cp.start()

    @pl.when(i == nu - 1)
    def _():
        for cp in y_copies(i, slot):
            cp.wait()

        @pl.when(i >= 1)
        def _():
            for cp in y_copies(i - 1, 1 - slot):
                cp.wait()


def _ffn(xs, blk_e, n_used, pad_end, w_gate_up, b_gate_up, w_down, b_down, layer):
    n_rows = xs.shape[0]
    depth, ne, d, ff2 = w_gate_up.shape
    ff = ff2 // 2
    n_blk = n_rows // MOE_BLOCK
    blk = jnp.arange(n_blk, dtype=I32)
    active = blk < n_used[0]
    first = (active & ((blk == 0) | (blk_e != jnp.roll(blk_e, 1)))).astype(I32)
    wslot = ((jnp.cumsum(first) - 1) % 2).astype(I32)
    nxt_blk = pad_end[blk_e] // MOE_BLOCK
    nxt = jnp.where(nxt_blk < n_used[0], blk_e[jnp.minimum(nxt_blk, n_blk - 1)], -1).astype(I32)
    row_buf = pltpu.VMEM((2, MOE_BLOCK, ROW_CHUNKS * LANES), U32)
    sp = lambda i, be, nu, fi, ws, nx: (layer, be[i], 0, 0)
    grid_spec = pltpu.PrefetchScalarGridSpec(
        num_scalar_prefetch=5,
        grid=(n_blk,),
        in_specs=[
            pl.BlockSpec(memory_space=pl.ANY),
            pl.BlockSpec(memory_space=pl.ANY),
            pl.BlockSpec((None, None, 1, ff2), sp),
            pl.BlockSpec(memory_space=pl.ANY),
            pl.BlockSpec((None, None, 1, d), sp),
        ],
        out_specs=pl.BlockSpec(memory_space=pl.ANY),
        scratch_shapes=[pltpu.VMEM((d, ff2), BF16), pltpu.VMEM((ff, d), BF16),
                        pltpu.VMEM((2, d, ff2), F32), pltpu.VMEM((2, ff, d), F32), row_buf, row_buf,
                        pltpu.SemaphoreType.DMA((2,)), pltpu.SemaphoreType.DMA((2,)),
                        pltpu.SemaphoreType.DMA((2,))],
    )
    return pl.pallas_call(
        functools.partial(_ffn_kernel, ff=ff, layer=layer),
        out_shape=jax.ShapeDtypeStruct((n_rows, ROW_CHUNKS, LANES), U32),
        grid_spec=grid_spec,
        compiler_params=_cparams("arbitrary"),
        name="ffn",
    )(blk_e, n_used, first, wslot, nxt, xs, w_gate_up, b_gate_up.reshape(depth, ne, 1, ff2), w_down,
      b_down.reshape(depth, ne, 1, d))


def _gather_scratch(tm):
    return [pltpu.VMEM((2, TOP_K, tm // ROW_UNROLL, ROW_CHUNKS, ROW_UNROLL, LANES), U32),
            pltpu.SemaphoreType.DMA((2,))]


def _gather_start(dest_ref, ys_hbm, buf, sem, tm):
    def body(tt, c):
        for u in range(ROW_UNROLL):
            for kk in range(TOP_K):
                src = ys_hbm.at[dest_ref[kk, tt * ROW_UNROLL + u]]
                pltpu.make_async_copy(src, buf.at[kk, tt, :, u], sem).start(priority=kk % 2)
        return c

    lax.fori_loop(0, tm // ROW_UNROLL, body, 0)


def _gather_wait(ys_hbm, buf, sem, tm):
    def body(tt, c):
        for u in range(ROW_UNROLL * TOP_K):
            pltpu.make_async_copy(ys_hbm.at[0], buf.at[0, 0, :, 0], sem).wait()
        return c

    lax.fori_loop(0, tm // ROW_UNROLL, body, 0)


def _combine_rows(dest0_ref, dest_next_ref, gate_ref, ys_hbm, buf, sem, tm):
    i = pl.program_id(0)
    slot = i % 2

    @pl.when(i == 0)
    def _():
        _gather_start(dest0_ref, ys_hbm, buf.at[0], sem.at[0], tm)

    @pl.when(i + 1 < pl.num_programs(0))
    def _():
        _gather_start(dest_next_ref, ys_hbm, buf.at[1 - slot], sem.at[1 - slot], tm)

    _gather_wait(ys_hbm, buf.at[slot], sem.at[slot], tm)
    gates = gate_ref[...]
    acc_lo = acc_hi = None
    for kk in range(TOP_K):
        rows = jnp.concatenate([buf[slot, kk, :, c, :, :].reshape(tm, LANES) for c in range(ROW_CHUNKS)], axis=1)
        lo, hi = _unpack_bf16_pair(rows)
        gk = gates[:, kk:kk + 1]
        acc_lo = gk * lo if acc_lo is None else acc_lo + gk * lo
        acc_hi = gk * hi if acc_hi is None else acc_hi + gk * hi
    return jnp.concatenate([acc_lo, acc_hi], axis=1)


def _combine_specs(tm, nsteps, idx):
    return [pl.BlockSpec((TOP_K, tm), lambda *a: (0, 0), memory_space=pltpu.SMEM),
            pl.BlockSpec((TOP_K, tm), lambda *a: (0, jnp.minimum(idx(*a) + 1, nsteps - 1)), memory_space=pltpu.SMEM)]


def _combine_kernel(dest0_ref, dest_next_ref, x1_ref, gate_ref, gt2_ref, gfin_ref, ys_hbm, o_ref, buf, sem,
                    *, tm, final):
    y = _combine_rows(dest0_ref, dest_next_ref, gate_ref, ys_hbm, buf, sem, tm)
    x2 = x1_ref[...] + gt2_ref[...] * y
    if final:
        x2 = _rms(x2) * gfin_ref[...]
    o_ref[...] = x2


def _combine(x1, gates, gt2, g_final, ys, dest_t, seqlen, tm, final):
    n, d = x1.shape
    tpb = seqlen // tm
    grid_spec = pltpu.PrefetchScalarGridSpec(
        num_scalar_prefetch=0,
        grid=(n // tm,),
        in_specs=_combine_specs(tm, n // tm, lambda i: i) + [
            pl.BlockSpec((tm, d), lambda i: (i, 0)),
            pl.BlockSpec((tm, LANES), lambda i: (i, 0)),
            pl.BlockSpec((None, 1, d), lambda i: (i // tpb, 0, 0)),
            pl.BlockSpec((1, d), lambda i: (0, 0)),
            pl.BlockSpec(memory_space=pl.ANY),
        ],
        out_specs=pl.BlockSpec((tm, d), lambda i: (i, 0)),
        scratch_shapes=_gather_scratch(tm),
    )
    return pl.pallas_call(
        functools.partial(_combine_kernel, tm=tm, final=final),
        out_shape=jax.ShapeDtypeStruct((n, d), F32),
        grid_spec=grid_spec,
        compiler_params=_cparams("arbitrary"),
        name="combine",
    )(dest_t, dest_t, x1, gates, gt2, g_final.reshape(1, d), ys)


def _moe(x1, h2p, meta_t, gates, counts, gt2, g_final, w_gate_up, b_gate_up, w_down, b_down, layer, seqlen,
         final, defer_combine):
    n, d = x1.shape
    n_pair = n * TOP_K
    n_rows = -(-n_pair // MOE_BLOCK) * MOE_BLOCK + N_EXPERTS * MOE_BLOCK
    n_blk = n_rows // MOE_BLOCK
    cnt = counts[0, :N_EXPERTS]
    padded = (cnt + MOE_BLOCK - 1) // MOE_BLOCK * MOE_BLOCK
    pad_end = jnp.cumsum(padded)
    pstart = jnp.concatenate([pad_end - padded, pad_end[-1:]]).astype(I32)
    n_used = (pad_end[-1:] // MOE_BLOCK).astype(I32)
    blk_start = jnp.arange(n_blk, dtype=I32) * MOE_BLOCK
    blk_e = jnp.minimum(jnp.sum(blk_start[:, None] >= pad_end[None, :], axis=1), N_EXPERTS - 1).astype(I32)
    tm = min(256, seqlen)
    dest_t = _dest_rows(meta_t, pstart)
    xs = _dispatch(h2p, dest_t, pstart, cnt, n_rows, tm)
    ys = _ffn(xs, blk_e, n_used, pad_end.astype(I32), w_gate_up, b_gate_up, w_down, b_down, layer)
    if defer_combine:
        return x1, gates, gt2, ys, dest_t
    return _combine(x1, gates, gt2, g_final, ys, dest_t, seqlen, tm, final)


def _softplus(x):
    return jnp.maximum(x, 0.0) + jnp.log(1.0 + jnp.exp(-jnp.abs(x)))


SSD_TILE = 256


def _ssd_pre_kernel(dest0_ref, dest_next_ref, x1_ref, gate_ref, gt2_ref, ys_hbm,
                    sh_ref, sc_ref, g_ref, wz_ref, wx_ref, wdt_ref, wdtT_ref, cw_ref, cb_ref,
                    dtb_ref, dtbT_ref, a_ref, aT_ref,
                    x_out_ref, z_ref, xbc_ref, dt_ref, adt_ref, adtT_ref, ubuf, gbuf, gsem, *, tm, tpb, cchunk):
    i = pl.program_id(0)
    x = x1_ref[...] + gt2_ref[...] * _combine_rows(dest0_ref, dest_next_ref, gate_ref, ys_hbm, gbuf, gsem, tm)
    x_out_ref[...] = x
    h = _rms(x) * g_ref[...] * (1.0 + sc_ref[...]) + sh_ref[...]
    hb = h.astype(BF16)
    z_ref[...] = _dot(hb, wz_ref[...]).astype(BF16)
    dt = _softplus(_dot(hb, wdt_ref[...]) + dtb_ref[...])
    dt_ref[...] = dt
    adt_ref[...] = dt * a_ref[...]
    dtT = _softplus(_dot_nt(wdtT_ref[...], hb) + dtbT_ref[...])
    adtT_ref[...] = dtT * aT_ref[...]

    @pl.when(i % tpb == 0)
    def _():
        ubuf[...] = jnp.zeros(ubuf.shape, F32)

    for c0 in range(0, SSM_CONV_DIM, cchunk):
        cols = slice(c0, c0 + cchunk)
        u = _dot(hb, wx_ref[:, cols])
        ext = jnp.concatenate([ubuf[:, cols], u], axis=0)
        acc = cb_ref[:, cols] + cw_ref[SSM_CONV - 1:SSM_CONV, cols] * u
        for j in range(1, SSM_CONV):
            shifted = pltpu.roll(ext, j, axis=0)[8:, :]
            acc = acc + cw_ref[SSM_CONV - 1 - j:SSM_CONV - j, cols] * shifted
        xbc_ref[:, cols] = (acc * _sigmoid(acc)).astype(BF16)
        ubuf[:, cols] = u[tm - 8:, :]


def _ssd_pre(moe_out, sh, sc, g, w_in, conv_w, conv_b, dt_bias, a_log, bsz, seqlen, tm):
    x1, gates, gt2, ys, dest_t = moe_out
    n, d = x1.shape
    tm = min(tm, seqlen)
    tpb = seqlen // tm
    nh = SSM_HEADS
    wz = w_in[:, :SSM_INNER].astype(BF16)
    wx = w_in[:, SSM_INNER:SSM_INNER + SSM_CONV_DIM].astype(BF16)
    wdt_raw = w_in[:, SSM_INNER + SSM_CONV_DIM:]
    wdt = jnp.concatenate([wdt_raw, jnp.zeros((d, LANES - nh), F32)], axis=1).astype(BF16)
    wdtT = wdt_raw.T.astype(BF16)
    pad = jnp.zeros((LANES - nh,), F32)
    dtb = jnp.concatenate([dt_bias, pad]).reshape(1, LANES)
    a_neg = -jnp.exp(a_log.astype(F32))
    a_row = jnp.concatenate([a_neg, pad]).reshape(1, LANES)
    vec = lambda i: (0, 0)
    bvec = lambda i: (i // tpb, 0, 0)
    return pl.pallas_call(
        functools.partial(_ssd_pre_kernel, tm=tm, tpb=tpb, cchunk=1024),
        out_shape=(
            jax.ShapeDtypeStruct((n, d), F32),
            jax.ShapeDtypeStruct((n, SSM_INNER), BF16),
            jax.ShapeDtypeStruct((n, SSM_CONV_DIM), BF16),
            jax.ShapeDtypeStruct((n, LANES), F32),
            jax.ShapeDtypeStruct((n, LANES), F32),
            jax.ShapeDtypeStruct((bsz, nh, seqlen), F32),
        ),
        grid=(n // tm,),
        in_specs=_combine_specs(tm, n // tm, lambda i: i) + [
            pl.BlockSpec((tm, d), lambda i: (i, 0)),
            pl.BlockSpec((tm, LANES), lambda i: (i, 0)),
            pl.BlockSpec((None, 1, d), bvec),
            pl.BlockSpec(memory_space=pl.ANY),
            pl.BlockSpec((None, 1, d), bvec),
            pl.BlockSpec((None, 1, d), bvec),
            pl.BlockSpec((1, d), vec),
            pl.BlockSpec((d, SSM_INNER), vec),
            pl.BlockSpec((d, SSM_CONV_DIM), vec),
            pl.BlockSpec((d, LANES), vec),
            pl.BlockSpec((nh, d), vec),
            pl.BlockSpec((SSM_CONV, SSM_CONV_DIM), vec),
            pl.BlockSpec((1, SSM_CONV_DIM), vec),
            pl.BlockSpec((1, LANES), vec),
            pl.BlockSpec((nh, 1), vec),
            pl.BlockSpec((1, LANES), vec),
            pl.BlockSpec((nh, 1), vec),
        ],
        out_specs=(
            pl.BlockSpec((tm, d), lambda i: (i, 0)),
            pl.BlockSpec((tm, SSM_INNER), lambda i: (i, 0)),
            pl.BlockSpec((tm, SSM_CONV_DIM), lambda i: (i, 0)),
            pl.BlockSpec((tm, LANES), lambda i: (i, 0)),
            pl.BlockSpec((tm, LANES), lambda i: (i, 0)),
            pl.BlockSpec((None, nh, tm), lambda i: (i // tpb, 0, i % tpb)),
        ),
        scratch_shapes=[pltpu.VMEM((8, SSM_CONV_DIM), F32)] + _gather_scratch(tm),
        compiler_params=_cparams("arbitrary"),
        name="ssd_pre",
    )(dest_t, dest_t, x1, gates, gt2, ys, sh, sc, g.reshape(1, d), wz, wx, wdt, wdtT, conv_w,
      conv_b.reshape(1, -1), dtb, dt_bias.reshape(nh, 1), a_row, a_neg.reshape(nh, 1))


def _expand_heads(v, g, rows):
    lane = lax.broadcasted_iota(I32, (rows, LANES), 1)
    lo = lane < SSM_HEAD_DIM
    parts = []
    for j in range(0, SSM_HPG, 2):
        h0 = g * SSM_HPG + j
        parts.append(jnp.where(lo, v[:, h0:h0 + 1], v[:, h0 + 1:h0 + 2]))
    return jnp.concatenate(parts, axis=1)


def _ssd_scan_kernel(xbc_ref, z_ref, dt_ref, adt_ref, adtT_ref, d_ref, gn_ref, hx_ref, yn_ref, state, *, lc):
    c = pl.program_id(1)

    @pl.when(c == 0)
    def _():
        state[...] = jnp.zeros(state.shape, F32)

    row = lax.broadcasted_iota(I32, (lc, lc), 0)
    col = lax.broadcasted_iota(I32, (lc, lc), 1)
    causal = row >= col
    tri = jnp.where(causal, 1.0, 0.0).astype(BF16)
    triT = jnp.where(row <= col, 1.0, 0.0).astype(BF16)
    ah, am, al = _split3(adt_ref[...])
    a_cum = _dot(tri, ah) + (_dot(tri, am) + _dot(tri, al))
    th, tmid, tl = _split3(adtT_ref[...])
    a_cumT = _dot(th, triT) + (_dot(tmid, triT) + _dot(tl, triT))
    dt = dt_ref[...]
    a_last = a_cum[lc - 1:lc, :]
    e_cum = jnp.exp(a_cum)
    d2e = jnp.exp(a_last - a_cum)
    cdec = jnp.exp(a_last)
    dskip = d_ref[...]
    hx = hx_ref[...]
    dt_x = _dot(dt.astype(BF16), hx)
    dtd2e_x = _dot((dt * d2e).astype(BF16), hx)
    ecum_x = _dot(e_cum.astype(BF16), hx)
    lane2 = lax.broadcasted_iota(I32, (lc, SSM_GROUP_W), 1)
    for g in range(SSM_GROUPS):
        xg = xbc_ref[:, g * SSM_GROUP_W:(g + 1) * SSM_GROUP_W].astype(F32)
        b0 = SSM_INNER + g * SSM_STATE
        c0 = SSM_INNER + SSM_BC + g * SSM_STATE
        bg = xbc_ref[:, b0:b0 + SSM_STATE]
        cg = xbc_ref[:, c0:c0 + SSM_STATE]
        gcols = slice(g * SSM_GROUP_W, (g + 1) * SSM_GROUP_W)
        xdt_b = (xg * dt_x[:, gcols]).astype(BF16)
        cb = _dot_nt(cg, bg)
        y = jnp.zeros((lc, SSM_GROUP_W), F32)
        for j in range(SSM_HPG):
            hd = g * SSM_HPG + j
            seg = a_cum[:, hd:hd + 1] - a_cumT[hd:hd + 1, :]
            decay = jnp.exp(jnp.where(causal, seg, -jnp.inf))
            m = (cb * decay).astype(BF16)
            in_head = (lane2 >= j * SSM_HEAD_DIM) & (lane2 < (j + 1) * SSM_HEAD_DIM)
            y = y + _dot(m, jnp.where(in_head, xdt_b, jnp.zeros_like(xdt_b)))
        st = state[g]
        y = y + _dot(cg, st.astype(BF16)) * ecum_x[:, gcols]
        xd2e = (xg * dtd2e_x[:, gcols]).astype(BF16)
        state[g] = st * _expand_heads(cdec, g, 1) + _dot_tn(bg, xd2e)
        y = y + _expand_heads(dskip, g, 1) * xg
        zg = z_ref[:, g * SSM_GROUP_W:(g + 1) * SSM_GROUP_W].astype(F32)
        yz = y * (zg * _sigmoid(zg))
        yn = _rms(yz) * gn_ref[:, g * SSM_GROUP_W:(g + 1) * SSM_GROUP_W]
        yn_ref[:, g * SSM_GROUP_W:(g + 1) * SSM_GROUP_W] = yn.astype(BF16)


def _ssd_scan(z, xbc, dt, adt, adtT, d_skip, g_norm, bsz, seqlen, lc):
    n = z.shape[0]
    nc = seqlen // lc
    pad = jnp.zeros((LANES - SSM_HEADS,), F32)
    d_row = jnp.concatenate([d_skip, pad]).reshape(1, LANES)
    rows = lambda b, c: (b * nc + c, 0)
    vec = lambda b, c: (0, 0)
    head_of_col = jnp.arange(SSM_INNER, dtype=I32) // SSM_HEAD_DIM
    head_expand = (jnp.arange(LANES, dtype=I32)[:, None] == head_of_col[None, :]).astype(BF16)
    return pl.pallas_call(
        functools.partial(_ssd_scan_kernel, lc=lc),
        out_shape=jax.ShapeDtypeStruct((n, SSM_INNER), BF16),
        grid=(bsz, nc),
        in_specs=[
            pl.BlockSpec((lc, SSM_CONV_DIM), rows),
            pl.BlockSpec((lc, SSM_INNER), rows),
            pl.BlockSpec((lc, LANES), rows),
            pl.BlockSpec((lc, LANES), rows),
            pl.BlockSpec((None, SSM_HEADS, lc), lambda b, c: (b, 0, c)),
            pl.BlockSpec((1, LANES), vec),
            pl.BlockSpec((1, SSM_INNER), vec),
            pl.BlockSpec((LANES, SSM_INNER), vec),
        ],
        out_specs=pl.BlockSpec((lc, SSM_INNER), rows),
        scratch_shapes=[pltpu.VMEM((SSM_GROUPS, SSM_STATE, SSM_GROUP_W), F32)],
        compiler_params=_cparams("arbitrary", "arbitrary"),
        name="ssd_scan",
    )(xbc, z, dt, adt, adtT, d_row, g_norm.reshape(1, -1), head_expand)


def kernel(x, c, positions, w_mod, b_mod, g_mix_norm, g_ffn_norm, mla_w_in, mla_g_q, mla_g_kv, mla_w_q_up, mla_w_kv_up, mla_w_out, ssm_w_in, ssm_conv_w, ssm_conv_b, ssm_dt_bias, ssm_a_log, ssm_d, ssm_g_norm, ssm_w_out, moe_w_router, moe_b_router, moe_w_gate_up, moe_b_gate_up, moe_w_down, moe_b_down, g_final):
    bsz, seqlen, d = x.shape
    depth = w_mod.shape[0]
    n = bsz * seqlen
    tm = min(512, seqlen)
    mod = _modulation(c, w_mod, b_mod)
    mod = mod.reshape(depth, 6, bsz, 1, d)
    pos_f = positions.astype(F32).reshape(n, 1)
    xc = x.reshape(n, d)
    pending = None
    for i in range(depth):
        sh1, sc1, gt1, sh2, sc2, gt2 = [mod[i, j] for j in range(6)]
        j = i // 2
        if i % 2 == 0:
            q, k, v = _mla_pre(xc, pos_f, sh1, sc1, g_mix_norm[i], mla_w_in[j], mla_g_q[j], mla_g_kv[j],
                               mla_w_q_up[j], mla_w_kv_up[j], bsz, seqlen, tm)
            o = _attention(q, k, v, tm).reshape(n, MLA_HEADS * V_HEAD)
            w_out = mla_w_out[j]
        else:
            xc, z, xbc, dt, adt, adtT = _ssd_pre(pending, sh1, sc1, g_mix_norm[i], ssm_w_in[j], ssm_conv_w[j],
                                                 ssm_conv_b[j], ssm_dt_bias[j], ssm_a_log[j], bsz, seqlen,
                                                 SSD_TILE)
            o = _ssd_scan(z, xbc, dt, adt, adtT, ssm_d[j], ssm_g_norm[j], bsz, seqlen, min(256, seqlen))
            w_out = ssm_w_out[j]
        x1, h2, meta, gates, counts = _post_mixer(xc, o, w_out, gt1, sh2, sc2, g_ffn_norm[i],
                                                  moe_w_router[i], moe_b_router[i], seqlen, tm)
        defer = i + 1 < depth and (i + 1) % 2 == 1
        out = _moe(x1, h2, meta, gates, counts, gt2, g_final, moe_w_gate_up, moe_b_gate_up,
                   moe_w_down, moe_b_down, i, seqlen, final=(i == depth - 1), defer_combine=defer)
        if defer:
            pending = out
        else:
            xc = out
    return xc.reshape(bsz, seqlen, d)
```

```python
import functools
import math

import jax
import jax.numpy as jnp
from jax import lax
from jax.experimental import pallas as pl
from jax.experimental.pallas import tpu as pltpu

F32 = jnp.float32
BF16 = jnp.bfloat16
I32 = jnp.int32
U32 = jnp.uint32

NORM_EPS = 1e-6
MLA_HEADS = 8
Q_LORA = 256
KV_LORA = 256
QK_NOPE = 128
QK_ROPE = 64
V_HEAD = 128
QK_HEAD = QK_NOPE + QK_ROPE
ROPE_THETA = 10000.0
MLA_SCALE = QK_HEAD ** -0.5
SSM_HEAD_DIM = 64
SSM_GROUPS = 8
SSM_HPG = 4
SSM_HEADS = SSM_GROUPS * SSM_HPG
SSM_STATE = 128
SSM_CONV = 4
SSM_GROUP_W = SSM_HPG * SSM_HEAD_DIM
SSM_INNER = SSM_GROUPS * SSM_GROUP_W
SSM_BC = SSM_GROUPS * SSM_STATE
SSM_CONV_DIM = SSM_INNER + 2 * SSM_BC
N_EXPERTS = 32
TOP_K = 4
SWIGLU_LIMIT = 7.0
SWIGLU_ALPHA = 1.702
MOE_BLOCK = 256

LANES = 128
VMEM_LIMIT = 56 * 1024 * 1024


def _cparams(*sem):
    return pltpu.CompilerParams(dimension_semantics=tuple(sem), vmem_limit_bytes=VMEM_LIMIT)


def _dot(a, b):
    return jnp.dot(a, b, preferred_element_type=F32)


def _dot_nt(a, b):
    return lax.dot_general(a, b, (((1,), (1,)), ((), ())), preferred_element_type=F32)


def _dot_tn(a, b):
    return lax.dot_general(a, b, (((0,), (0,)), ((), ())), preferred_element_type=F32)


def _split3(a):
    hi = a.astype(BF16)
    r1 = a - hi.astype(F32)
    mid = r1.astype(BF16)
    lo = (r1 - mid.astype(F32)).astype(BF16)
    return hi, mid, lo


def _dot_f32ish(a, b):
    ah, am, _ = _split3(a)
    bh, bm, _ = _split3(b)
    return _dot(ah, bh) + (_dot(ah, bm) + _dot(am, bh))


def _pack_bf16_pair(lo, hi):
    lo_b = lax.bitcast_convert_type(lo.astype(BF16).astype(F32), U32)
    hi_b = lax.bitcast_convert_type(hi.astype(BF16).astype(F32), U32)
    return hi_b | lax.shift_right_logical(lo_b, jnp.uint32(16))


def _unpack_bf16_pair(w):
    lo = lax.bitcast_convert_type(lax.shift_left(w, jnp.uint32(16)), F32)
    hi = lax.bitcast_convert_type(w & jnp.uint32(0xFFFF0000), F32)
    return lo, hi


ROW_CHUNKS = 4


def _chunk_copies(buf2d, rows_hbm, r0, nrows, sem, *, to_hbm):
    cps = []
    for c in range(ROW_CHUNKS):
        v = buf2d.at[:, pl.ds(c * LANES, LANES)]
        h = rows_hbm.at[pl.ds(r0, nrows), c]
        cps.append(pltpu.make_async_copy(v, h, sem) if to_hbm else pltpu.make_async_copy(h, v, sem))
    return cps


def _sigmoid(x):
    return 1.0 / (1.0 + jnp.exp(-x))


def _rms(x):
    return x * lax.rsqrt(jnp.mean(x * x, axis=-1, keepdims=True) + NORM_EPS)


def _mod_kernel(c_ref, w_ref, b_ref, o_ref):
    c = c_ref[...]
    cond = c * _sigmoid(c)
    o_ref[...] = _dot_f32ish(cond, w_ref[...]) + b_ref[...]


def _modulation(c, w_mod, b_mod):
    depth, d, d6 = w_mod.shape
    bsz = c.shape[0]
    nj = d6 // d
    return pl.pallas_call(
        _mod_kernel,
        out_shape=jax.ShapeDtypeStruct((depth, nj, bsz, d), F32),
        grid=(depth, nj),
        in_specs=[
            pl.BlockSpec((bsz, d), lambda l, j: (0, 0)),
            pl.BlockSpec((None, d, d), lambda l, j: (l, 0, j)),
            pl.BlockSpec((None, 1, d), lambda l, j: (l, 0, j)),
        ],
        out_specs=pl.BlockSpec((None, None, bsz, d), lambda l, j: (l, j, 0, 0)),
        compiler_params=_cparams("arbitrary", "arbitrary"),
        name="mod",
    )(c, w_mod, b_mod.reshape(depth, 1, d6))


def _mla_pre_kernel(x_ref, pos_ref, sh_ref, sc_ref, g_ref, win_ref, gq_ref, gkv_ref, wq_ref, wkv_ref,
                    invf_ref, q_ref, k_ref, v_ref):
    x = x_ref[...]
    h = _rms(x) * g_ref[...] * (1.0 + sc_ref[...]) + sh_ref[...]
    lat = _dot(h.astype(BF16), win_ref[...])
    q_lat = _rms(lat[:, :Q_LORA]) * gq_ref[...]
    kv_lat = _rms(lat[:, Q_LORA:Q_LORA + KV_LORA]) * gkv_ref[...]
    ang = pos_ref[...] * invf_ref[...]
    cs = jnp.cos(ang)
    sn = jnp.sin(ang)
    o = Q_LORA + KV_LORA
    k_rope = lat[:, o:o + LANES] * cs + lat[:, o + LANES:o + 2 * LANES] * sn
    qq = _dot(q_lat.astype(BF16), wq_ref[...])
    kv = _dot(kv_lat.astype(BF16), wkv_ref[...])
    rot0 = MLA_HEADS * 2 * LANES
    k_rope_b = k_rope[:, :QK_ROPE].astype(BF16)
    for hd in range(MLA_HEADS):
        c0 = hd * 2 * LANES
        q_nope = qq[:, c0:c0 + LANES] * MLA_SCALE
        q_rope = (qq[:, c0 + LANES:c0 + 2 * LANES] * cs
                  + qq[:, rot0 + hd * LANES:rot0 + (hd + 1) * LANES] * sn) * MLA_SCALE
        q_ref[hd, :, 0:QK_NOPE] = q_nope.astype(BF16)
        q_ref[hd, :, QK_NOPE:QK_HEAD] = q_rope[:, :QK_ROPE].astype(BF16)
        k_ref[hd, :, 0:QK_NOPE] = kv[:, c0:c0 + LANES].astype(BF16)
        k_ref[hd, :, QK_NOPE:QK_HEAD] = k_rope_b
        v_ref[hd] = kv[:, c0 + LANES:c0 + 2 * LANES].astype(BF16)


def _rot_half_cols(w):
    half = QK_ROPE // 2
    return jnp.concatenate([-w[..., half:], w[..., :half]], axis=-1)


def _mla_pre(x2d, pos_f, sh, sc, g, w_in, g_q, g_kv, w_q_up, w_kv_up, bsz, seqlen, tm):
    n, d = x2d.shape
    hh = MLA_HEADS
    o = Q_LORA + KV_LORA
    wr = w_in[:, o:o + QK_ROPE]
    zpad = jnp.zeros((d, LANES - QK_ROPE), F32)
    w_in_ext = jnp.concatenate([w_in[:, :o], wr, zpad, _rot_half_cols(wr), zpad], axis=1).astype(BF16)
    wq = w_q_up.reshape(Q_LORA, hh, QK_HEAD)
    zq = jnp.zeros((Q_LORA, hh, LANES - QK_ROPE), F32)
    wq_main = jnp.concatenate([wq, zq], axis=-1).reshape(Q_LORA, hh * 2 * LANES)
    wq_rot = jnp.concatenate([_rot_half_cols(wq[..., QK_NOPE:]), zq], axis=-1).reshape(Q_LORA, hh * LANES)
    wq_ext = jnp.concatenate([wq_main, wq_rot], axis=1).astype(BF16)
    inv_freq = 1.0 / (ROPE_THETA ** (jnp.arange(0, QK_ROPE, 2, dtype=F32) / QK_ROPE))
    invf = jnp.concatenate([inv_freq, inv_freq, jnp.zeros((LANES - QK_ROPE,), F32)]).reshape(1, LANES)
    tpb = seqlen // tm
    vec = lambda i: (0, 0)
    outs = pl.pallas_call(
        _mla_pre_kernel,
        out_shape=(
            jax.ShapeDtypeStruct((bsz, hh, seqlen, QK_HEAD), BF16),
            jax.ShapeDtypeStruct((bsz, hh, seqlen, QK_HEAD), BF16),
            jax.ShapeDtypeStruct((bsz, hh, seqlen, V_HEAD), BF16),
        ),
        grid=(n // tm,),
        in_specs=[
            pl.BlockSpec((tm, d), lambda i: (i, 0)),
            pl.BlockSpec((tm, 1), lambda i: (i, 0)),
            pl.BlockSpec((None, 1, d), lambda i: (i // tpb, 0, 0)),
            pl.BlockSpec((None, 1, d), lambda i: (i // tpb, 0, 0)),
            pl.BlockSpec((1, d), vec),
            pl.BlockSpec(w_in_ext.shape, vec),
            pl.BlockSpec((1, Q_LORA), vec),
            pl.BlockSpec((1, KV_LORA), vec),
            pl.BlockSpec(wq_ext.shape, vec),
            pl.BlockSpec((KV_LORA, hh * 2 * LANES), vec),
            pl.BlockSpec((1, LANES), vec),
        ],
        out_specs=(
            pl.BlockSpec((None, hh, tm, QK_HEAD), lambda i: (i // tpb, 0, i % tpb, 0)),
            pl.BlockSpec((None, hh, tm, QK_HEAD), lambda i: (i // tpb, 0, i % tpb, 0)),
            pl.BlockSpec((None, hh, tm, V_HEAD), lambda i: (i // tpb, 0, i % tpb, 0)),
        ),
        compiler_params=_cparams("arbitrary"),
        name="mla_pre",
    )(x2d, pos_f, sh, sc, g.reshape(1, d), w_in_ext, g_q.reshape(1, -1), g_kv.reshape(1, -1), wq_ext,
      w_kv_up.astype(BF16), invf)
    return outs


ATTN_HEADS_PER_STEP = 4


def _attn_kernel(q_ref, k_ref, v_ref, o_ref, m_scr, acc_scr, *, tq, hp):
    qi = pl.program_id(2)
    m_scr[...] = jnp.full(m_scr.shape, -jnp.inf, F32)
    acc_scr[...] = jnp.zeros(acc_scr.shape, F32)

    def block(hd, r0, tk, masked):
        k = k_ref[hd, pl.ds(r0, tk), :]
        v_ext = jnp.concatenate([v_ref[hd, pl.ds(r0, tk), :], jnp.ones((tk, V_HEAD), BF16)], axis=1)
        s = _dot_nt(q_ref[hd], k)
        if masked:
            row = lax.broadcasted_iota(I32, (tq, tk), 0)
            col = lax.broadcasted_iota(I32, (tq, tk), 1)
            s = jnp.where(row + (tk - tq) >= col, s, -jnp.inf)
        m_prev = m_scr[hd]
        m_new = jnp.maximum(m_prev, jnp.max(s, axis=-1, keepdims=True))
        alpha = jnp.exp(m_prev - m_new)
        p = jnp.exp(s - m_new).astype(BF16)
        acc_scr[hd] = alpha * acc_scr[hd] + _dot(p, v_ext)
        m_scr[hd] = m_new

    def body(j, carry):
        for hd in range(hp):
            block(hd, pl.multiple_of(j * (2 * tq), 2 * tq), 2 * tq, False)
        return carry

    lax.fori_loop(0, qi // 2, body, 0)

    @pl.when(qi % 2 == 1)
    def _():
        for hd in range(hp):
            block(hd, pl.multiple_of((qi - 1) * tq, tq), 2 * tq, True)

    @pl.when(qi % 2 == 0)
    def _():
        for hd in range(hp):
            block(hd, pl.multiple_of(qi * tq, tq), tq, True)

    for hd in range(hp):
        acc = acc_scr[hd]
        o_ref[:, hd * V_HEAD:(hd + 1) * V_HEAD] = (acc[:, :V_HEAD] / acc[:, V_HEAD:]).astype(o_ref.dtype)


def _attention(q, k, v, tq):
    bsz, hh, seqlen, _ = q.shape
    hp = ATTN_HEADS_PER_STEP
    return pl.pallas_call(
        functools.partial(_attn_kernel, tq=tq, hp=hp),
        out_shape=jax.ShapeDtypeStruct((bsz, seqlen, hh * V_HEAD), BF16),
        grid=(bsz, hh // hp, seqlen // tq),
        in_specs=[
            pl.BlockSpec((None, hp, tq, QK_HEAD), lambda b, h, i: (b, h, i, 0)),
            pl.BlockSpec((None, hp, seqlen, QK_HEAD), lambda b, h, i: (b, h, 0, 0)),
            pl.BlockSpec((None, hp, seqlen, V_HEAD), lambda b, h, i: (b, h, 0, 0)),
        ],
        out_specs=pl.BlockSpec((None, tq, hp * V_HEAD), lambda b, h, i: (b, i, h)),
        scratch_shapes=[
            pltpu.VMEM((hp, tq, 1), F32),
            pltpu.VMEM((hp, tq, 2 * V_HEAD), F32),
        ],
        compiler_params=_cparams("arbitrary", "arbitrary", "arbitrary"),
        name="attn",
    )(q, k, v)


def _post_mixer_kernel(x_ref, o_ref, wout_ref, gt1_ref, sh_ref, sc_ref, g_ref, wr_ref, br_ref,
                       x1_ref, h2_hbm, meta_ref, gate_ref, cnt_ref, carry_scr, pbuf, psem, *, tm):
    i = pl.program_id(0)

    @pl.when(i == 0)
    def _():
        carry_scr[...] = jnp.zeros(carry_scr.shape, F32)

    y = _dot(o_ref[...], wout_ref[...])
    x1 = x_ref[...] + gt1_ref[...] * y
    x1_ref[...] = x1
    h2 = _rms(x1) * g_ref[...] * (1.0 + sc_ref[...]) + sh_ref[...]
    half = h2.shape[1] // 2

    @pl.when(i > 0)
    def _():
        for cp in _chunk_copies(pbuf, h2_hbm, (i - 1) * tm, tm, psem, to_hbm=True):
            cp.wait()

    pbuf[...] = _pack_bf16_pair(h2[:, :half], h2[:, half:])
    for cp in _chunk_copies(pbuf, h2_hbm, i * tm, tm, psem, to_hbm=True):
        cp.start()

    logits = _dot_f32ish(h2, wr_ref[...]) + br_ref[...]
    lane = lax.broadcasted_iota(I32, (tm, LANES), 1).astype(F32)
    work = logits
    idxs, vals = [], []
    for _ in range(TOP_K):
        mx = jnp.max(work, axis=-1, keepdims=True)
        idx = jnp.min(jnp.where(work == mx, lane, float(LANES)), axis=-1, keepdims=True)
        idxs.append(idx)
        vals.append(mx)
        work = jnp.where(lane == idx, -jnp.inf, work)
    exps = [jnp.exp(vk - vals[0]) for vk in vals]
    denom = exps[0] + exps[1] + exps[2] + exps[3]
    onehot = jnp.zeros((tm, LANES), F32)
    for idx in idxs:
        onehot = onehot + jnp.where(lane == idx, 1.0, 0.0)
    row = lax.broadcasted_iota(I32, (tm, tm), 0)
    col = lax.broadcasted_iota(I32, (tm, tm), 1)
    ltri = jnp.where(row > col, 1.0, 0.0).astype(BF16)
    cum = _dot(ltri, onehot.astype(BF16)) + carry_scr[...]
    meta = jnp.zeros((tm, LANES), F32)
    gates = jnp.zeros((tm, LANES), F32)
    for kk in range(TOP_K):
        rank = jnp.sum(jnp.where(lane == idxs[kk], cum, 0.0), axis=-1, keepdims=True)
        meta = jnp.where(lane == float(kk), idxs[kk], meta)
        meta = jnp.where(lane == float(TOP_K + kk), rank, meta)
        gates = jnp.where(lane == float(kk), exps[kk] / denom, gates)
    meta_ref[...] = jnp.transpose(meta)[:2 * TOP_K, :].astype(I32)
    gate_ref[...] = gates
    carry = carry_scr[...] + jnp.sum(onehot, axis=0, keepdims=True)
    carry_scr[...] = carry
    cnt_ref[...] = carry.astype(I32)

    @pl.when(i == pl.num_programs(0) - 1)
    def _():
        for cp in _chunk_copies(pbuf, h2_hbm, i * tm, tm, psem, to_hbm=True):
            cp.wait()


def _post_mixer(x2d, o2d, w_out, gt1, sh2, sc2, g_ffn, w_router, b_router, seqlen, tm):
    n, d = x2d.shape
    kdim = o2d.shape[1]
    tpb = seqlen // tm
    wr = jnp.concatenate([w_router, jnp.zeros((d, LANES - N_EXPERTS), F32)], axis=1)
    br = jnp.concatenate([b_router, jnp.full((LANES - N_EXPERTS,), -1e30, F32)]).reshape(1, LANES)
    vec = lambda i: (0, 0)
    bvec = lambda i: (i // tpb, 0, 0)
    return pl.pallas_call(
        functools.partial(_post_mixer_kernel, tm=tm),
        out_shape=(
            jax.ShapeDtypeStruct((n, d), F32),
            jax.ShapeDtypeStruct((n, ROW_CHUNKS, LANES), U32),
            jax.ShapeDtypeStruct((2 * TOP_K, n), I32),
            jax.ShapeDtypeStruct((n, LANES), F32),
            jax.ShapeDtypeStruct((1, LANES), I32),
        ),
        grid=(n // tm,),
        in_specs=[
            pl.BlockSpec((tm, d), lambda i: (i, 0)),
            pl.BlockSpec((tm, kdim), lambda i: (i, 0)),
            pl.BlockSpec((kdim, d), vec),
            pl.BlockSpec((None, 1, d), bvec),
            pl.BlockSpec((None, 1, d), bvec),
            pl.BlockSpec((None, 1, d), bvec),
            pl.BlockSpec((1, d), vec),
            pl.BlockSpec((d, LANES), vec),
            pl.BlockSpec((1, LANES), vec),
        ],
        out_specs=(
            pl.BlockSpec((tm, d), lambda i: (i, 0)),
            pl.BlockSpec(memory_space=pl.ANY),
            pl.BlockSpec((2 * TOP_K, tm), lambda i: (0, i)),
            pl.BlockSpec((tm, LANES), lambda i: (i, 0)),
            pl.BlockSpec((1, LANES), vec),
        ),
        scratch_shapes=[pltpu.VMEM((1, LANES), F32), pltpu.VMEM((tm, d // 2), U32), pltpu.SemaphoreType.DMA],
        compiler_params=_cparams("arbitrary"),
        name="post_mixer",
    )(x2d, o2d, w_out.astype(BF16), gt1, sh2, sc2, g_ffn.reshape(1, d), wr, br)


ROW_UNROLL = 8


def _row_copy(src, s, dst, t, sem):
    return pltpu.make_async_copy(src.at[s], dst.at[t], sem)


def _dest_kernel(pstart_ref, meta_ref, dest_ref):
    e = meta_ref[0:TOP_K, :]
    dest = meta_ref[TOP_K:2 * TOP_K, :]
    for j in range(N_EXPERTS):
        dest = dest + jnp.where(e == j, pstart_ref[j], 0)
    dest_ref[...] = dest


def _dest_rows(meta_t, pstart):
    n = meta_t.shape[1]
    tn = min(4096, n)
    grid_spec = pltpu.PrefetchScalarGridSpec(
        num_scalar_prefetch=1,
        grid=(n // tn,),
        in_specs=[pl.BlockSpec((2 * TOP_K, tn), lambda i, ps: (0, i))],
        out_specs=pl.BlockSpec((TOP_K, tn), lambda i, ps: (0, i)),
    )
    return pl.pallas_call(
        _dest_kernel,
        out_shape=jax.ShapeDtypeStruct((TOP_K, n), I32),
        grid_spec=grid_spec,
        compiler_params=_cparams("arbitrary"),
        name="dest_rows",
    )(pstart, meta_t)


def _dispatch_kernel(pstart_ref, cnt_ref, dest_ref, h_ref, xs_hbm, zbuf, sem, zsem, *, tm):
    i = pl.program_id(0)

    @pl.when(i == 0)
    def _():
        zbuf[...] = jnp.zeros(zbuf.shape, U32)

        def per_expert(e, carry):
            lo = pstart_ref[e] + cnt_ref[e]
            hi = pstart_ref[e + 1]

            def start(j, c):
                _row_copy(zbuf, 0, xs_hbm, j, zsem).start()
                return c

            def wait(j, c):
                _row_copy(zbuf, 0, xs_hbm, j, zsem).wait()
                return c

            lax.fori_loop(lo, hi, start, 0)
            lax.fori_loop(lo, hi, wait, 0)
            return carry

        lax.fori_loop(0, N_EXPERTS, per_expert, 0)

    def start(tt, c):
        for u in range(ROW_UNROLL):
            t = tt * ROW_UNROLL + u
            for kk in range(TOP_K):
                _row_copy(h_ref, t, xs_hbm, dest_ref[kk, t], sem).start(priority=kk % 2)
        return c

    def wait(tt, c):
        for u in range(ROW_UNROLL * TOP_K):
            _row_copy(h_ref, 0, xs_hbm, 0, sem).wait()
        return c

    lax.fori_loop(0, tm // ROW_UNROLL, start, 0)
    lax.fori_loop(0, tm // ROW_UNROLL, wait, 0)


def _dispatch(h2p, dest_t, pstart, counts, n_rows, tm):
    n = h2p.shape[0]
    grid_spec = pltpu.PrefetchScalarGridSpec(
        num_scalar_prefetch=2,
        grid=(n // tm,),
        in_specs=[
            pl.BlockSpec((TOP_K, tm), lambda i, ps, cn: (0, i), memory_space=pltpu.SMEM),
            pl.BlockSpec((tm, ROW_CHUNKS, LANES), lambda i, ps, cn: (i, 0, 0)),
        ],
        out_specs=pl.BlockSpec(memory_space=pl.ANY),
        scratch_shapes=[
            pltpu.VMEM((1, ROW_CHUNKS, LANES), U32),
            pltpu.SemaphoreType.DMA,
            pltpu.SemaphoreType.DMA,
        ],
    )
    return pl.pallas_call(
        functools.partial(_dispatch_kernel, tm=tm),
        out_shape=jax.ShapeDtypeStruct((n_rows, ROW_CHUNKS, LANES), U32),
        grid_spec=grid_spec,
        compiler_params=pltpu.CompilerParams(dimension_semantics=("arbitrary",), vmem_limit_bytes=VMEM_LIMIT,
                                             has_side_effects=True),
        name="dispatch",
    )(pstart, counts, dest_t, h2p)


def _ffn_kernel(be_ref, nu_ref, first_ref, wslot_ref, nxt_ref, xs_hbm, wgu_hbm, bgu_ref, wd_hbm, bd_ref, ys_hbm,
                wgu_b, wd_b, wgu_f, wd_f, xbuf, ybuf, xsem, ysem, wsem, *, ff, layer):
    i = pl.program_id(0)
    nu = nu_ref[0]
    slot = i % 2

    def w_copies(e, s):
        return [pltpu.make_async_copy(wgu_hbm.at[layer, e], wgu_f.at[s], wsem.at[s]),
                pltpu.make_async_copy(wd_hbm.at[layer, e], wd_f.at[s], wsem.at[s])]

    @pl.when(i == 0)
    def _():
        for cp in w_copies(be_ref[0], 0):
            cp.start()

    def x_copies(step, s):
        return _chunk_copies(xbuf.at[s], xs_hbm, step * MOE_BLOCK, MOE_BLOCK, xsem.at[s], to_hbm=False)

    def y_copies(step, s):
        return _chunk_copies(ybuf.at[s], ys_hbm, step * MOE_BLOCK, MOE_BLOCK, ysem.at[s], to_hbm=True)

    @pl.when(i == 0)
    def _():
        for cp in x_copies(0, 0):
            cp.start()

    @pl.when(i + 1 < nu)
    def _():
        for cp in x_copies(i + 1, 1 - slot):
            cp.start()

    @pl.when(first_ref[i] == 1)
    def _():
        ws = wslot_ref[i]
        for cp in w_copies(be_ref[i], ws):
            cp.wait()

        @pl.when(nxt_ref[i] >= 0)
        def _():
            for cp in w_copies(nxt_ref[i], 1 - ws):
                cp.start(priority=1)

        wgu_b[...] = wgu_f[ws].astype(BF16)
        wd_b[...] = wd_f[ws].astype(BF16)

    @pl.when(i < nu)
    def _():
        for cp in x_copies(i, slot):
            cp.wait()
        x_lo, x_hi = _unpack_bf16_pair(xbuf[slot])
        dh = x_lo.shape[1]
        gu = (_dot(x_lo.astype(BF16), wgu_b[0:dh, :]) + _dot(x_hi.astype(BF16), wgu_b[dh:2 * dh, :])
              + bgu_ref[...])
        g = jnp.minimum(gu[:, :ff], SWIGLU_LIMIT)
        lin = jnp.clip(gu[:, ff:], -SWIGLU_LIMIT, SWIGLU_LIMIT)
        act = g * _sigmoid(SWIGLU_ALPHA * g) * (lin + 1.0)
        y = _dot(act.astype(BF16), wd_b[...]) + bd_ref[...]

        @pl.when(i >= 2)
        def _():
            for cp in y_copies(i - 2, slot):
                cp.wait()

        ybuf[slot] = _pack_bf16_pair(y[:, :dh], y[:, dh:])
        for cp in y_copies(i, slot):
            cp.start()

    @pl.when(i == nu - 1)
    def _():
        for cp in y_copies(i, slot):
            cp.wait()

        @pl.when(i >= 1)
        def _():
            for cp in y_copies(i - 1, 1 - slot):
                cp.wait()


def _ffn(xs, blk_e, n_used, pad_end, w_gate_up, b_gate_up, w_down, b_down, layer):
    n_rows = xs.shape[0]
    depth, ne, d, ff2 = w_gate_up.shape
    ff = ff2 // 2
    n_blk = n_rows // MOE_BLOCK
    blk = jnp.arange(n_blk, dtype=I32)
    active = blk < n_used[0]
    first = (active & ((blk == 0) | (blk_e != jnp.roll(blk_e, 1)))).astype(I32)
    wslot = ((jnp.cumsum(first) - 1) % 2).astype(I32)
    nxt_blk = pad_end[blk_e] // MOE_BLOCK
    nxt = jnp.where(nxt_blk < n_used[0], blk_e[jnp.minimum(nxt_blk, n_blk - 1)], -1).astype(I32)
    row_buf = pltpu.VMEM((2, MOE_BLOCK, ROW_CHUNKS * LANES), U32)
    sp = lambda i, be, nu, fi, ws, nx: (layer, be[i], 0, 0)
    grid_spec = pltpu.PrefetchScalarGridSpec(
        num_scalar_prefetch=5,
        grid=(n_blk,),
        in_specs=[
            pl.BlockSpec(memory_space=pl.ANY),
            pl.BlockSpec(memory_space=pl.ANY),
            pl.BlockSpec((None, None, 1, ff2), sp),
            pl.BlockSpec(memory_space=pl.ANY),
            pl.BlockSpec((None, None, 1, d), sp),
        ],
        out_specs=pl.BlockSpec(memory_space=pl.ANY),
        scratch_shapes=[pltpu.VMEM((d, ff2), BF16), pltpu.VMEM((ff, d), BF16),
                        pltpu.VMEM((2, d, ff2), F32), pltpu.VMEM((2, ff, d), F32), row_buf, row_buf,
                        pltpu.SemaphoreType.DMA((2,)), pltpu.SemaphoreType.DMA((2,)),
                        pltpu.SemaphoreType.DMA((2,))],
    )
    return pl.pallas_call(
        functools.partial(_ffn_kernel, ff=ff, layer=layer),
        out_shape=jax.ShapeDtypeStruct((n_rows, ROW_CHUNKS, LANES), U32),
        grid_spec=grid_spec,
        compiler_params=_cparams("arbitrary"),
        name="ffn",
    )(blk_e, n_used, first, wslot, nxt, xs, w_gate_up, b_gate_up.reshape(depth, ne, 1, ff2), w_down,
      b_down.reshape(depth, ne, 1, d))


def _gather_scratch(tm):
    return [pltpu.VMEM((2, TOP_K, tm // ROW_UNROLL, ROW_CHUNKS, ROW_UNROLL, LANES), U32),
            pltpu.SemaphoreType.DMA((2,))]


def _gather_start(dest_ref, ys_hbm, buf, sem, tm, both_queues):
    def body(tt, c):
        for u in range(ROW_UNROLL):
            for kk in range(TOP_K):
                src = ys_hbm.at[dest_ref[kk, tt * ROW_UNROLL + u]]
                pltpu.make_async_copy(src, buf.at[kk, tt, :, u], sem).start(priority=kk % 2 if both_queues else 1)
        return c

    lax.fori_loop(0, tm // ROW_UNROLL, body, 0)


def _gather_wait(ys_hbm, buf, sem, tm):
    def body(tt, c):
        for u in range(ROW_UNROLL * TOP_K):
            pltpu.make_async_copy(ys_hbm.at[0], buf.at[0, 0, :, 0], sem).wait()
        return c

    lax.fori_loop(0, tm // ROW_UNROLL, body, 0)


def _combine_rows(dest0_ref, dest_next_ref, gate_ref, ys_hbm, buf, sem, tm, both_queues):
    i = pl.program_id(0)
    slot = i % 2

    @pl.when(i == 0)
    def _():
        _gather_start(dest0_ref, ys_hbm, buf.at[0], sem.at[0], tm, both_queues)

    @pl.when(i + 1 < pl.num_programs(0))
    def _():
        _gather_start(dest_next_ref, ys_hbm, buf.at[1 - slot], sem.at[1 - slot], tm, both_queues)

    _gather_wait(ys_hbm, buf.at[slot], sem.at[slot], tm)
    gates = gate_ref[...]
    acc_lo = acc_hi = None
    for kk in range(TOP_K):
        rows = jnp.concatenate([buf[slot, kk, :, c, :, :].reshape(tm, LANES) for c in range(ROW_CHUNKS)], axis=1)
        lo, hi = _unpack_bf16_pair(rows)
        gk = gates[:, kk:kk + 1]
        acc_lo = gk * lo if acc_lo is None else acc_lo + gk * lo
        acc_hi = gk * hi if acc_hi is None else acc_hi + gk * hi
    return jnp.concatenate([acc_lo, acc_hi], axis=1)


def _combine_specs(tm, nsteps, idx):
    return [pl.BlockSpec((TOP_K, tm), lambda *a: (0, 0), memory_space=pltpu.SMEM),
            pl.BlockSpec((TOP_K, tm), lambda *a: (0, jnp.minimum(idx(*a) + 1, nsteps - 1)), memory_space=pltpu.SMEM)]


def _combine_kernel(dest0_ref, dest_next_ref, x1_ref, gate_ref, gt2_ref, gfin_ref, ys_hbm, o_ref, buf, sem,
                    *, tm, final):
    y = _combine_rows(dest0_ref, dest_next_ref, gate_ref, ys_hbm, buf, sem, tm, both_queues=True)
    x2 = x1_ref[...] + gt2_ref[...] * y
    if final:
        x2 = _rms(x2) * gfin_ref[...]
    o_ref[...] = x2


def _combine(x1, gates, gt2, g_final, ys, dest_t, seqlen, tm, final):
    n, d = x1.shape
    tpb = seqlen // tm
    grid_spec = pltpu.PrefetchScalarGridSpec(
        num_scalar_prefetch=0,
        grid=(n // tm,),
        in_specs=_combine_specs(tm, n // tm, lambda i: i) + [
            pl.BlockSpec((tm, d), lambda i: (i, 0)),
            pl.BlockSpec((tm, LANES), lambda i: (i, 0)),
            pl.BlockSpec((None, 1, d), lambda i: (i // tpb, 0, 0)),
            pl.BlockSpec((1, d), lambda i: (0, 0)),
            pl.BlockSpec(memory_space=pl.ANY),
        ],
        out_specs=pl.BlockSpec((tm, d), lambda i: (i, 0)),
        scratch_shapes=_gather_scratch(tm),
    )
    return pl.pallas_call(
        functools.partial(_combine_kernel, tm=tm, final=final),
        out_shape=jax.ShapeDtypeStruct((n, d), F32),
        grid_spec=grid_spec,
        compiler_params=_cparams("arbitrary"),
        name="combine",
    )(dest_t, dest_t, x1, gates, gt2, g_final.reshape(1, d), ys)


def _moe(x1, h2p, meta_t, gates, counts, gt2, g_final, w_gate_up, b_gate_up, w_down, b_down, layer, seqlen,
         final, defer_combine):
    n, d = x1.shape
    n_pair = n * TOP_K
    n_rows = -(-n_pair // MOE_BLOCK) * MOE_BLOCK + N_EXPERTS * MOE_BLOCK
    n_blk = n_rows // MOE_BLOCK
    cnt = counts[0, :N_EXPERTS]
    padded = (cnt + MOE_BLOCK - 1) // MOE_BLOCK * MOE_BLOCK
    pad_end = jnp.cumsum(padded)
    pstart = jnp.concatenate([pad_end - padded, pad_end[-1:]]).astype(I32)
    n_used = (pad_end[-1:] // MOE_BLOCK).astype(I32)
    blk_start = jnp.arange(n_blk, dtype=I32) * MOE_BLOCK
    blk_e = jnp.minimum(jnp.sum(blk_start[:, None] >= pad_end[None, :], axis=1), N_EXPERTS - 1).astype(I32)
    tm = min(256, seqlen)
    dest_t = _dest_rows(meta_t, pstart)
    xs = _dispatch(h2p, dest_t, pstart, cnt, n_rows, tm)
    ys = _ffn(xs, blk_e, n_used, pad_end.astype(I32), w_gate_up, b_gate_up, w_down, b_down, layer)
    if defer_combine:
        return x1, gates, gt2, ys, dest_t
    return _combine(x1, gates, gt2, g_final, ys, dest_t, seqlen, tm, final)


def _softplus(x):
    return jnp.maximum(x, 0.0) + jnp.log(1.0 + jnp.exp(-jnp.abs(x)))


SSD_TILE = 256


def _ssd_pre_kernel(dest0_ref, dest_next_ref, x1_ref, gate_ref, gt2_ref, ys_hbm,
                    sh_ref, sc_ref, g_ref, wz_ref, wx_ref, wdt_ref, wdtT_ref, cw_ref, cb_ref,
                    dtb_ref, dtbT_ref, a_ref, aT_ref,
                    x_out_ref, z_ref, xbc_ref, dt_ref, adt_ref, adtT_ref, ubuf, gbuf, gsem, *, tm, tpb, cchunk):
    i = pl.program_id(0)
    x = x1_ref[...] + gt2_ref[...] * _combine_rows(dest0_ref, dest_next_ref, gate_ref, ys_hbm, gbuf, gsem, tm,
                                                     both_queues=False)
    x_out_ref[...] = x
    h = _rms(x) * g_ref[...] * (1.0 + sc_ref[...]) + sh_ref[...]
    hb = h.astype(BF16)
    z_ref[...] = _dot(hb, wz_ref[...]).astype(BF16)
    dt = _softplus(_dot(hb, wdt_ref[...]) + dtb_ref[...])
    dt_ref[...] = dt
    adt_ref[...] = dt * a_ref[...]
    dtT = _softplus(_dot_nt(wdtT_ref[...], hb) + dtbT_ref[...])
    adtT_ref[...] = dtT * aT_ref[...]

    @pl.when(i % tpb == 0)
    def _():
        ubuf[...] = jnp.zeros(ubuf.shape, F32)

    for c0 in range(0, SSM_CONV_DIM, cchunk):
        cols = slice(c0, c0 + cchunk)
        u = _dot(hb, wx_ref[:, cols])
        ext = jnp.concatenate([ubuf[:, cols], u], axis=0)
        acc = cb_ref[:, cols] + cw_ref[SSM_CONV - 1:SSM_CONV, cols] * u
        for j in range(1, SSM_CONV):
            shifted = pltpu.roll(ext, j, axis=0)[8:, :]
            acc = acc + cw_ref[SSM_CONV - 1 - j:SSM_CONV - j, cols] * shifted
        xbc_ref[:, cols] = (acc * _sigmoid(acc)).astype(BF16)
        ubuf[:, cols] = u[tm - 8:, :]


def _ssd_pre(moe_out, sh, sc, g, w_in, conv_w, conv_b, dt_bias, a_log, bsz, seqlen, tm):
    x1, gates, gt2, ys, dest_t = moe_out
    n, d = x1.shape
    tm = min(tm, seqlen)
    tpb = seqlen // tm
    nh = SSM_HEADS
    wz = w_in[:, :SSM_INNER].astype(BF16)
    wx = w_in[:, SSM_INNER:SSM_INNER + SSM_CONV_DIM].astype(BF16)
    wdt_raw = w_in[:, SSM_INNER + SSM_CONV_DIM:]
    wdt = jnp.concatenate([wdt_raw, jnp.zeros((d, LANES - nh), F32)], axis=1).astype(BF16)
    wdtT = wdt_raw.T.astype(BF16)
    pad = jnp.zeros((LANES - nh,), F32)
    dtb = jnp.concatenate([dt_bias, pad]).reshape(1, LANES)
    a_neg = -jnp.exp(a_log.astype(F32))
    a_row = jnp.concatenate([a_neg, pad]).reshape(1, LANES)
    vec = lambda i: (0, 0)
    bvec = lambda i: (i // tpb, 0, 0)
    return pl.pallas_call(
        functools.partial(_ssd_pre_kernel, tm=tm, tpb=tpb, cchunk=1024),
        out_shape=(
            jax.ShapeDtypeStruct((n, d), F32),
            jax.ShapeDtypeStruct((n, SSM_INNER), BF16),
            jax.ShapeDtypeStruct((n, SSM_CONV_DIM), BF16),
            jax.ShapeDtypeStruct((n, LANES), F32),
            jax.ShapeDtypeStruct((n, LANES), F32),
            jax.ShapeDtypeStruct((bsz, nh, seqlen), F32),
        ),
        grid=(n // tm,),
        in_specs=_combine_specs(tm, n // tm, lambda i: i) + [
            pl.BlockSpec((tm, d), lambda i: (i, 0)),
            pl.BlockSpec((tm, LANES), lambda i: (i, 0)),
            pl.BlockSpec((None, 1, d), bvec),
            pl.BlockSpec(memory_space=pl.ANY),
            pl.BlockSpec((None, 1, d), bvec),
            pl.BlockSpec((None, 1, d), bvec),
            pl.BlockSpec((1, d), vec),
            pl.BlockSpec((d, SSM_INNER), vec),
            pl.BlockSpec((d, SSM_CONV_DIM), vec),
            pl.BlockSpec((d, LANES), vec),
            pl.BlockSpec((nh, d), vec),
            pl.BlockSpec((SSM_CONV, SSM_CONV_DIM), vec),
            pl.BlockSpec((1, SSM_CONV_DIM), vec),
            pl.BlockSpec((1, LANES), vec),
            pl.BlockSpec((nh, 1), vec),
            pl.BlockSpec((1, LANES), vec),
            pl.BlockSpec((nh, 1), vec),
        ],
        out_specs=(
            pl.BlockSpec((tm, d), lambda i: (i, 0)),
            pl.BlockSpec((tm, SSM_INNER), lambda i: (i, 0)),
            pl.BlockSpec((tm, SSM_CONV_DIM), lambda i: (i, 0)),
            pl.BlockSpec((tm, LANES), lambda i: (i, 0)),
            pl.BlockSpec((tm, LANES), lambda i: (i, 0)),
            pl.BlockSpec((None, nh, tm), lambda i: (i // tpb, 0, i % tpb)),
        ),
        scratch_shapes=[pltpu.VMEM((8, SSM_CONV_DIM), F32)] + _gather_scratch(tm),
        compiler_params=_cparams("arbitrary"),
        name="ssd_pre",
    )(dest_t, dest_t, x1, gates, gt2, ys, sh, sc, g.reshape(1, d), wz, wx, wdt, wdtT, conv_w,
      conv_b.reshape(1, -1), dtb, dt_bias.reshape(nh, 1), a_row, a_neg.reshape(nh, 1))


def _expand_heads(v, g, rows):
    lane = lax.broadcasted_iota(I32, (rows, LANES), 1)
    lo = lane < SSM_HEAD_DIM
    parts = []
    for j in range(0, SSM_HPG, 2):
        h0 = g * SSM_HPG + j
        parts.append(jnp.where(lo, v[:, h0:h0 + 1], v[:, h0 + 1:h0 + 2]))
    return jnp.concatenate(parts, axis=1)


def _ssd_scan_kernel(xbc_ref, z_ref, dt_ref, adt_ref, adtT_ref, d_ref, gn_ref, hx_ref, yn_ref, state, *, lc):
    c = pl.program_id(1)

    @pl.when(c == 0)
    def _():
        state[...] = jnp.zeros(state.shape, F32)

    row = lax.broadcasted_iota(I32, (lc, lc), 0)
    col = lax.broadcasted_iota(I32, (lc, lc), 1)
    causal = row >= col
    tri = jnp.where(causal, 1.0, 0.0).astype(BF16)
    triT = jnp.where(row <= col, 1.0, 0.0).astype(BF16)
    ah, am, al = _split3(adt_ref[...])
    a_cum = _dot(tri, ah) + (_dot(tri, am) + _dot(tri, al))
    th, tmid, tl = _split3(adtT_ref[...])
    a_cumT = _dot(th, triT) + (_dot(tmid, triT) + _dot(tl, triT))
    dt = dt_ref[...]
    a_last = a_cum[lc - 1:lc, :]
    e_cum = jnp.exp(a_cum)
    d2e = jnp.exp(a_last - a_cum)
    cdec = jnp.exp(a_last)
    dskip = d_ref[...]
    hx = hx_ref[...]
    dt_x = _dot(dt.astype(BF16), hx)
    dtd2e_x = _dot((dt * d2e).astype(BF16), hx)
    ecum_x = _dot(e_cum.astype(BF16), hx)
    lane2 = lax.broadcasted_iota(I32, (lc, SSM_GROUP_W), 1)
    for g in range(SSM_GROUPS):
        xg = xbc_ref[:, g * SSM_GROUP_W:(g + 1) * SSM_GROUP_W].astype(F32)
        b0 = SSM_INNER + g * SSM_STATE
        c0 = SSM_INNER + SSM_BC + g * SSM_STATE
        bg = xbc_ref[:, b0:b0 + SSM_STATE]
        cg = xbc_ref[:, c0:c0 + SSM_STATE]
        gcols = slice(g * SSM_GROUP_W, (g + 1) * SSM_GROUP_W)
        xdt_b = (xg * dt_x[:, gcols]).astype(BF16)
        cb = _dot_nt(cg, bg)
        y = jnp.zeros((lc, SSM_GROUP_W), F32)
        for j in range(SSM_HPG):
            hd = g * SSM_HPG + j
            seg = a_cum[:, hd:hd + 1] - a_cumT[hd:hd + 1, :]
            decay = jnp.exp(jnp.where(causal, seg, -jnp.inf))
            m = (cb * decay).astype(BF16)
            in_head = (lane2 >= j * SSM_HEAD_DIM) & (lane2 < (j + 1) * SSM_HEAD_DIM)
            y = y + _dot(m, jnp.where(in_head, xdt_b, jnp.zeros_like(xdt_b)))
        st = state[g]
        y = y + _dot(cg, st.astype(BF16)) * ecum_x[:, gcols]
        xd2e = (xg * dtd2e_x[:, gcols]).astype(BF16)
        state[g] = st * _expand_heads(cdec, g, 1) + _dot_tn(bg, xd2e)
        y = y + _expand_heads(dskip, g, 1) * xg
        zg = z_ref[:, g * SSM_GROUP_W:(g + 1) * SSM_GROUP_W].astype(F32)
        yz = y * (zg * _sigmoid(zg))
        yn = _rms(yz) * gn_ref[:, g * SSM_GROUP_W:(g + 1) * SSM_GROUP_W]
        yn_ref[:, g * SSM_GROUP_W:(g + 1) * SSM_GROUP_W] = yn.astype(BF16)


def _ssd_scan(z, xbc, dt, adt, adtT, d_skip, g_norm, bsz, seqlen, lc):
    n = z.shape[0]
    nc = seqlen // lc
    pad = jnp.zeros((LANES - SSM_HEADS,), F32)
    d_row = jnp.concatenate([d_skip, pad]).reshape(1, LANES)
    rows = lambda b, c: (b * nc + c, 0)
    vec = lambda b, c: (0, 0)
    head_of_col = jnp.arange(SSM_INNER, dtype=I32) // SSM_HEAD_DIM
    head_expand = (jnp.arange(LANES, dtype=I32)[:, None] == head_of_col[None, :]).astype(BF16)
    return pl.pallas_call(
        functools.partial(_ssd_scan_kernel, lc=lc),
        out_shape=jax.ShapeDtypeStruct((n, SSM_INNER), BF16),
        grid=(bsz, nc),
        in_specs=[
            pl.BlockSpec((lc, SSM_CONV_DIM), rows),
            pl.BlockSpec((lc, SSM_INNER), rows),
            pl.BlockSpec((lc, LANES), rows),
            pl.BlockSpec((lc, LANES), rows),
            pl.BlockSpec((None, SSM_HEADS, lc), lambda b, c: (b, 0, c)),
            pl.BlockSpec((1, LANES), vec),
            pl.BlockSpec((1, SSM_INNER), vec),
            pl.BlockSpec((LANES, SSM_INNER), vec),
        ],
        out_specs=pl.BlockSpec((lc, SSM_INNER), rows),
        scratch_shapes=[pltpu.VMEM((SSM_GROUPS, SSM_STATE, SSM_GROUP_W), F32)],
        compiler_params=_cparams("arbitrary", "arbitrary"),
        name="ssd_scan",
    )(xbc, z, dt, adt, adtT, d_row, g_norm.reshape(1, -1), head_expand)


def kernel(x, c, positions, w_mod, b_mod, g_mix_norm, g_ffn_norm, mla_w_in, mla_g_q, mla_g_kv, mla_w_q_up, mla_w_kv_up, mla_w_out, ssm_w_in, ssm_conv_w, ssm_conv_b, ssm_dt_bias, ssm_a_log, ssm_d, ssm_g_norm, ssm_w_out, moe_w_router, moe_b_router, moe_w_gate_up, moe_b_gate_up, moe_w_down, moe_b_down, g_final):
    bsz, seqlen, d = x.shape
    depth = w_mod.shape[0]
    n = bsz * seqlen
    tm = min(512, seqlen)
    mod = _modulation(c, w_mod, b_mod)
    mod = mod.reshape(depth, 6, bsz, 1, d)
    pos_f = positions.astype(F32).reshape(n, 1)
    xc = x.reshape(n, d)
    pending = None
    for i in range(depth):
        sh1, sc1, gt1, sh2, sc2, gt2 = [mod[i, j] for j in range(6)]
        j = i // 2
        if i % 2 == 0:
            q, k, v = _mla_pre(xc, pos_f, sh1, sc1, g_mix_norm[i], mla_w_in[j], mla_g_q[j], mla_g_kv[j],
                               mla_w_q_up[j], mla_w_kv_up[j], bsz, seqlen, tm)
            o = _attention(q, k, v, tm).reshape(n, MLA_HEADS * V_HEAD)
            w_out = mla_w_out[j]
        else:
            xc, z, xbc, dt, adt, adtT = _ssd_pre(pending, sh1, sc1, g_mix_norm[i], ssm_w_in[j], ssm_conv_w[j],
                                                 ssm_conv_b[j], ssm_dt_bias[j], ssm_a_log[j], bsz, seqlen,
                                                 SSD_TILE)
            o = _ssd_scan(z, xbc, dt, adt, adtT, ssm_d[j], ssm_g_norm[j], bsz, seqlen, min(256, seqlen))
            w_out = ssm_w_out[j]
        x1, h2, meta, gates, counts = _post_mixer(xc, o, w_out, gt1, sh2, sc2, g_ffn_norm[i],
                                                  moe_w_router[i], moe_b_router[i], seqlen, tm)
        defer = i + 1 < depth and (i + 1) % 2 == 1
        out = _moe(x1, h2, meta, gates, counts, gt2, g_final, moe_w_gate_up, moe_b_gate_up,
                   moe_w_down, moe_b_down, i, seqlen, final=(i == depth - 1), defer_combine=defer)
        if defer:
            pending = out
        else:
            xc = out
    return xc.reshape(bsz, seqlen, d)
```

```python
import functools
import math

import jax
import jax.numpy as jnp
from jax import lax
from jax.experimental import pallas as pl
from jax.experimental.pallas import tpu as pltpu

F32 = jnp.float32
BF16 = jnp.bfloat16
I32 = jnp.int32
U32 = jnp.uint32

NORM_EPS = 1e-6
MLA_HEADS = 8
Q_LORA = 256
KV_LORA = 256
QK_NOPE = 128
QK_ROPE = 64
V_HEAD = 128
QK_HEAD = QK_NOPE + QK_ROPE
ROPE_THETA = 10000.0
MLA_SCALE = QK_HEAD ** -0.5
SSM_HEAD_DIM = 64
SSM_GROUPS = 8
SSM_HPG = 4
SSM_HEADS = SSM_GROUPS * SSM_HPG
SSM_STATE = 128
SSM_CONV = 4
SSM_GROUP_W = SSM_HPG * SSM_HEAD_DIM
SSM_INNER = SSM_GROUPS * SSM_GROUP_W
SSM_BC = SSM_GROUPS * SSM_STATE
SSM_CONV_DIM = SSM_INNER + 2 * SSM_BC
N_EXPERTS = 32
TOP_K = 4
SWIGLU_LIMIT = 7.0
SWIGLU_ALPHA = 1.702
MOE_BLOCK = 512

LANES = 128
VMEM_LIMIT = 56 * 1024 * 1024


def _cparams(*sem):
    return pltpu.CompilerParams(dimension_semantics=tuple(sem), vmem_limit_bytes=VMEM_LIMIT)


def _dot(a, b):
    return jnp.dot(a, b, preferred_element_type=F32)


def _dot_nt(a, b):
    return lax.dot_general(a, b, (((1,), (1,)), ((), ())), preferred_element_type=F32)


def _dot_tn(a, b):
    return lax.dot_general(a, b, (((0,), (0,)), ((), ())), preferred_element_type=F32)


def _split3(a):
    hi = a.astype(BF16)
    r1 = a - hi.astype(F32)
    mid = r1.astype(BF16)
    lo = (r1 - mid.astype(F32)).astype(BF16)
    return hi, mid, lo


def _dot_f32ish(a, b):
    ah, am, _ = _split3(a)
    bh, bm, _ = _split3(b)
    return _dot(ah, bh) + (_dot(ah, bm) + _dot(am, bh))


def _pack_bf16_pair(lo, hi):
    lo_b = lax.bitcast_convert_type(lo.astype(BF16).astype(F32), U32)
    hi_b = lax.bitcast_convert_type(hi.astype(BF16).astype(F32), U32)
    return hi_b | lax.shift_right_logical(lo_b, jnp.uint32(16))


def _unpack_bf16_pair(w):
    lo = lax.bitcast_convert_type(lax.shift_left(w, jnp.uint32(16)), F32)
    hi = lax.bitcast_convert_type(w & jnp.uint32(0xFFFF0000), F32)
    return lo, hi


ROW_CHUNKS = 4


def _chunk_copies(buf2d, rows_hbm, r0, nrows, sem, *, to_hbm):
    cps = []
    for c in range(ROW_CHUNKS):
        v = buf2d.at[:, pl.ds(c * LANES, LANES)]
        h = rows_hbm.at[pl.ds(r0, nrows), c]
        cps.append(pltpu.make_async_copy(v, h, sem) if to_hbm else pltpu.make_async_copy(h, v, sem))
    return cps


def _sigmoid(x):
    return 1.0 / (1.0 + jnp.exp(-x))


def _rms(x):
    return x * lax.rsqrt(jnp.mean(x * x, axis=-1, keepdims=True) + NORM_EPS)


def _mod_kernel(c_ref, w_ref, b_ref, o_ref):
    c = c_ref[...]
    cond = c * _sigmoid(c)
    o_ref[...] = _dot_f32ish(cond, w_ref[...]) + b_ref[...]


def _modulation(c, w_mod, b_mod):
    depth, d, d6 = w_mod.shape
    bsz = c.shape[0]
    nj = d6 // d
    return pl.pallas_call(
        _mod_kernel,
        out_shape=jax.ShapeDtypeStruct((depth, nj, bsz, d), F32),
        grid=(depth, nj),
        in_specs=[
            pl.BlockSpec((bsz, d), lambda l, j: (0, 0)),
            pl.BlockSpec((None, d, d), lambda l, j: (l, 0, j)),
            pl.BlockSpec((None, 1, d), lambda l, j: (l, 0, j)),
        ],
        out_specs=pl.BlockSpec((None, None, bsz, d), lambda l, j: (l, j, 0, 0)),
        compiler_params=_cparams("arbitrary", "arbitrary"),
        name="mod",
    )(c, w_mod, b_mod.reshape(depth, 1, d6))


def _mla_pre_kernel(x_ref, pos_ref, sh_ref, sc_ref, g_ref, win_ref, gq_ref, gkv_ref, wq_ref, wkv_ref,
                    invf_ref, q_ref, k_ref, v_ref):
    x = x_ref[...]
    h = _rms(x) * g_ref[...] * (1.0 + sc_ref[...]) + sh_ref[...]
    lat = _dot(h.astype(BF16), win_ref[...])
    q_lat = _rms(lat[:, :Q_LORA]) * gq_ref[...]
    kv_lat = _rms(lat[:, Q_LORA:Q_LORA + KV_LORA]) * gkv_ref[...]
    ang = pos_ref[...] * invf_ref[...]
    cs = jnp.cos(ang)
    sn = jnp.sin(ang)
    o = Q_LORA + KV_LORA
    k_rope = lat[:, o:o + LANES] * cs + lat[:, o + LANES:o + 2 * LANES] * sn
    qq = _dot(q_lat.astype(BF16), wq_ref[...])
    kv = _dot(kv_lat.astype(BF16), wkv_ref[...])
    rot0 = MLA_HEADS * 2 * LANES
    k_rope_b = k_rope[:, :QK_ROPE].astype(BF16)
    for hd in range(MLA_HEADS):
        c0 = hd * 2 * LANES
        q_nope = qq[:, c0:c0 + LANES] * MLA_SCALE
        q_rope = (qq[:, c0 + LANES:c0 + 2 * LANES] * cs
                  + qq[:, rot0 + hd * LANES:rot0 + (hd + 1) * LANES] * sn) * MLA_SCALE
        q_ref[hd, :, 0:QK_NOPE] = q_nope.astype(BF16)
        q_ref[hd, :, QK_NOPE:QK_HEAD] = q_rope[:, :QK_ROPE].astype(BF16)
        k_ref[hd, :, 0:QK_NOPE] = kv[:, c0:c0 + LANES].astype(BF16)
        k_ref[hd, :, QK_NOPE:QK_HEAD] = k_rope_b
        v_ref[hd] = kv[:, c0 + LANES:c0 + 2 * LANES].astype(BF16)


def _rot_half_cols(w):
    half = QK_ROPE // 2
    return jnp.concatenate([-w[..., half:], w[..., :half]], axis=-1)


def _mla_pre(x2d, pos_f, sh, sc, g, w_in, g_q, g_kv, w_q_up, w_kv_up, bsz, seqlen, tm):
    n, d = x2d.shape
    hh = MLA_HEADS
    o = Q_LORA + KV_LORA
    wr = w_in[:, o:o + QK_ROPE]
    zpad = jnp.zeros((d, LANES - QK_ROPE), F32)
    w_in_ext = jnp.concatenate([w_in[:, :o], wr, zpad, _rot_half_cols(wr), zpad], axis=1).astype(BF16)
    wq = w_q_up.reshape(Q_LORA, hh, QK_HEAD)
    zq = jnp.zeros((Q_LORA, hh, LANES - QK_ROPE), F32)
    wq_main = jnp.concatenate([wq, zq], axis=-1).reshape(Q_LORA, hh * 2 * LANES)
    wq_rot = jnp.concatenate([_rot_half_cols(wq[..., QK_NOPE:]), zq], axis=-1).reshape(Q_LORA, hh * LANES)
    wq_ext = jnp.concatenate([wq_main, wq_rot], axis=1).astype(BF16)
    inv_freq = 1.0 / (ROPE_THETA ** (jnp.arange(0, QK_ROPE, 2, dtype=F32) / QK_ROPE))
    invf = jnp.concatenate([inv_freq, inv_freq, jnp.zeros((LANES - QK_ROPE,), F32)]).reshape(1, LANES)
    tpb = seqlen // tm
    vec = lambda i: (0, 0)
    outs = pl.pallas_call(
        _mla_pre_kernel,
        out_shape=(
            jax.ShapeDtypeStruct((bsz, hh, seqlen, QK_HEAD), BF16),
            jax.ShapeDtypeStruct((bsz, hh, seqlen, QK_HEAD), BF16),
            jax.ShapeDtypeStruct((bsz, hh, seqlen, V_HEAD), BF16),
        ),
        grid=(n // tm,),
        in_specs=[
            pl.BlockSpec((tm, d), lambda i: (i, 0)),
            pl.BlockSpec((tm, 1), lambda i: (i, 0)),
            pl.BlockSpec((None, 1, d), lambda i: (i // tpb, 0, 0)),
            pl.BlockSpec((None, 1, d), lambda i: (i // tpb, 0, 0)),
            pl.BlockSpec((1, d), vec),
            pl.BlockSpec(w_in_ext.shape, vec),
            pl.BlockSpec((1, Q_LORA), vec),
            pl.BlockSpec((1, KV_LORA), vec),
            pl.BlockSpec(wq_ext.shape, vec),
            pl.BlockSpec((KV_LORA, hh * 2 * LANES), vec),
            pl.BlockSpec((1, LANES), vec),
        ],
        out_specs=(
            pl.BlockSpec((None, hh, tm, QK_HEAD), lambda i: (i // tpb, 0, i % tpb, 0)),
            pl.BlockSpec((None, hh, tm, QK_HEAD), lambda i: (i // tpb, 0, i % tpb, 0)),
            pl.BlockSpec((None, hh, tm, V_HEAD), lambda i: (i // tpb, 0, i % tpb, 0)),
        ),
        compiler_params=_cparams("arbitrary"),
        name="mla_pre",
    )(x2d, pos_f, sh, sc, g.reshape(1, d), w_in_ext, g_q.reshape(1, -1), g_kv.reshape(1, -1), wq_ext,
      w_kv_up.astype(BF16), invf)
    return outs


ATTN_HEADS_PER_STEP = 4


def _attn_kernel(q_ref, k_ref, v_ref, o_ref, m_scr, acc_scr, *, tq, hp):
    qi = pl.program_id(2)
    m_scr[...] = jnp.full(m_scr.shape, -jnp.inf, F32)
    acc_scr[...] = jnp.zeros(acc_scr.shape, F32)

    def block(hd, r0, tk, masked):
        k = k_ref[hd, pl.ds(r0, tk), :]
        v_ext = jnp.concatenate([v_ref[hd, pl.ds(r0, tk), :], jnp.ones((tk, V_HEAD), BF16)], axis=1)
        s = _dot_nt(q_ref[hd], k)
        if masked:
            row = lax.broadcasted_iota(I32, (tq, tk), 0)
            col = lax.broadcasted_iota(I32, (tq, tk), 1)
            s = jnp.where(row + (tk - tq) >= col, s, -jnp.inf)
        m_prev = m_scr[hd]
        m_new = jnp.maximum(m_prev, jnp.max(s, axis=-1, keepdims=True))
        alpha = jnp.exp(m_prev - m_new)
        p = jnp.exp(s - m_new).astype(BF16)
        acc_scr[hd] = alpha * acc_scr[hd] + _dot(p, v_ext)
        m_scr[hd] = m_new

    def body(j, carry):
        for hd in range(hp):
            block(hd, pl.multiple_of(j * (2 * tq), 2 * tq), 2 * tq, False)
        return carry

    lax.fori_loop(0, qi // 2, body, 0)

    @pl.when(qi % 2 == 1)
    def _():
        for hd in range(hp):
            block(hd, pl.multiple_of((qi - 1) * tq, tq), 2 * tq, True)

    @pl.when(qi % 2 == 0)
    def _():
        for hd in range(hp):
            block(hd, pl.multiple_of(qi * tq, tq), tq, True)

    for hd in range(hp):
        acc = acc_scr[hd]
        o_ref[:, hd * V_HEAD:(hd + 1) * V_HEAD] = (acc[:, :V_HEAD] / acc[:, V_HEAD:]).astype(o_ref.dtype)


def _attention(q, k, v, tq):
    bsz, hh, seqlen, _ = q.shape
    hp = ATTN_HEADS_PER_STEP
    return pl.pallas_call(
        functools.partial(_attn_kernel, tq=tq, hp=hp),
        out_shape=jax.ShapeDtypeStruct((bsz, seqlen, hh * V_HEAD), BF16),
        grid=(bsz, hh // hp, seqlen // tq),
        in_specs=[
            pl.BlockSpec((None, hp, tq, QK_HEAD), lambda b, h, i: (b, h, i, 0)),
            pl.BlockSpec((None, hp, seqlen, QK_HEAD), lambda b, h, i: (b, h, 0, 0)),
            pl.BlockSpec((None, hp, seqlen, V_HEAD), lambda b, h, i: (b, h, 0, 0)),
        ],
        out_specs=pl.BlockSpec((None, tq, hp * V_HEAD), lambda b, h, i: (b, i, h)),
        scratch_shapes=[
            pltpu.VMEM((hp, tq, 1), F32),
            pltpu.VMEM((hp, tq, 2 * V_HEAD), F32),
        ],
        compiler_params=_cparams("arbitrary", "arbitrary", "arbitrary"),
        name="attn",
    )(q, k, v)


def _post_mixer_kernel(x_ref, o_ref, wout_ref, gt1_ref, sh_ref, sc_ref, g_ref, wr_ref, br_ref,
                       x1_ref, h2_hbm, meta_ref, gate_ref, cnt_ref, carry_scr, pbuf, psem, *, tm):
    i = pl.program_id(0)

    @pl.when(i == 0)
    def _():
        carry_scr[...] = jnp.zeros(carry_scr.shape, F32)

    y = _dot(o_ref[...], wout_ref[...])
    x1 = x_ref[...] + gt1_ref[...] * y
    x1_ref[...] = x1
    h2 = _rms(x1) * g_ref[...] * (1.0 + sc_ref[...]) + sh_ref[...]
    half = h2.shape[1] // 2

    @pl.when(i > 0)
    def _():
        for cp in _chunk_copies(pbuf, h2_hbm, (i - 1) * tm, tm, psem, to_hbm=True):
            cp.wait()

    pbuf[...] = _pack_bf16_pair(h2[:, :half], h2[:, half:])
    for cp in _chunk_copies(pbuf, h2_hbm, i * tm, tm, psem, to_hbm=True):
        cp.start()

    logits = _dot_f32ish(h2, wr_ref[...]) + br_ref[...]
    lane = lax.broadcasted_iota(I32, (tm, LANES), 1).astype(F32)
    work = logits
    idxs, vals = [], []
    for _ in range(TOP_K):
        mx = jnp.max(work, axis=-1, keepdims=True)
        idx = jnp.min(jnp.where(work == mx, lane, float(LANES)), axis=-1, keepdims=True)
        idxs.append(idx)
        vals.append(mx)
        work = jnp.where(lane == idx, -jnp.inf, work)
    exps = [jnp.exp(vk - vals[0]) for vk in vals]
    denom = exps[0] + exps[1] + exps[2] + exps[3]
    onehot = jnp.zeros((tm, LANES), F32)
    for idx in idxs:
        onehot = onehot + jnp.where(lane == idx, 1.0, 0.0)
    row = lax.broadcasted_iota(I32, (tm, tm), 0)
    col = lax.broadcasted_iota(I32, (tm, tm), 1)
    ltri = jnp.where(row > col, 1.0, 0.0).astype(BF16)
    cum = _dot(ltri, onehot.astype(BF16)) + carry_scr[...]
    meta = jnp.zeros((tm, LANES), F32)
    gates = jnp.zeros((tm, LANES), F32)
    for kk in range(TOP_K):
        rank = jnp.sum(jnp.where(lane == idxs[kk], cum, 0.0), axis=-1, keepdims=True)
        meta = jnp.where(lane == float(kk), idxs[kk], meta)
        meta = jnp.where(lane == float(TOP_K + kk), rank, meta)
        gates = jnp.where(lane == float(kk), exps[kk] / denom, gates)
    meta_ref[...] = jnp.transpose(meta)[:2 * TOP_K, :].astype(I32)
    gate_ref[...] = gates
    carry = carry_scr[...] + jnp.sum(onehot, axis=0, keepdims=True)
    carry_scr[...] = carry
    cnt_ref[...] = carry.astype(I32)

    @pl.when(i == pl.num_programs(0) - 1)
    def _():
        for cp in _chunk_copies(pbuf, h2_hbm, i * tm, tm, psem, to_hbm=True):
            cp.wait()


def _post_mixer(x2d, o2d, w_out, gt1, sh2, sc2, g_ffn, w_router, b_router, seqlen, tm):
    n, d = x2d.shape
    kdim = o2d.shape[1]
    tpb = seqlen // tm
    wr = jnp.concatenate([w_router, jnp.zeros((d, LANES - N_EXPERTS), F32)], axis=1)
    br = jnp.concatenate([b_router, jnp.full((LANES - N_EXPERTS,), -1e30, F32)]).reshape(1, LANES)
    vec = lambda i: (0, 0)
    bvec = lambda i: (i // tpb, 0, 0)
    return pl.pallas_call(
        functools.partial(_post_mixer_kernel, tm=tm),
        out_shape=(
            jax.ShapeDtypeStruct((n, d), F32),
            jax.ShapeDtypeStruct((n, ROW_CHUNKS, LANES), U32),
            jax.ShapeDtypeStruct((2 * TOP_K, n), I32),
            jax.ShapeDtypeStruct((n, LANES), F32),
            jax.ShapeDtypeStruct((1, LANES), I32),
        ),
        grid=(n // tm,),
        in_specs=[
            pl.BlockSpec((tm, d), lambda i: (i, 0)),
            pl.BlockSpec((tm, kdim), lambda i: (i, 0)),
            pl.BlockSpec((kdim, d), vec),
            pl.BlockSpec((None, 1, d), bvec),
            pl.BlockSpec((None, 1, d), bvec),
            pl.BlockSpec((None, 1, d), bvec),
            pl.BlockSpec((1, d), vec),
            pl.BlockSpec((d, LANES), vec),
            pl.BlockSpec((1, LANES), vec),
        ],
        out_specs=(
            pl.BlockSpec((tm, d), lambda i: (i, 0)),
            pl.BlockSpec(memory_space=pl.ANY),
            pl.BlockSpec((2 * TOP_K, tm), lambda i: (0, i)),
            pl.BlockSpec((tm, LANES), lambda i: (i, 0)),
            pl.BlockSpec((1, LANES), vec),
        ),
        scratch_shapes=[pltpu.VMEM((1, LANES), F32), pltpu.VMEM((tm, d // 2), U32), pltpu.SemaphoreType.DMA],
        compiler_params=_cparams("arbitrary"),
        name="post_mixer",
    )(x2d, o2d, w_out.astype(BF16), gt1, sh2, sc2, g_ffn.reshape(1, d), wr, br)


ROW_UNROLL = 8


def _row_copy(src, s, dst, t, sem):
    return pltpu.make_async_copy(src.at[s], dst.at[t], sem)


def _dest_kernel(pstart_ref, meta_ref, dest_ref):
    e = meta_ref[0:TOP_K, :]
    dest = meta_ref[TOP_K:2 * TOP_K, :]
    for j in range(N_EXPERTS):
        dest = dest + jnp.where(e == j, pstart_ref[j], 0)
    dest_ref[...] = dest


def _dest_rows(meta_t, pstart):
    n = meta_t.shape[1]
    tn = min(4096, n)
    grid_spec = pltpu.PrefetchScalarGridSpec(
        num_scalar_prefetch=1,
        grid=(n // tn,),
        in_specs=[pl.BlockSpec((2 * TOP_K, tn), lambda i, ps: (0, i))],
        out_specs=pl.BlockSpec((TOP_K, tn), lambda i, ps: (0, i)),
    )
    return pl.pallas_call(
        _dest_kernel,
        out_shape=jax.ShapeDtypeStruct((TOP_K, n), I32),
        grid_spec=grid_spec,
        compiler_params=_cparams("arbitrary"),
        name="dest_rows",
    )(pstart, meta_t)


def _dispatch_kernel(pstart_ref, cnt_ref, dest_ref, h_ref, xs_hbm, zbuf, sem, zsem, *, tm):
    i = pl.program_id(0)

    @pl.when(i == 0)
    def _():
        zbuf[...] = jnp.zeros(zbuf.shape, U32)

        def per_expert(e, carry):
            lo = pstart_ref[e] + cnt_ref[e]
            hi = pstart_ref[e + 1]

            def start(j, c):
                _row_copy(zbuf, 0, xs_hbm, j, zsem).start()
                return c

            def wait(j, c):
                _row_copy(zbuf, 0, xs_hbm, j, zsem).wait()
                return c

            lax.fori_loop(lo, hi, start, 0)
            lax.fori_loop(lo, hi, wait, 0)
            return carry

        lax.fori_loop(0, N_EXPERTS, per_expert, 0)

    def start(tt, c):
        for u in range(ROW_UNROLL):
            t = tt * ROW_UNROLL + u
            for kk in range(TOP_K):
                _row_copy(h_ref, t, xs_hbm, dest_ref[kk, t], sem).start(priority=kk % 2)
        return c

    def wait(tt, c):
        for u in range(ROW_UNROLL * TOP_K):
            _row_copy(h_ref, 0, xs_hbm, 0, sem).wait()
        return c

    lax.fori_loop(0, tm // ROW_UNROLL, start, 0)
    lax.fori_loop(0, tm // ROW_UNROLL, wait, 0)


def _dispatch(h2p, dest_t, pstart, counts, n_rows, tm):
    n = h2p.shape[0]
    grid_spec = pltpu.PrefetchScalarGridSpec(
        num_scalar_prefetch=2,
        grid=(n // tm,),
        in_specs=[
            pl.BlockSpec((TOP_K, tm), lambda i, ps, cn: (0, i), memory_space=pltpu.SMEM),
            pl.BlockSpec((tm, ROW_CHUNKS, LANES), lambda i, ps, cn: (i, 0, 0)),
        ],
        out_specs=pl.BlockSpec(memory_space=pl.ANY),
        scratch_shapes=[
            pltpu.VMEM((1, ROW_CHUNKS, LANES), U32),
            pltpu.SemaphoreType.DMA,
            pltpu.SemaphoreType.DMA,
        ],
    )
    return pl.pallas_call(
        functools.partial(_dispatch_kernel, tm=tm),
        out_shape=jax.ShapeDtypeStruct((n_rows, ROW_CHUNKS, LANES), U32),
        grid_spec=grid_spec,
        compiler_params=pltpu.CompilerParams(dimension_semantics=("arbitrary",), vmem_limit_bytes=VMEM_LIMIT,
                                             has_side_effects=True),
        name="dispatch",
    )(pstart, counts, dest_t, h2p)


def _ffn_kernel(be_ref, nu_ref, first_ref, wslot_ref, nxt_ref, xs_hbm, wgu_hbm, bgu_ref, wd_hbm, bd_ref, ys_hbm,
                wgu_b, wd_b, wgu_f, wd_f, xbuf, ybuf, xsem, ysem, wsem, *, ff, layer):
    i = pl.program_id(0)
    nu = nu_ref[0]
    slot = i % 2

    def w_copies(e, s):
        return [pltpu.make_async_copy(wgu_hbm.at[layer, e], wgu_f.at[s], wsem.at[s]),
                pltpu.make_async_copy(wd_hbm.at[layer, e], wd_f.at[s], wsem.at[s])]

    @pl.when(i == 0)
    def _():
        for cp in w_copies(be_ref[0], 0):
            cp.start()

    def x_copies(step, s):
        return _chunk_copies(xbuf.at[s], xs_hbm, step * MOE_BLOCK, MOE_BLOCK, xsem.at[s], to_hbm=False)

    def y_copies(step, s):
        return _chunk_copies(ybuf.at[s], ys_hbm, step * MOE_BLOCK, MOE_BLOCK, ysem.at[s], to_hbm=True)

    @pl.when(i == 0)
    def _():
        for cp in x_copies(0, 0):
            cp.start()

    @pl.when(i + 1 < nu)
    def _():
        for cp in x_copies(i + 1, 1 - slot):
            cp.start()

    @pl.when(first_ref[i] == 1)
    def _():
        ws = wslot_ref[i]
        for cp in w_copies(be_ref[i], ws):
            cp.wait()

        @pl.when(nxt_ref[i] >= 0)
        def _():
            for cp in w_copies(nxt_ref[i], 1 - ws):
                cp.start(priority=1)

        wgu_b[...] = wgu_f[ws].astype(BF16)
        wd_b[...] = wd_f[ws].astype(BF16)

    @pl.when(i < nu)
    def _():
        for cp in x_copies(i, slot):
            cp.wait()
        x_lo, x_hi = _unpack_bf16_pair(xbuf[slot])
        dh = x_lo.shape[1]
        gu = (_dot(x_lo.astype(BF16), wgu_b[0:dh, :]) + _dot(x_hi.astype(BF16), wgu_b[dh:2 * dh, :])
              + bgu_ref[...])
        g = jnp.minimum(gu[:, :ff], SWIGLU_LIMIT)
        lin = jnp.clip(gu[:, ff:], -SWIGLU_LIMIT, SWIGLU_LIMIT)
        act = g * _sigmoid(SWIGLU_ALPHA * g) * (lin + 1.0)
        y = _dot(act.astype(BF16), wd_b[...]) + bd_ref[...]

        @pl.when(i >= 2)
        def _():
            for cp in y_copies(i - 2, slot):
                cp.wait()

        ybuf[slot] = _pack_bf16_pair(y[:, :dh], y[:, dh:])
        for cp in y_copies(i, slot):
            cp.start()

    @pl.when(i == nu - 1)
    def _():
        for cp in y_copies(i, slot):
            cp.wait()

        @pl.when(i >= 1)
        def _():
            for cp in y_copies(i - 1, 1 - slot):
                cp.wait()


def _ffn(xs, blk_e, n_used, pad_end, w_gate_up, b_gate_up, w_down, b_down, layer):
    n_rows = xs.shape[0]
    depth, ne, d, ff2 = w_gate_up.shape
    ff = ff2 // 2
    n_blk = n_rows // MOE_BLOCK
    blk = jnp.arange(n_blk, dtype=I32)
    active = blk < n_used[0]
    first = (active & ((blk == 0) | (blk_e != jnp.roll(blk_e, 1)))).astype(I32)
    wslot = ((jnp.cumsum(first) - 1) % 2).astype(I32)
    nxt_blk = pad_end[blk_e] // MOE_BLOCK
    nxt = jnp.where(nxt_blk < n_used[0], blk_e[jnp.minimum(nxt_blk, n_blk - 1)], -1).astype(I32)
    row_buf = pltpu.VMEM((2, MOE_BLOCK, ROW_CHUNKS * LANES), U32)
    sp = lambda i, be, nu, fi, ws, nx: (layer, be[i], 0, 0)
    grid_spec = pltpu.PrefetchScalarGridSpec(
        num_scalar_prefetch=5,
        grid=(n_blk,),
        in_specs=[
            pl.BlockSpec(memory_space=pl.ANY),
            pl.BlockSpec(memory_space=pl.ANY),
            pl.BlockSpec((None, None, 1, ff2), sp),
            pl.BlockSpec(memory_space=pl.ANY),
            pl.BlockSpec((None, None, 1, d), sp),
        ],
        out_specs=pl.BlockSpec(memory_space=pl.ANY),
        scratch_shapes=[pltpu.VMEM((d, ff2), BF16), pltpu.VMEM((ff, d), BF16),
                        pltpu.VMEM((2, d, ff2), F32), pltpu.VMEM((2, ff, d), F32), row_buf, row_buf,
                        pltpu.SemaphoreType.DMA((2,)), pltpu.SemaphoreType.DMA((2,)),
                        pltpu.SemaphoreType.DMA((2,))],
    )
    return pl.pallas_call(
        functools.partial(_ffn_kernel, ff=ff, layer=layer),
        out_shape=jax.ShapeDtypeStruct((n_rows, ROW_CHUNKS, LANES), U32),
        grid_spec=grid_spec,
        compiler_params=_cparams("arbitrary"),
        name="ffn",
    )(blk_e, n_used, first, wslot, nxt, xs, w_gate_up, b_gate_up.reshape(depth, ne, 1, ff2), w_down,
      b_down.reshape(depth, ne, 1, d))


def _gather_scratch(tm):
    return [pltpu.VMEM((2, TOP_K, tm // ROW_UNROLL, ROW_CHUNKS, ROW_UNROLL, LANES), U32),
            pltpu.SemaphoreType.DMA((2,))]


def _gather_start(dest_ref, ys_hbm, buf, sem, tm, both_queues):
    def body(tt, c):
        for u in range(ROW_UNROLL):
            for kk in range(TOP_K):
                src = ys_hbm.at[dest_ref[kk, tt * ROW_UNROLL + u]]
                pltpu.make_async_copy(src, buf.at[kk, tt, :, u], sem).start(priority=kk % 2 if both_queues else 1)
        return c

    lax.fori_loop(0, tm // ROW_UNROLL, body, 0)


def _gather_wait(ys_hbm, buf, sem, tm):
    def body(tt, c):
        for u in range(ROW_UNROLL * TOP_K):
            pltpu.make_async_copy(ys_hbm.at[0], buf.at[0, 0, :, 0], sem).wait()
        return c

    lax.fori_loop(0, tm // ROW_UNROLL, body, 0)


def _combine_rows(dest0_ref, dest_next_ref, gate_ref, ys_hbm, buf, sem, tm, both_queues):
    i = pl.program_id(0)
    slot = i % 2

    @pl.when(i == 0)
    def _():
        _gather_start(dest0_ref, ys_hbm, buf.at[0], sem.at[0], tm, both_queues)

    @pl.when(i + 1 < pl.num_programs(0))
    def _():
        _gather_start(dest_next_ref, ys_hbm, buf.at[1 - slot], sem.at[1 - slot], tm, both_queues)

    _gather_wait(ys_hbm, buf.at[slot], sem.at[slot], tm)
    gates = gate_ref[...]
    acc_lo = acc_hi = None
    for kk in range(TOP_K):
        rows = jnp.concatenate([buf[slot, kk, :, c, :, :].reshape(tm, LANES) for c in range(ROW_CHUNKS)], axis=1)
        lo, hi = _unpack_bf16_pair(rows)
        gk = gates[:, kk:kk + 1]
        acc_lo = gk * lo if acc_lo is None else acc_lo + gk * lo
        acc_hi = gk * hi if acc_hi is None else acc_hi + gk * hi
    return jnp.concatenate([acc_lo, acc_hi], axis=1)


def _combine_specs(tm, nsteps, idx):
    return [pl.BlockSpec((TOP_K, tm), lambda *a: (0, 0), memory_space=pltpu.SMEM),
            pl.BlockSpec((TOP_K, tm), lambda *a: (0, jnp.minimum(idx(*a) + 1, nsteps - 1)), memory_space=pltpu.SMEM)]


def _combine_kernel(dest0_ref, dest_next_ref, x1_ref, gate_ref, gt2_ref, gfin_ref, ys_hbm, o_ref, buf, sem,
                    *, tm, final):
    y = _combine_rows(dest0_ref, dest_next_ref, gate_ref, ys_hbm, buf, sem, tm, both_queues=True)
    x2 = x1_ref[...] + gt2_ref[...] * y
    if final:
        x2 = _rms(x2) * gfin_ref[...]
    o_ref[...] = x2


def _combine(x1, gates, gt2, g_final, ys, dest_t, seqlen, tm, final):
    n, d = x1.shape
    tpb = seqlen // tm
    grid_spec = pltpu.PrefetchScalarGridSpec(
        num_scalar_prefetch=0,
        grid=(n // tm,),
        in_specs=_combine_specs(tm, n // tm, lambda i: i) + [
            pl.BlockSpec((tm, d), lambda i: (i, 0)),
            pl.BlockSpec((tm, LANES), lambda i: (i, 0)),
            pl.BlockSpec((None, 1, d), lambda i: (i // tpb, 0, 0)),
            pl.BlockSpec((1, d), lambda i: (0, 0)),
            pl.BlockSpec(memory_space=pl.ANY),
        ],
        out_specs=pl.BlockSpec((tm, d), lambda i: (i, 0)),
        scratch_shapes=_gather_scratch(tm),
    )
    return pl.pallas_call(
        functools.partial(_combine_kernel, tm=tm, final=final),
        out_shape=jax.ShapeDtypeStruct((n, d), F32),
        grid_spec=grid_spec,
        compiler_params=_cparams("arbitrary"),
        name="combine",
    )(dest_t, dest_t, x1, gates, gt2, g_final.reshape(1, d), ys)


def _moe(x1, h2p, meta_t, gates, counts, gt2, g_final, w_gate_up, b_gate_up, w_down, b_down, layer, seqlen,
         final, defer_combine):
    n, d = x1.shape
    n_pair = n * TOP_K
    n_rows = -(-n_pair // MOE_BLOCK) * MOE_BLOCK + N_EXPERTS * MOE_BLOCK
    n_blk = n_rows // MOE_BLOCK
    cnt = counts[0, :N_EXPERTS]
    padded = (cnt + MOE_BLOCK - 1) // MOE_BLOCK * MOE_BLOCK
    pad_end = jnp.cumsum(padded)
    pstart = jnp.concatenate([pad_end - padded, pad_end[-1:]]).astype(I32)
    n_used = (pad_end[-1:] // MOE_BLOCK).astype(I32)
    blk_start = jnp.arange(n_blk, dtype=I32) * MOE_BLOCK
    blk_e = jnp.minimum(jnp.sum(blk_start[:, None] >= pad_end[None, :], axis=1), N_EXPERTS - 1).astype(I32)
    tm = min(256, seqlen)
    dest_t = _dest_rows(meta_t, pstart)
    xs = _dispatch(h2p, dest_t, pstart, cnt, n_rows, tm)
    ys = _ffn(xs, blk_e, n_used, pad_end.astype(I32), w_gate_up, b_gate_up, w_down, b_down, layer)
    if defer_combine:
        return x1, gates, gt2, ys, dest_t
    return _combine(x1, gates, gt2, g_final, ys, dest_t, seqlen, tm, final)


def _softplus(x):
    return jnp.maximum(x, 0.0) + jnp.log(1.0 + jnp.exp(-jnp.abs(x)))


SSD_TILE = 256


def _ssd_pre_kernel(dest0_ref, dest_next_ref, x1_ref, gate_ref, gt2_ref, ys_hbm,
                    sh_ref, sc_ref, g_ref, wz_ref, wx_ref, wdt_ref, wdtT_ref, cw_ref, cb_ref,
                    dtb_ref, dtbT_ref, a_ref, aT_ref,
                    x_out_ref, z_ref, xbc_ref, dt_ref, adt_ref, adtT_ref, ubuf, gbuf, gsem, *, tm, tpb, cchunk):
    i = pl.program_id(0)
    x = x1_ref[...] + gt2_ref[...] * _combine_rows(dest0_ref, dest_next_ref, gate_ref, ys_hbm, gbuf, gsem, tm,
                                                     both_queues=False)
    x_out_ref[...] = x
    h = _rms(x) * g_ref[...] * (1.0 + sc_ref[...]) + sh_ref[...]
    hb = h.astype(BF16)
    z_ref[...] = _dot(hb, wz_ref[...]).astype(BF16)
    dt = _softplus(_dot(hb, wdt_ref[...]) + dtb_ref[...])
    dt_ref[...] = dt
    adt_ref[...] = dt * a_ref[...]
    dtT = _softplus(_dot_nt(wdtT_ref[...], hb) + dtbT_ref[...])
    adtT_ref[...] = dtT * aT_ref[...]

    @pl.when(i % tpb == 0)
    def _():
        ubuf[...] = jnp.zeros(ubuf.shape, F32)

    for c0 in range(0, SSM_CONV_DIM, cchunk):
        cols = slice(c0, c0 + cchunk)
        u = _dot(hb, wx_ref[:, cols])
        ext = jnp.concatenate([ubuf[:, cols], u], axis=0)
        acc = cb_ref[:, cols] + cw_ref[SSM_CONV - 1:SSM_CONV, cols] * u
        for j in range(1, SSM_CONV):
            shifted = pltpu.roll(ext, j, axis=0)[8:, :]
            acc = acc + cw_ref[SSM_CONV - 1 - j:SSM_CONV - j, cols] * shifted
        xbc_ref[:, cols] = (acc * _sigmoid(acc)).astype(BF16)
        ubuf[:, cols] = u[tm - 8:, :]


def _ssd_pre(moe_out, sh, sc, g, w_in, conv_w, conv_b, dt_bias, a_log, bsz, seqlen, tm):
    x1, gates, gt2, ys, dest_t = moe_out
    n, d = x1.shape
    tm = min(tm, seqlen)
    tpb = seqlen // tm
    nh = SSM_HEADS
    wz = w_in[:, :SSM_INNER].astype(BF16)
    wx = w_in[:, SSM_INNER:SSM_INNER + SSM_CONV_DIM].astype(BF16)
    wdt_raw = w_in[:, SSM_INNER + SSM_CONV_DIM:]
    wdt = jnp.concatenate([wdt_raw, jnp.zeros((d, LANES - nh), F32)], axis=1).astype(BF16)
    wdtT = wdt_raw.T.astype(BF16)
    pad = jnp.zeros((LANES - nh,), F32)
    dtb = jnp.concatenate([dt_bias, pad]).reshape(1, LANES)
    a_neg = -jnp.exp(a_log.astype(F32))
    a_row = jnp.concatenate([a_neg, pad]).reshape(1, LANES)
    vec = lambda i: (0, 0)
    bvec = lambda i: (i // tpb, 0, 0)
    return pl.pallas_call(
        functools.partial(_ssd_pre_kernel, tm=tm, tpb=tpb, cchunk=1024),
        out_shape=(
            jax.ShapeDtypeStruct((n, d), F32),
            jax.ShapeDtypeStruct((n, SSM_INNER), BF16),
            jax.ShapeDtypeStruct((n, SSM_CONV_DIM), BF16),
            jax.ShapeDtypeStruct((n, LANES), F32),
            jax.ShapeDtypeStruct((n, LANES), F32),
            jax.ShapeDtypeStruct((bsz, nh, seqlen), F32),
        ),
        grid=(n // tm,),
        in_specs=_combine_specs(tm, n // tm, lambda i: i) + [
            pl.BlockSpec((tm, d), lambda i: (i, 0)),
            pl.BlockSpec((tm, LANES), lambda i: (i, 0)),
            pl.BlockSpec((None, 1, d), bvec),
            pl.BlockSpec(memory_space=pl.ANY),
            pl.BlockSpec((None, 1, d), bvec),
            pl.BlockSpec((None, 1, d), bvec),
            pl.BlockSpec((1, d), vec),
            pl.BlockSpec((d, SSM_INNER), vec),
            pl.BlockSpec((d, SSM_CONV_DIM), vec),
            pl.BlockSpec((d, LANES), vec),
            pl.BlockSpec((nh, d), vec),
            pl.BlockSpec((SSM_CONV, SSM_CONV_DIM), vec),
            pl.BlockSpec((1, SSM_CONV_DIM), vec),
            pl.BlockSpec((1, LANES), vec),
            pl.BlockSpec((nh, 1), vec),
            pl.BlockSpec((1, LANES), vec),
            pl.BlockSpec((nh, 1), vec),
        ],
        out_specs=(
            pl.BlockSpec((tm, d), lambda i: (i, 0)),
            pl.BlockSpec((tm, SSM_INNER), lambda i: (i, 0)),
            pl.BlockSpec((tm, SSM_CONV_DIM), lambda i: (i, 0)),
            pl.BlockSpec((tm, LANES), lambda i: (i, 0)),
            pl.BlockSpec((tm, LANES), lambda i: (i, 0)),
            pl.BlockSpec((None, nh, tm), lambda i: (i // tpb, 0, i % tpb)),
        ),
        scratch_shapes=[pltpu.VMEM((8, SSM_CONV_DIM), F32)] + _gather_scratch(tm),
        compiler_params=_cparams("arbitrary"),
        name="ssd_pre",
    )(dest_t, dest_t, x1, gates, gt2, ys, sh, sc, g.reshape(1, d), wz, wx, wdt, wdtT, conv_w,
      conv_b.reshape(1, -1), dtb, dt_bias.reshape(nh, 1), a_row, a_neg.reshape(nh, 1))


def _expand_heads(v, g, rows):
    lane = lax.broadcasted_iota(I32, (rows, LANES), 1)
    lo = lane < SSM_HEAD_DIM
    parts = []
    for j in range(0, SSM_HPG, 2):
        h0 = g * SSM_HPG + j
        parts.append(jnp.where(lo, v[:, h0:h0 + 1], v[:, h0 + 1:h0 + 2]))
    return jnp.concatenate(parts, axis=1)


def _ssd_scan_kernel(xbc_ref, z_ref, dt_ref, adt_ref, adtT_ref, d_ref, gn_ref, hx_ref, yn_ref, state, *, lc):
    c = pl.program_id(1)

    @pl.when(c == 0)
    def _():
        state[...] = jnp.zeros(state.shape, F32)

    row = lax.broadcasted_iota(I32, (lc, lc), 0)
    col = lax.broadcasted_iota(I32, (lc, lc), 1)
    causal = row >= col
    tri = jnp.where(causal, 1.0, 0.0).astype(BF16)
    triT = jnp.where(row <= col, 1.0, 0.0).astype(BF16)
    ah, am, al = _split3(adt_ref[...])
    a_cum = _dot(tri, ah) + (_dot(tri, am) + _dot(tri, al))
    th, tmid, tl = _split3(adtT_ref[...])
    a_cumT = _dot(th, triT) + (_dot(tmid, triT) + _dot(tl, triT))
    dt = dt_ref[...]
    a_last = a_cum[lc - 1:lc, :]
    e_cum = jnp.exp(a_cum)
    d2e = jnp.exp(a_last - a_cum)
    cdec = jnp.exp(a_last)
    dskip = d_ref[...]
    hx = hx_ref[...]
    dt_x = _dot(dt.astype(BF16), hx)
    dtd2e_x = _dot((dt * d2e).astype(BF16), hx)
    ecum_x = _dot(e_cum.astype(BF16), hx)
    lane2 = lax.broadcasted_iota(I32, (lc, SSM_GROUP_W), 1)
    for g in range(SSM_GROUPS):
        xg = xbc_ref[:, g * SSM_GROUP_W:(g + 1) * SSM_GROUP_W].astype(F32)
        b0 = SSM_INNER + g * SSM_STATE
        c0 = SSM_INNER + SSM_BC + g * SSM_STATE
        bg = xbc_ref[:, b0:b0 + SSM_STATE]
        cg = xbc_ref[:, c0:c0 + SSM_STATE]
        gcols = slice(g * SSM_GROUP_W, (g + 1) * SSM_GROUP_W)
        xdt_b = (xg * dt_x[:, gcols]).astype(BF16)
        cb = _dot_nt(cg, bg)
        y = jnp.zeros((lc, SSM_GROUP_W), F32)
        for j in range(SSM_HPG):
            hd = g * SSM_HPG + j
            seg = a_cum[:, hd:hd + 1] - a_cumT[hd:hd + 1, :]
            decay = jnp.exp(jnp.where(causal, seg, -jnp.inf))
            m = (cb * decay).astype(BF16)
            in_head = (lane2 >= j * SSM_HEAD_DIM) & (lane2 < (j + 1) * SSM_HEAD_DIM)
            y = y + _dot(m, jnp.where(in_head, xdt_b, jnp.zeros_like(xdt_b)))
        st = state[g]
        y = y + _dot(cg, st.astype(BF16)) * ecum_x[:, gcols]
        xd2e = (xg * dtd2e_x[:, gcols]).astype(BF16)
        state[g] = st * _expand_heads(cdec, g, 1) + _dot_tn(bg, xd2e)
        y = y + _expand_heads(dskip, g, 1) * xg
        zg = z_ref[:, g * SSM_GROUP_W:(g + 1) * SSM_GROUP_W].astype(F32)
        yz = y * (zg * _sigmoid(zg))
        yn = _rms(yz) * gn_ref[:, g * SSM_GROUP_W:(g + 1) * SSM_GROUP_W]
        yn_ref[:, g * SSM_GROUP_W:(g + 1) * SSM_GROUP_W] = yn.astype(BF16)


def _ssd_scan(z, xbc, dt, adt, adtT, d_skip, g_norm, bsz, seqlen, lc):
    n = z.shape[0]
    nc = seqlen // lc
    pad = jnp.zeros((LANES - SSM_HEADS,), F32)
    d_row = jnp.concatenate([d_skip, pad]).reshape(1, LANES)
    rows = lambda b, c: (b * nc + c, 0)
    vec = lambda b, c: (0, 0)
    head_of_col = jnp.arange(SSM_INNER, dtype=I32) // SSM_HEAD_DIM
    head_expand = (jnp.arange(LANES, dtype=I32)[:, None] == head_of_col[None, :]).astype(BF16)
    return pl.pallas_call(
        functools.partial(_ssd_scan_kernel, lc=lc),
        out_shape=jax.ShapeDtypeStruct((n, SSM_INNER), BF16),
        grid=(bsz, nc),
        in_specs=[
            pl.BlockSpec((lc, SSM_CONV_DIM), rows),
            pl.BlockSpec((lc, SSM_INNER), rows),
            pl.BlockSpec((lc, LANES), rows),
            pl.BlockSpec((lc, LANES), rows),
            pl.BlockSpec((None, SSM_HEADS, lc), lambda b, c: (b, 0, c)),
            pl.BlockSpec((1, LANES), vec),
            pl.BlockSpec((1, SSM_INNER), vec),
            pl.BlockSpec((LANES, SSM_INNER), vec),
        ],
        out_specs=pl.BlockSpec((lc, SSM_INNER), rows),
        scratch_shapes=[pltpu.VMEM((SSM_GROUPS, SSM_STATE, SSM_GROUP_W), F32)],
        compiler_params=_cparams("arbitrary", "arbitrary"),
        name="ssd_scan",
    )(xbc, z, dt, adt, adtT, d_row, g_norm.reshape(1, -1), head_expand)


def kernel(x, c, positions, w_mod, b_mod, g_mix_norm, g_ffn_norm, mla_w_in, mla_g_q, mla_g_kv, mla_w_q_up, mla_w_kv_up, mla_w_out, ssm_w_in, ssm_conv_w, ssm_conv_b, ssm_dt_bias, ssm_a_log, ssm_d, ssm_g_norm, ssm_w_out, moe_w_router, moe_b_router, moe_w_gate_up, moe_b_gate_up, moe_w_down, moe_b_down, g_final):
    bsz, seqlen, d = x.shape
    depth = w_mod.shape[0]
    n = bsz * seqlen
    tm = min(512, seqlen)
    mod = _modulation(c, w_mod, b_mod)
    mod = mod.reshape(depth, 6, bsz, 1, d)
    pos_f = positions.astype(F32).reshape(n, 1)
    xc = x.reshape(n, d)
    pending = None
    for i in range(depth):
        sh1, sc1, gt1, sh2, sc2, gt2 = [mod[i, j] for j in range(6)]
        j = i // 2
        if i % 2 == 0:
            q, k, v = _mla_pre(xc, pos_f, sh1, sc1, g_mix_norm[i], mla_w_in[j], mla_g_q[j], mla_g_kv[j],
                               mla_w_q_up[j], mla_w_kv_up[j], bsz, seqlen, tm)
            o = _attention(q, k, v, tm).reshape(n, MLA_HEADS * V_HEAD)
            w_out = mla_w_out[j]
        else:
            xc, z, xbc, dt, adt, adtT = _ssd_pre(pending, sh1, sc1, g_mix_norm[i], ssm_w_in[j], ssm_conv_w[j],
                                                 ssm_conv_b[j], ssm_dt_bias[j], ssm_a_log[j], bsz, seqlen,
                                                 SSD_TILE)
            o = _ssd_scan(z, xbc, dt, adt, adtT, ssm_d[j], ssm_g_norm[j], bsz, seqlen, min(256, seqlen))
            w_out = ssm_w_out[j]
        x1, h2, meta, gates, counts = _post_mixer(xc, o, w_out, gt1, sh2, sc2, g_ffn_norm[i],
                                                  moe_w_router[i], moe_b_router[i], seqlen, tm)
        defer = i + 1 < depth and (i + 1) % 2 == 1
        out = _moe(x1, h2, meta, gates, counts, gt2, g_final, moe_w_gate_up, moe_b_gate_up,
                   moe_w_down, moe_b_down, i, seqlen, final=(i == depth - 1), defer_combine=defer)
        if defer:
            pending = out
        else:
            xc = out
    return xc.reshape(bsz, seqlen, d)
```

```python
import functools
import math

import jax
import jax.numpy as jnp
from jax import lax
from jax.experimental import pallas as pl
from jax.experimental.pallas import tpu as pltpu

F32 = jnp.float32
BF16 = jnp.bfloat16
I32 = jnp.int32
U32 = jnp.uint32

NORM_EPS = 1e-6
MLA_HEADS = 8
Q_LORA = 256
KV_LORA = 256
QK_NOPE = 128
QK_ROPE = 64
V_HEAD = 128
QK_HEAD = QK_NOPE + QK_ROPE
ROPE_THETA = 10000.0
MLA_SCALE = QK_HEAD ** -0.5
SSM_HEAD_DIM = 64
SSM_GROUPS = 8
SSM_HPG = 4
SSM_HEADS = SSM_GROUPS * SSM_HPG
SSM_STATE = 128
SSM_CONV = 4
SSM_GROUP_W = SSM_HPG * SSM_HEAD_DIM
SSM_INNER = SSM_GROUPS * SSM_GROUP_W
SSM_BC = SSM_GROUPS * SSM_STATE
SSM_CONV_DIM = SSM_INNER + 2 * SSM_BC
N_EXPERTS = 32
TOP_K = 4
SWIGLU_LIMIT = 7.0
SWIGLU_ALPHA = 1.702
MOE_BLOCK = 512

LANES = 128
VMEM_LIMIT = 56 * 1024 * 1024


def _cparams(*sem):
    return pltpu.CompilerParams(dimension_semantics=tuple(sem), vmem_limit_bytes=VMEM_LIMIT)


def _dot(a, b):
    return jnp.dot(a, b, preferred_element_type=F32)


def _dot_nt(a, b):
    return lax.dot_general(a, b, (((1,), (1,)), ((), ())), preferred_element_type=F32)


def _dot_tn(a, b):
    return lax.dot_general(a, b, (((0,), (0,)), ((), ())), preferred_element_type=F32)


def _split3(a):
    hi = a.astype(BF16)
    r1 = a - hi.astype(F32)
    mid = r1.astype(BF16)
    lo = (r1 - mid.astype(F32)).astype(BF16)
    return hi, mid, lo


def _dot_f32ish(a, b):
    ah, am, _ = _split3(a)
    bh, bm, _ = _split3(b)
    return _dot(ah, bh) + (_dot(ah, bm) + _dot(am, bh))


def _pack_bf16_pair(lo, hi):
    lo_b = lax.bitcast_convert_type(lo.astype(BF16).astype(F32), U32)
    hi_b = lax.bitcast_convert_type(hi.astype(BF16).astype(F32), U32)
    return hi_b | lax.shift_right_logical(lo_b, jnp.uint32(16))


def _unpack_bf16_pair(w):
    lo = lax.bitcast_convert_type(lax.shift_left(w, jnp.uint32(16)), F32)
    hi = lax.bitcast_convert_type(w & jnp.uint32(0xFFFF0000), F32)
    return lo, hi


ROW_CHUNKS = 4


def _chunk_copies(buf2d, rows_hbm, r0, nrows, sem, *, to_hbm):
    cps = []
    for c in range(ROW_CHUNKS):
        v = buf2d.at[:, pl.ds(c * LANES, LANES)]
        h = rows_hbm.at[pl.ds(r0, nrows), c]
        cps.append(pltpu.make_async_copy(v, h, sem) if to_hbm else pltpu.make_async_copy(h, v, sem))
    return cps


def _sigmoid(x):
    return 1.0 / (1.0 + jnp.exp(-x))


def _rms(x):
    return x * lax.rsqrt(jnp.mean(x * x, axis=-1, keepdims=True) + NORM_EPS)


def _mod_kernel(c_ref, w_ref, b_ref, o_ref):
    c = c_ref[...]
    cond = c * _sigmoid(c)
    o_ref[...] = _dot_f32ish(cond, w_ref[...]) + b_ref[...]


def _modulation(c, w_mod, b_mod):
    depth, d, d6 = w_mod.shape
    bsz = c.shape[0]
    nj = d6 // d
    return pl.pallas_call(
        _mod_kernel,
        out_shape=jax.ShapeDtypeStruct((depth, nj, bsz, d), F32),
        grid=(depth, nj),
        in_specs=[
            pl.BlockSpec((bsz, d), lambda l, j: (0, 0)),
            pl.BlockSpec((None, d, d), lambda l, j: (l, 0, j)),
            pl.BlockSpec((None, 1, d), lambda l, j: (l, 0, j)),
        ],
        out_specs=pl.BlockSpec((None, None, bsz, d), lambda l, j: (l, j, 0, 0)),
        compiler_params=_cparams("arbitrary", "arbitrary"),
        name="mod",
    )(c, w_mod, b_mod.reshape(depth, 1, d6))


def _mla_pre_kernel(x_ref, pos_ref, sh_ref, sc_ref, g_ref, win_ref, gq_ref, gkv_ref, wq_ref, wkv_ref,
                    invf_ref, q_ref, k_ref, v_ref):
    x = x_ref[...]
    h = _rms(x) * g_ref[...] * (1.0 + sc_ref[...]) + sh_ref[...]
    lat = _dot(h.astype(BF16), win_ref[...])
    q_lat = _rms(lat[:, :Q_LORA]) * gq_ref[...]
    kv_lat = _rms(lat[:, Q_LORA:Q_LORA + KV_LORA]) * gkv_ref[...]
    ang = pos_ref[...] * invf_ref[...]
    cs = jnp.cos(ang)
    sn = jnp.sin(ang)
    o = Q_LORA + KV_LORA
    k_rope = lat[:, o:o + LANES] * cs + lat[:, o + LANES:o + 2 * LANES] * sn
    qq = _dot(q_lat.astype(BF16), wq_ref[...])
    kv = _dot(kv_lat.astype(BF16), wkv_ref[...])
    rot0 = MLA_HEADS * 2 * LANES
    k_rope_b = k_rope[:, :QK_ROPE].astype(BF16)
    for hd in range(MLA_HEADS):
        c0 = hd * 2 * LANES
        q_nope = qq[:, c0:c0 + LANES] * MLA_SCALE
        q_rope = (qq[:, c0 + LANES:c0 + 2 * LANES] * cs
                  + qq[:, rot0 + hd * LANES:rot0 + (hd + 1) * LANES] * sn) * MLA_SCALE
        q_ref[hd, :, 0:QK_NOPE] = q_nope.astype(BF16)
        q_ref[hd, :, QK_NOPE:QK_HEAD] = q_rope[:, :QK_ROPE].astype(BF16)
        k_ref[hd, :, 0:QK_NOPE] = kv[:, c0:c0 + LANES].astype(BF16)
        k_ref[hd, :, QK_NOPE:QK_HEAD] = k_rope_b
        v_ref[hd] = kv[:, c0 + LANES:c0 + 2 * LANES].astype(BF16)


def _rot_half_cols(w):
    half = QK_ROPE // 2
    return jnp.concatenate([-w[..., half:], w[..., :half]], axis=-1)


def _mla_pre(x2d, pos_f, sh, sc, g, w_in, g_q, g_kv, w_q_up, w_kv_up, bsz, seqlen, tm):
    n, d = x2d.shape
    hh = MLA_HEADS
    o = Q_LORA + KV_LORA
    wr = w_in[:, o:o + QK_ROPE]
    zpad = jnp.zeros((d, LANES - QK_ROPE), F32)
    w_in_ext = jnp.concatenate([w_in[:, :o], wr, zpad, _rot_half_cols(wr), zpad], axis=1).astype(BF16)
    wq = w_q_up.reshape(Q_LORA, hh, QK_HEAD)
    zq = jnp.zeros((Q_LORA, hh, LANES - QK_ROPE), F32)
    wq_main = jnp.concatenate([wq, zq], axis=-1).reshape(Q_LORA, hh * 2 * LANES)
    wq_rot = jnp.concatenate([_rot_half_cols(wq[..., QK_NOPE:]), zq], axis=-1).reshape(Q_LORA, hh * LANES)
    wq_ext = jnp.concatenate([wq_main, wq_rot], axis=1).astype(BF16)
    inv_freq = 1.0 / (ROPE_THETA ** (jnp.arange(0, QK_ROPE, 2, dtype=F32) / QK_ROPE))
    invf = jnp.concatenate([inv_freq, inv_freq, jnp.zeros((LANES - QK_ROPE,), F32)]).reshape(1, LANES)
    tpb = seqlen // tm
    vec = lambda i: (0, 0)
    outs = pl.pallas_call(
        _mla_pre_kernel,
        out_shape=(
            jax.ShapeDtypeStruct((bsz, hh, seqlen, QK_HEAD), BF16),
            jax.ShapeDtypeStruct((bsz, hh, seqlen, QK_HEAD), BF16),
            jax.ShapeDtypeStruct((bsz, hh, seqlen, V_HEAD), BF16),
        ),
        grid=(n // tm,),
        in_specs=[
            pl.BlockSpec((tm, d), lambda i: (i, 0)),
            pl.BlockSpec((tm, 1), lambda i: (i, 0)),
            pl.BlockSpec((None, 1, d), lambda i: (i // tpb, 0, 0)),
            pl.BlockSpec((None, 1, d), lambda i: (i // tpb, 0, 0)),
            pl.BlockSpec((1, d), vec),
            pl.BlockSpec(w_in_ext.shape, vec),
            pl.BlockSpec((1, Q_LORA), vec),
            pl.BlockSpec((1, KV_LORA), vec),
            pl.BlockSpec(wq_ext.shape, vec),
            pl.BlockSpec((KV_LORA, hh * 2 * LANES), vec),
            pl.BlockSpec((1, LANES), vec),
        ],
        out_specs=(
            pl.BlockSpec((None, hh, tm, QK_HEAD), lambda i: (i // tpb, 0, i % tpb, 0)),
            pl.BlockSpec((None, hh, tm, QK_HEAD), lambda i: (i // tpb, 0, i % tpb, 0)),
            pl.BlockSpec((None, hh, tm, V_HEAD), lambda i: (i // tpb, 0, i % tpb, 0)),
        ),
        compiler_params=_cparams("arbitrary"),
        name="mla_pre",
    )(x2d, pos_f, sh, sc, g.reshape(1, d), w_in_ext, g_q.reshape(1, -1), g_kv.reshape(1, -1), wq_ext,
      w_kv_up.astype(BF16), invf)
    return outs


ATTN_HEADS_PER_STEP = 4


def _attn_kernel(q_ref, k_ref, v_ref, o_ref, m_scr, acc_scr, *, tq, hp):
    qi = pl.program_id(2)
    m_scr[...] = jnp.full(m_scr.shape, -jnp.inf, F32)
    acc_scr[...] = jnp.zeros(acc_scr.shape, F32)

    def block(hd, r0, tk, masked):
        k = k_ref[hd, pl.ds(r0, tk), :]
        v_ext = jnp.concatenate([v_ref[hd, pl.ds(r0, tk), :], jnp.ones((tk, V_HEAD), BF16)], axis=1)
        s = _dot_nt(q_ref[hd], k)
        if masked:
            row = lax.broadcasted_iota(I32, (tq, tk), 0)
            col = lax.broadcasted_iota(I32, (tq, tk), 1)
            s = jnp.where(row + (tk - tq) >= col, s, -jnp.inf)
        m_prev = m_scr[hd]
        m_new = jnp.maximum(m_prev, jnp.max(s, axis=-1, keepdims=True))
        alpha = jnp.exp(m_prev - m_new)
        p = jnp.exp(s - m_new).astype(BF16)
        acc_scr[hd] = alpha * acc_scr[hd] + _dot(p, v_ext)
        m_scr[hd] = m_new

    def body(j, carry):
        for hd in range(hp):
            block(hd, pl.multiple_of(j * (2 * tq), 2 * tq), 2 * tq, False)
        return carry

    lax.fori_loop(0, qi // 2, body, 0)

    @pl.when(qi % 2 == 1)
    def _():
        for hd in range(hp):
            block(hd, pl.multiple_of((qi - 1) * tq, tq), 2 * tq, True)

    @pl.when(qi % 2 == 0)
    def _():
        for hd in range(hp):
            block(hd, pl.multiple_of(qi * tq, tq), tq, True)

    for hd in range(hp):
        acc = acc_scr[hd]
        o_ref[:, hd * V_HEAD:(hd + 1) * V_HEAD] = (acc[:, :V_HEAD] / acc[:, V_HEAD:]).astype(o_ref.dtype)


def _attention(q, k, v, tq):
    bsz, hh, seqlen, _ = q.shape
    hp = ATTN_HEADS_PER_STEP
    return pl.pallas_call(
        functools.partial(_attn_kernel, tq=tq, hp=hp),
        out_shape=jax.ShapeDtypeStruct((bsz, seqlen, hh * V_HEAD), BF16),
        grid=(bsz, hh // hp, seqlen // tq),
        in_specs=[
            pl.BlockSpec((None, hp, tq, QK_HEAD), lambda b, h, i: (b, h, i, 0)),
            pl.BlockSpec((None, hp, seqlen, QK_HEAD), lambda b, h, i: (b, h, 0, 0)),
            pl.BlockSpec((None, hp, seqlen, V_HEAD), lambda b, h, i: (b, h, 0, 0)),
        ],
        out_specs=pl.BlockSpec((None, tq, hp * V_HEAD), lambda b, h, i: (b, i, h)),
        scratch_shapes=[
            pltpu.VMEM((hp, tq, 1), F32),
            pltpu.VMEM((hp, tq, 2 * V_HEAD), F32),
        ],
        compiler_params=_cparams("arbitrary", "arbitrary", "arbitrary"),
        name="attn",
    )(q, k, v)


def _post_mixer_kernel(x_ref, o_ref, wout_ref, gt1_ref, sh_ref, sc_ref, g_ref, wr_ref, br_ref,
                       x1_ref, h2_hbm, meta_ref, gate_ref, cnt_ref, carry_scr, pbuf, psem, *, tm):
    i = pl.program_id(0)

    @pl.when(i == 0)
    def _():
        carry_scr[...] = jnp.zeros(carry_scr.shape, F32)

    y = _dot(o_ref[...], wout_ref[...])
    x1 = x_ref[...] + gt1_ref[...] * y
    x1_ref[...] = x1
    h2 = _rms(x1) * g_ref[...] * (1.0 + sc_ref[...]) + sh_ref[...]
    half = h2.shape[1] // 2

    @pl.when(i > 0)
    def _():
        for cp in _chunk_copies(pbuf, h2_hbm, (i - 1) * tm, tm, psem, to_hbm=True):
            cp.wait()

    pbuf[...] = _pack_bf16_pair(h2[:, :half], h2[:, half:])
    for cp in _chunk_copies(pbuf, h2_hbm, i * tm, tm, psem, to_hbm=True):
        cp.start()

    logits = _dot_f32ish(h2, wr_ref[...]) + br_ref[...]
    lane = lax.broadcasted_iota(I32, (tm, LANES), 1).astype(F32)
    work = logits
    idxs, vals = [], []
    for _ in range(TOP_K):
        mx = jnp.max(work, axis=-1, keepdims=True)
        idx = jnp.min(jnp.where(work == mx, lane, float(LANES)), axis=-1, keepdims=True)
        idxs.append(idx)
        vals.append(mx)
        work = jnp.where(lane == idx, -jnp.inf, work)
    exps = [jnp.exp(vk - vals[0]) for vk in vals]
    denom = exps[0] + exps[1] + exps[2] + exps[3]
    onehot = jnp.zeros((tm, LANES), F32)
    for idx in idxs:
        onehot = onehot + jnp.where(lane == idx, 1.0, 0.0)
    row = lax.broadcasted_iota(I32, (tm, tm), 0)
    col = lax.broadcasted_iota(I32, (tm, tm), 1)
    ltri = jnp.where(row > col, 1.0, 0.0).astype(BF16)
    cum = _dot(ltri, onehot.astype(BF16)) + carry_scr[...]
    meta = jnp.zeros((tm, LANES), F32)
    gates = jnp.zeros((tm, LANES), F32)
    for kk in range(TOP_K):
        rank = jnp.sum(jnp.where(lane == idxs[kk], cum, 0.0), axis=-1, keepdims=True)
        meta = jnp.where(lane == float(kk), idxs[kk], meta)
        meta = jnp.where(lane == float(TOP_K + kk), rank, meta)
        gates = jnp.where(lane == float(kk), exps[kk] / denom, gates)
    meta_ref[...] = jnp.transpose(meta)[:2 * TOP_K, :].astype(I32)
    gate_ref[...] = gates
    carry = carry_scr[...] + jnp.sum(onehot, axis=0, keepdims=True)
    carry_scr[...] = carry
    cnt_ref[...] = carry.astype(I32)

    @pl.when(i == pl.num_programs(0) - 1)
    def _():
        for cp in _chunk_copies(pbuf, h2_hbm, i * tm, tm, psem, to_hbm=True):
            cp.wait()


def _post_mixer(x2d, o2d, w_out, gt1, sh2, sc2, g_ffn, w_router, b_router, seqlen, tm):
    n, d = x2d.shape
    kdim = o2d.shape[1]
    tpb = seqlen // tm
    wr = jnp.concatenate([w_router, jnp.zeros((d, LANES - N_EXPERTS), F32)], axis=1)
    br = jnp.concatenate([b_router, jnp.full((LANES - N_EXPERTS,), -1e30, F32)]).reshape(1, LANES)
    vec = lambda i: (0, 0)
    bvec = lambda i: (i // tpb, 0, 0)
    return pl.pallas_call(
        functools.partial(_post_mixer_kernel, tm=tm),
        out_shape=(
            jax.ShapeDtypeStruct((n, d), F32),
            jax.ShapeDtypeStruct((n, ROW_CHUNKS, LANES), U32),
            jax.ShapeDtypeStruct((2 * TOP_K, n), I32),
            jax.ShapeDtypeStruct((n, LANES), F32),
            jax.ShapeDtypeStruct((1, LANES), I32),
        ),
        grid=(n // tm,),
        in_specs=[
            pl.BlockSpec((tm, d), lambda i: (i, 0)),
            pl.BlockSpec((tm, kdim), lambda i: (i, 0)),
            pl.BlockSpec((kdim, d), vec),
            pl.BlockSpec((None, 1, d), bvec),
            pl.BlockSpec((None, 1, d), bvec),
            pl.BlockSpec((None, 1, d), bvec),
            pl.BlockSpec((1, d), vec),
            pl.BlockSpec((d, LANES), vec),
            pl.BlockSpec((1, LANES), vec),
        ],
        out_specs=(
            pl.BlockSpec((tm, d), lambda i: (i, 0)),
            pl.BlockSpec(memory_space=pl.ANY),
            pl.BlockSpec((2 * TOP_K, tm), lambda i: (0, i)),
            pl.BlockSpec((tm, LANES), lambda i: (i, 0)),
            pl.BlockSpec((1, LANES), vec),
        ),
        scratch_shapes=[pltpu.VMEM((1, LANES), F32), pltpu.VMEM((tm, d // 2), U32), pltpu.SemaphoreType.DMA],
        compiler_params=_cparams("arbitrary"),
        name="post_mixer",
    )(x2d, o2d, w_out.astype(BF16), gt1, sh2, sc2, g_ffn.reshape(1, d), wr, br)


ROW_UNROLL = 8


def _row_copy(src, s, dst, t, sem):
    return pltpu.make_async_copy(src.at[s], dst.at[t], sem)


def _dest_kernel(pstart_ref, meta_ref, dest_ref):
    e = meta_ref[0:TOP_K, :]
    dest = meta_ref[TOP_K:2 * TOP_K, :]
    for j in range(N_EXPERTS):
        dest = dest + jnp.where(e == j, pstart_ref[j], 0)
    dest_ref[...] = dest


def _dest_rows(meta_t, pstart):
    n = meta_t.shape[1]
    tn = min(4096, n)
    grid_spec = pltpu.PrefetchScalarGridSpec(
        num_scalar_prefetch=1,
        grid=(n // tn,),
        in_specs=[pl.BlockSpec((2 * TOP_K, tn), lambda i, ps: (0, i))],
        out_specs=pl.BlockSpec((TOP_K, tn), lambda i, ps: (0, i)),
    )
    return pl.pallas_call(
        _dest_kernel,
        out_shape=jax.ShapeDtypeStruct((TOP_K, n), I32),
        grid_spec=grid_spec,
        compiler_params=_cparams("arbitrary"),
        name="dest_rows",
    )(pstart, meta_t)


def _dispatch_kernel(pstart_ref, cnt_ref, dest_ref, h_ref, xs_hbm, zbuf, sem, zsem, *, tm, nsteps):
    i = pl.program_id(0)

    @pl.when(i == 0)
    def _():
        zbuf[...] = jnp.zeros(zbuf.shape, U32)

    experts_per_step = -(-N_EXPERTS // nsteps)

    def zero_fill(wait):
        for kx in range(experts_per_step):
            e = i * experts_per_step + kx

            @pl.when(e < N_EXPERTS)
            def _():
                def body(j, c):
                    cp = _row_copy(zbuf, 0, xs_hbm, j, zsem)
                    cp.wait() if wait else cp.start()
                    return c

                lax.fori_loop(pstart_ref[e] + cnt_ref[e], pstart_ref[e + 1], body, 0)

    zero_fill(wait=False)

    def start(tt, c):
        for u in range(ROW_UNROLL):
            t = tt * ROW_UNROLL + u
            for kk in range(TOP_K):
                _row_copy(h_ref, t, xs_hbm, dest_ref[kk, t], sem).start(priority=kk % 2)
        return c

    def wait(tt, c):
        for u in range(ROW_UNROLL * TOP_K):
            _row_copy(h_ref, 0, xs_hbm, 0, sem).wait()
        return c

    lax.fori_loop(0, tm // ROW_UNROLL, start, 0)
    lax.fori_loop(0, tm // ROW_UNROLL, wait, 0)
    zero_fill(wait=True)


def _dispatch(h2p, dest_t, pstart, counts, n_rows, tm):
    n = h2p.shape[0]
    grid_spec = pltpu.PrefetchScalarGridSpec(
        num_scalar_prefetch=2,
        grid=(n // tm,),
        in_specs=[
            pl.BlockSpec((TOP_K, tm), lambda i, ps, cn: (0, i), memory_space=pltpu.SMEM),
            pl.BlockSpec((tm, ROW_CHUNKS, LANES), lambda i, ps, cn: (i, 0, 0)),
        ],
        out_specs=pl.BlockSpec(memory_space=pl.ANY),
        scratch_shapes=[
            pltpu.VMEM((1, ROW_CHUNKS, LANES), U32),
            pltpu.SemaphoreType.DMA,
            pltpu.SemaphoreType.DMA,
        ],
    )
    return pl.pallas_call(
        functools.partial(_dispatch_kernel, tm=tm, nsteps=n // tm),
        out_shape=jax.ShapeDtypeStruct((n_rows, ROW_CHUNKS, LANES), U32),
        grid_spec=grid_spec,
        compiler_params=pltpu.CompilerParams(dimension_semantics=("arbitrary",), vmem_limit_bytes=VMEM_LIMIT,
                                             has_side_effects=True),
        name="dispatch",
    )(pstart, counts, dest_t, h2p)


def _ffn_kernel(be_ref, nu_ref, first_ref, wslot_ref, nxt_ref, xs_hbm, wgu_hbm, bgu_ref, wd_hbm, bd_ref, ys_hbm,
                wgu_b, wd_b, wgu_f, wd_f, xbuf, ybuf, xsem, ysem, wsem, *, ff, layer):
    i = pl.program_id(0)
    nu = nu_ref[0]
    slot = i % 2

    def w_copies(e, s):
        return [pltpu.make_async_copy(wgu_hbm.at[layer, e], wgu_f.at[s], wsem.at[s]),
                pltpu.make_async_copy(wd_hbm.at[layer, e], wd_f.at[s], wsem.at[s])]

    @pl.when(i == 0)
    def _():
        for cp in w_copies(be_ref[0], 0):
            cp.start()

    def x_copies(step, s):
        return _chunk_copies(xbuf.at[s], xs_hbm, step * MOE_BLOCK, MOE_BLOCK, xsem.at[s], to_hbm=False)

    def y_copies(step, s):
        return _chunk_copies(ybuf.at[s], ys_hbm, step * MOE_BLOCK, MOE_BLOCK, ysem.at[s], to_hbm=True)

    @pl.when(i == 0)
    def _():
        for cp in x_copies(0, 0):
            cp.start()

    @pl.when(i + 1 < nu)
    def _():
        for cp in x_copies(i + 1, 1 - slot):
            cp.start()

    @pl.when(first_ref[i] == 1)
    def _():
        ws = wslot_ref[i]
        for cp in w_copies(be_ref[i], ws):
            cp.wait()

        @pl.when(nxt_ref[i] >= 0)
        def _():
            for cp in w_copies(nxt_ref[i], 1 - ws):
                cp.start(priority=1)

        wgu_b[...] = wgu_f[ws].astype(BF16)
        wd_b[...] = wd_f[ws].astype(BF16)

    @pl.when(i < nu)
    def _():
        for cp in x_copies(i, slot):
            cp.wait()
        x_lo, x_hi = _unpack_bf16_pair(xbuf[slot])
        dh = x_lo.shape[1]
        gu = (_dot(x_lo.astype(BF16), wgu_b[0:dh, :]) + _dot(x_hi.astype(BF16), wgu_b[dh:2 * dh, :])
              + bgu_ref[...])
        g = jnp.minimum(gu[:, :ff], SWIGLU_LIMIT)
        lin = jnp.clip(gu[:, ff:], -SWIGLU_LIMIT, SWIGLU_LIMIT)
        act = g * _sigmoid(SWIGLU_ALPHA * g) * (lin + 1.0)
        y = _dot(act.astype(BF16), wd_b[...]) + bd_ref[...]

        @pl.when(i >= 2)
        def _():
            for cp in y_copies(i - 2, slot):
                cp.wait()

        ybuf[slot] = _pack_bf16_pair(y[:, :dh], y[:, dh:])
        for cp in y_copies(i, slot):
            cp.start()

    @pl.when(i == nu - 1)
    def _():
        for cp in y_copies(i, slot):
            cp.wait()

        @pl.when(i >= 1)
        def _():
            for cp in y_copies(i - 1, 1 - slot):
                cp.wait()


def _ffn(xs, blk_e, n_used, pad_end, w_gate_up, b_gate_up, w_down, b_down, layer):
    n_rows = xs.shape[0]
    depth, ne, d, ff2 = w_gate_up.shape
    ff = ff2 // 2
    n_blk = n_rows // MOE_BLOCK
    blk = jnp.arange(n_blk, dtype=I32)
    active = blk < n_used[0]
    first = (active & ((blk == 0) | (blk_e != jnp.roll(blk_e, 1)))).astype(I32)
    wslot = ((jnp.cumsum(first) - 1) % 2).astype(I32)
    nxt_blk = pad_end[blk_e] // MOE_BLOCK
    nxt = jnp.where(nxt_blk < n_used[0], blk_e[jnp.minimum(nxt_blk, n_blk - 1)], -1).astype(I32)
    row_buf = pltpu.VMEM((2, MOE_BLOCK, ROW_CHUNKS * LANES), U32)
    sp = lambda i, be, nu, fi, ws, nx: (layer, be[i], 0, 0)
    grid_spec = pltpu.PrefetchScalarGridSpec(
        num_scalar_prefetch=5,
        grid=(n_blk,),
        in_specs=[
            pl.BlockSpec(memory_space=pl.ANY),
            pl.BlockSpec(memory_space=pl.ANY),
            pl.BlockSpec((None, None, 1, ff2), sp),
            pl.BlockSpec(memory_space=pl.ANY),
            pl.BlockSpec((None, None, 1, d), sp),
        ],
        out_specs=pl.BlockSpec(memory_space=pl.ANY),
        scratch_shapes=[pltpu.VMEM((d, ff2), BF16), pltpu.VMEM((ff, d), BF16),
                        pltpu.VMEM((2, d, ff2), F32), pltpu.VMEM((2, ff, d), F32), row_buf, row_buf,
                        pltpu.SemaphoreType.DMA((2,)), pltpu.SemaphoreType.DMA((2,)),
                        pltpu.SemaphoreType.DMA((2,))],
    )
    return pl.pallas_call(
        functools.partial(_ffn_kernel, ff=ff, layer=layer),
        out_shape=jax.ShapeDtypeStruct((n_rows, ROW_CHUNKS, LANES), U32),
        grid_spec=grid_spec,
        compiler_params=_cparams("arbitrary"),
        name="ffn",
    )(blk_e, n_used, first, wslot, nxt, xs, w_gate_up, b_gate_up.reshape(depth, ne, 1, ff2), w_down,
      b_down.reshape(depth, ne, 1, d))


def _gather_scratch(tm):
    return [pltpu.VMEM((2, TOP_K, tm // ROW_UNROLL, ROW_CHUNKS, ROW_UNROLL, LANES), U32),
            pltpu.SemaphoreType.DMA((2,))]


def _gather_start(dest_ref, ys_hbm, buf, sem, tm, both_queues):
    def body(tt, c):
        for u in range(ROW_UNROLL):
            for kk in range(TOP_K):
                src = ys_hbm.at[dest_ref[kk, tt * ROW_UNROLL + u]]
                pltpu.make_async_copy(src, buf.at[kk, tt, :, u], sem).start(priority=kk % 2 if both_queues else 1)
        return c

    lax.fori_loop(0, tm // ROW_UNROLL, body, 0)


def _gather_next_unrolled(dest_next_ref, ys_hbm, buf, sem, tm):
    nslot = 1 - pl.program_id(0) % 2
    for tt in range(tm // ROW_UNROLL):
        for u in range(ROW_UNROLL):
            for kk in range(TOP_K):
                src = ys_hbm.at[dest_next_ref[kk, tt * ROW_UNROLL + u]]
                pltpu.make_async_copy(src, buf.at[nslot, kk, tt, :, u], sem.at[nslot]).start(priority=1)


def _gather_drain_last(ys_hbm, buf, sem, tm):
    i = pl.program_id(0)

    @pl.when(i == pl.num_programs(0) - 1)
    def _():
        nslot = 1 - i % 2
        _gather_wait(ys_hbm, buf.at[nslot], sem.at[nslot], tm)


def _gather_wait(ys_hbm, buf, sem, tm):
    def body(tt, c):
        for u in range(ROW_UNROLL * TOP_K):
            pltpu.make_async_copy(ys_hbm.at[0], buf.at[0, 0, :, 0], sem).wait()
        return c

    lax.fori_loop(0, tm // ROW_UNROLL, body, 0)


def _combine_rows(dest0_ref, dest_next_ref, gate_ref, ys_hbm, buf, sem, tm, both_queues):
    i = pl.program_id(0)
    slot = i % 2

    @pl.when(i == 0)
    def _():
        _gather_start(dest0_ref, ys_hbm, buf.at[0], sem.at[0], tm, both_queues)

    if dest_next_ref is not None:
        @pl.when(i + 1 < pl.num_programs(0))
        def _():
            _gather_start(dest_next_ref, ys_hbm, buf.at[1 - slot], sem.at[1 - slot], tm, both_queues)

    _gather_wait(ys_hbm, buf.at[slot], sem.at[slot], tm)
    gates = gate_ref[...]
    acc_lo = acc_hi = None
    for kk in range(TOP_K):
        rows = jnp.concatenate([buf[slot, kk, :, c, :, :].reshape(tm, LANES) for c in range(ROW_CHUNKS)], axis=1)
        lo, hi = _unpack_bf16_pair(rows)
        gk = gates[:, kk:kk + 1]
        acc_lo = gk * lo if acc_lo is None else acc_lo + gk * lo
        acc_hi = gk * hi if acc_hi is None else acc_hi + gk * hi
    return jnp.concatenate([acc_lo, acc_hi], axis=1)


def _combine_specs(tm, nsteps, idx):
    return [pl.BlockSpec((TOP_K, tm), lambda *a: (0, 0), memory_space=pltpu.SMEM),
            pl.BlockSpec((TOP_K, tm), lambda *a: (0, jnp.minimum(idx(*a) + 1, nsteps - 1)), memory_space=pltpu.SMEM)]


def _combine_kernel(dest0_ref, dest_next_ref, x1_ref, gate_ref, gt2_ref, gfin_ref, ys_hbm, o_ref, buf, sem,
                    *, tm, final):
    y = _combine_rows(dest0_ref, dest_next_ref, gate_ref, ys_hbm, buf, sem, tm, both_queues=True)
    x2 = x1_ref[...] + gt2_ref[...] * y
    if final:
        x2 = _rms(x2) * gfin_ref[...]
    o_ref[...] = x2


def _combine(x1, gates, gt2, g_final, ys, dest_t, seqlen, tm, final):
    n, d = x1.shape
    tpb = seqlen // tm
    grid_spec = pltpu.PrefetchScalarGridSpec(
        num_scalar_prefetch=0,
        grid=(n // tm,),
        in_specs=_combine_specs(tm, n // tm, lambda i: i) + [
            pl.BlockSpec((tm, d), lambda i: (i, 0)),
            pl.BlockSpec((tm, LANES), lambda i: (i, 0)),
            pl.BlockSpec((None, 1, d), lambda i: (i // tpb, 0, 0)),
            pl.BlockSpec((1, d), lambda i: (0, 0)),
            pl.BlockSpec(memory_space=pl.ANY),
        ],
        out_specs=pl.BlockSpec((tm, d), lambda i: (i, 0)),
        scratch_shapes=_gather_scratch(tm),
    )
    return pl.pallas_call(
        functools.partial(_combine_kernel, tm=tm, final=final),
        out_shape=jax.ShapeDtypeStruct((n, d), F32),
        grid_spec=grid_spec,
        compiler_params=_cparams("arbitrary"),
        name="combine",
    )(dest_t, dest_t, x1, gates, gt2, g_final.reshape(1, d), ys)


def _moe(x1, h2p, meta_t, gates, counts, gt2, g_final, w_gate_up, b_gate_up, w_down, b_down, layer, seqlen,
         final, defer_combine):
    n, d = x1.shape
    n_pair = n * TOP_K
    n_rows = -(-n_pair // MOE_BLOCK) * MOE_BLOCK + N_EXPERTS * MOE_BLOCK
    n_blk = n_rows // MOE_BLOCK
    cnt = counts[0, :N_EXPERTS]
    padded = (cnt + MOE_BLOCK - 1) // MOE_BLOCK * MOE_BLOCK
    pad_end = jnp.cumsum(padded)
    pstart = jnp.concatenate([pad_end - padded, pad_end[-1:]]).astype(I32)
    n_used = (pad_end[-1:] // MOE_BLOCK).astype(I32)
    blk_start = jnp.arange(n_blk, dtype=I32) * MOE_BLOCK
    blk_e = jnp.minimum(jnp.sum(blk_start[:, None] >= pad_end[None, :], axis=1), N_EXPERTS - 1).astype(I32)
    tm = min(256, seqlen)
    dest_t = _dest_rows(meta_t, pstart)
    xs = _dispatch(h2p, dest_t, pstart, cnt, n_rows, tm)
    ys = _ffn(xs, blk_e, n_used, pad_end.astype(I32), w_gate_up, b_gate_up, w_down, b_down, layer)
    if defer_combine:
        return x1, gates, gt2, ys, dest_t
    return _combine(x1, gates, gt2, g_final, ys, dest_t, seqlen, tm, final)


def _softplus(x):
    return jnp.maximum(x, 0.0) + jnp.log(1.0 + jnp.exp(-jnp.abs(x)))


SSD_TILE = 256


def _ssd_pre_kernel(dest0_ref, dest_next_ref, x1_ref, gate_ref, gt2_ref, ys_hbm,
                    sh_ref, sc_ref, g_ref, wz_ref, wx_ref, wdt_ref, wdtT_ref, cw_ref, cb_ref,
                    dtb_ref, dtbT_ref, a_ref, aT_ref,
                    x_out_ref, z_ref, xbc_ref, dt_ref, adt_ref, adtT_ref, ubuf, gbuf, gsem, *, tm, tpb, cchunk):
    i = pl.program_id(0)
    x = x1_ref[...] + gt2_ref[...] * _combine_rows(dest0_ref, None, gate_ref, ys_hbm, gbuf, gsem, tm,
                                                     both_queues=False)
    x_out_ref[...] = x
    _gather_next_unrolled(dest_next_ref, ys_hbm, gbuf, gsem, tm)
    h = _rms(x) * g_ref[...] * (1.0 + sc_ref[...]) + sh_ref[...]
    hb = h.astype(BF16)
    z_ref[...] = _dot(hb, wz_ref[...]).astype(BF16)
    dt = _softplus(_dot(hb, wdt_ref[...]) + dtb_ref[...])
    dt_ref[...] = dt
    adt_ref[...] = dt * a_ref[...]
    dtT = _softplus(_dot_nt(wdtT_ref[...], hb) + dtbT_ref[...])
    adtT_ref[...] = dtT * aT_ref[...]

    @pl.when(i % tpb == 0)
    def _():
        ubuf[...] = jnp.zeros(ubuf.shape, F32)

    for c0 in range(0, SSM_CONV_DIM, cchunk):
        cols = slice(c0, c0 + cchunk)
        u = _dot(hb, wx_ref[:, cols])
        ext = jnp.concatenate([ubuf[:, cols], u], axis=0)
        acc = cb_ref[:, cols] + cw_ref[SSM_CONV - 1:SSM_CONV, cols] * u
        for j in range(1, SSM_CONV):
            shifted = pltpu.roll(ext, j, axis=0)[8:, :]
            acc = acc + cw_ref[SSM_CONV - 1 - j:SSM_CONV - j, cols] * shifted
        xbc_ref[:, cols] = (acc * _sigmoid(acc)).astype(BF16)
        ubuf[:, cols] = u[tm - 8:, :]
    _gather_drain_last(ys_hbm, gbuf, gsem, tm)


def _ssd_pre(moe_out, sh, sc, g, w_in, conv_w, conv_b, dt_bias, a_log, bsz, seqlen, tm):
    x1, gates, gt2, ys, dest_t = moe_out
    n, d = x1.shape
    tm = min(tm, seqlen)
    tpb = seqlen // tm
    nh = SSM_HEADS
    wz = w_in[:, :SSM_INNER].astype(BF16)
    wx = w_in[:, SSM_INNER:SSM_INNER + SSM_CONV_DIM].astype(BF16)
    wdt_raw = w_in[:, SSM_INNER + SSM_CONV_DIM:]
    wdt = jnp.concatenate([wdt_raw, jnp.zeros((d, LANES - nh), F32)], axis=1).astype(BF16)
    wdtT = wdt_raw.T.astype(BF16)
    pad = jnp.zeros((LANES - nh,), F32)
    dtb = jnp.concatenate([dt_bias, pad]).reshape(1, LANES)
    a_neg = -jnp.exp(a_log.astype(F32))
    a_row = jnp.concatenate([a_neg, pad]).reshape(1, LANES)
    vec = lambda i: (0, 0)
    bvec = lambda i: (i // tpb, 0, 0)
    return pl.pallas_call(
        functools.partial(_ssd_pre_kernel, tm=tm, tpb=tpb, cchunk=1024),
        out_shape=(
            jax.ShapeDtypeStruct((n, d), F32),
            jax.ShapeDtypeStruct((n, SSM_INNER), BF16),
            jax.ShapeDtypeStruct((n, SSM_CONV_DIM), BF16),
            jax.ShapeDtypeStruct((n, LANES), F32),
            jax.ShapeDtypeStruct((n, LANES), F32),
            jax.ShapeDtypeStruct((bsz, nh, seqlen), F32),
        ),
        grid=(n // tm,),
        in_specs=_combine_specs(tm, n // tm, lambda i: i) + [
            pl.BlockSpec((tm, d), lambda i: (i, 0)),
            pl.BlockSpec((tm, LANES), lambda i: (i, 0)),
            pl.BlockSpec((None, 1, d), bvec),
            pl.BlockSpec(memory_space=pl.ANY),
            pl.BlockSpec((None, 1, d), bvec),
            pl.BlockSpec((None, 1, d), bvec),
            pl.BlockSpec((1, d), vec),
            pl.BlockSpec((d, SSM_INNER), vec),
            pl.BlockSpec((d, SSM_CONV_DIM), vec),
            pl.BlockSpec((d, LANES), vec),
            pl.BlockSpec((nh, d), vec),
            pl.BlockSpec((SSM_CONV, SSM_CONV_DIM), vec),
            pl.BlockSpec((1, SSM_CONV_DIM), vec),
            pl.BlockSpec((1, LANES), vec),
            pl.BlockSpec((nh, 1), vec),
            pl.BlockSpec((1, LANES), vec),
            pl.BlockSpec((nh, 1), vec),
        ],
        out_specs=(
            pl.BlockSpec((tm, d), lambda i: (i, 0)),
            pl.BlockSpec((tm, SSM_INNER), lambda i: (i, 0)),
            pl.BlockSpec((tm, SSM_CONV_DIM), lambda i: (i, 0)),
            pl.BlockSpec((tm, LANES), lambda i: (i, 0)),
            pl.BlockSpec((tm, LANES), lambda i: (i, 0)),
            pl.BlockSpec((None, nh, tm), lambda i: (i // tpb, 0, i % tpb)),
        ),
        scratch_shapes=[pltpu.VMEM((8, SSM_CONV_DIM), F32)] + _gather_scratch(tm),
        compiler_params=_cparams("arbitrary"),
        name="ssd_pre",
    )(dest_t, dest_t, x1, gates, gt2, ys, sh, sc, g.reshape(1, d), wz, wx, wdt, wdtT, conv_w,
      conv_b.reshape(1, -1), dtb, dt_bias.reshape(nh, 1), a_row, a_neg.reshape(nh, 1))


def _expand_heads(v, g, rows):
    lane = lax.broadcasted_iota(I32, (rows, LANES), 1)
    lo = lane < SSM_HEAD_DIM
    parts = []
    for j in range(0, SSM_HPG, 2):
        h0 = g * SSM_HPG + j
        parts.append(jnp.where(lo, v[:, h0:h0 + 1], v[:, h0 + 1:h0 + 2]))
    return jnp.concatenate(parts, axis=1)


def _ssd_scan_kernel(xbc_ref, z_ref, dt_ref, adt_ref, adtT_ref, d_ref, gn_ref, hx_ref, yn_ref, state, *, lc):
    c = pl.program_id(1)

    @pl.when(c == 0)
    def _():
        state[...] = jnp.zeros(state.shape, F32)

    row = lax.broadcasted_iota(I32, (lc, lc), 0)
    col = lax.broadcasted_iota(I32, (lc, lc), 1)
    causal = row >= col
    tri = jnp.where(causal, 1.0, 0.0).astype(BF16)
    triT = jnp.where(row <= col, 1.0, 0.0).astype(BF16)
    ah, am, al = _split3(adt_ref[...])
    a_cum = _dot(tri, ah) + (_dot(tri, am) + _dot(tri, al))
    th, tmid, tl = _split3(adtT_ref[...])
    a_cumT = _dot(th, triT) + (_dot(tmid, triT) + _dot(tl, triT))
    dt = dt_ref[...]
    a_last = a_cum[lc - 1:lc, :]
    e_cum = jnp.exp(a_cum)
    d2e = jnp.exp(a_last - a_cum)
    cdec = jnp.exp(a_last)
    dskip = d_ref[...]
    hx = hx_ref[...]
    dt_x = _dot(dt.astype(BF16), hx)
    dtd2e_x = _dot((dt * d2e).astype(BF16), hx)
    ecum_x = _dot(e_cum.astype(BF16), hx)
    lane2 = lax.broadcasted_iota(I32, (lc, SSM_GROUP_W), 1)
    for g in range(SSM_GROUPS):
        xg = xbc_ref[:, g * SSM_GROUP_W:(g + 1) * SSM_GROUP_W].astype(F32)
        b0 = SSM_INNER + g * SSM_STATE
        c0 = SSM_INNER + SSM_BC + g * SSM_STATE
        bg = xbc_ref[:, b0:b0 + SSM_STATE]
        cg = xbc_ref[:, c0:c0 + SSM_STATE]
        gcols = slice(g * SSM_GROUP_W, (g + 1) * SSM_GROUP_W)
        xdt_b = (xg * dt_x[:, gcols]).astype(BF16)
        cb = _dot_nt(cg, bg)
        y = jnp.zeros((lc, SSM_GROUP_W), F32)
        for j in range(SSM_HPG):
            hd = g * SSM_HPG + j
            seg = a_cum[:, hd:hd + 1] - a_cumT[hd:hd + 1, :]
            decay = jnp.exp(jnp.where(causal, seg, -jnp.inf))
            m = (cb * decay).astype(BF16)
            in_head = (lane2 >= j * SSM_HEAD_DIM) & (lane2 < (j + 1) * SSM_HEAD_DIM)
            y = y + _dot(m, jnp.where(in_head, xdt_b, jnp.zeros_like(xdt_b)))
        st = state[g]
        y = y + _dot(cg, st.astype(BF16)) * ecum_x[:, gcols]
        xd2e = (xg * dtd2e_x[:, gcols]).astype(BF16)
        state[g] = st * _expand_heads(cdec, g, 1) + _dot_tn(bg, xd2e)
        y = y + _expand_heads(dskip, g, 1) * xg
        zg = z_ref[:, g * SSM_GROUP_W:(g + 1) * SSM_GROUP_W].astype(F32)
        yz = y * (zg * _sigmoid(zg))
        yn = _rms(yz) * gn_ref[:, g * SSM_GROUP_W:(g + 1) * SSM_GROUP_W]
        yn_ref[:, g * SSM_GROUP_W:(g + 1) * SSM_GROUP_W] = yn.astype(BF16)


def _ssd_scan(z, xbc, dt, adt, adtT, d_skip, g_norm, bsz, seqlen, lc):
    n = z.shape[0]
    nc = seqlen // lc
    pad = jnp.zeros((LANES - SSM_HEADS,), F32)
    d_row = jnp.concatenate([d_skip, pad]).reshape(1, LANES)
    rows = lambda b, c: (b * nc + c, 0)
    vec = lambda b, c: (0, 0)
    head_of_col = jnp.arange(SSM_INNER, dtype=I32) // SSM_HEAD_DIM
    head_expand = (jnp.arange(LANES, dtype=I32)[:, None] == head_of_col[None, :]).astype(BF16)
    return pl.pallas_call(
        functools.partial(_ssd_scan_kernel, lc=lc),
        out_shape=jax.ShapeDtypeStruct((n, SSM_INNER), BF16),
        grid=(bsz, nc),
        in_specs=[
            pl.BlockSpec((lc, SSM_CONV_DIM), rows),
            pl.BlockSpec((lc, SSM_INNER), rows),
            pl.BlockSpec((lc, LANES), rows),
            pl.BlockSpec((lc, LANES), rows),
            pl.BlockSpec((None, SSM_HEADS, lc), lambda b, c: (b, 0, c)),
            pl.BlockSpec((1, LANES), vec),
            pl.BlockSpec((1, SSM_INNER), vec),
            pl.BlockSpec((LANES, SSM_INNER), vec),
        ],
        out_specs=pl.BlockSpec((lc, SSM_INNER), rows),
        scratch_shapes=[pltpu.VMEM((SSM_GROUPS, SSM_STATE, SSM_GROUP_W), F32)],
        compiler_params=_cparams("arbitrary", "arbitrary"),
        name="ssd_scan",
    )(xbc, z, dt, adt, adtT, d_row, g_norm.reshape(1, -1), head_expand)


def kernel(x, c, positions, w_mod, b_mod, g_mix_norm, g_ffn_norm, mla_w_in, mla_g_q, mla_g_kv, mla_w_q_up, mla_w_kv_up, mla_w_out, ssm_w_in, ssm_conv_w, ssm_conv_b, ssm_dt_bias, ssm_a_log, ssm_d, ssm_g_norm, ssm_w_out, moe_w_router, moe_b_router, moe_w_gate_up, moe_b_gate_up, moe_w_down, moe_b_down, g_final):
    bsz, seqlen, d = x.shape
    depth = w_mod.shape[0]
    n = bsz * seqlen
    tm = min(512, seqlen)
    mod = _modulation(c, w_mod, b_mod)
    mod = mod.reshape(depth, 6, bsz, 1, d)
    pos_f = positions.astype(F32).reshape(n, 1)
    xc = x.reshape(n, d)
    pending = None
    for i in range(depth):
        sh1, sc1, gt1, sh2, sc2, gt2 = [mod[i, j] for j in range(6)]
        j = i // 2
        if i % 2 == 0:
            q, k, v = _mla_pre(xc, pos_f, sh1, sc1, g_mix_norm[i], mla_w_in[j], mla_g_q[j], mla_g_kv[j],
                               mla_w_q_up[j], mla_w_kv_up[j], bsz, seqlen, tm)
            o = _attention(q, k, v, tm).reshape(n, MLA_HEADS * V_HEAD)
            w_out = mla_w_out[j]
        else:
            xc, z, xbc, dt, adt, adtT = _ssd_pre(pending, sh1, sc1, g_mix_norm[i], ssm_w_in[j], ssm_conv_w[j],
                                                 ssm_conv_b[j], ssm_dt_bias[j], ssm_a_log[j], bsz, seqlen,
                                                 SSD_TILE)
            o = _ssd_scan(z, xbc, dt, adt, adtT, ssm_d[j], ssm_g_norm[j], bsz, seqlen, min(256, seqlen))
            w_out = ssm_w_out[j]
        x1, h2, meta, gates, counts = _post_mixer(xc, o, w_out, gt1, sh2, sc2, g_ffn_norm[i],
                                                  moe_w_router[i], moe_b_router[i], seqlen, tm)
        defer = i + 1 < depth and (i + 1) % 2 == 1
        out = _moe(x1, h2, meta, gates, counts, gt2, g_final, moe_w_gate_up, moe_b_gate_up,
                   moe_w_down, moe_b_down, i, seqlen, final=(i == depth - 1), defer_combine=defer)
        if defer:
            pending = out
        else:
            xc = out
    return xc.reshape(bsz, seqlen, d)
```

```python
import functools
import math

import jax
import jax.numpy as jnp
from jax import lax
from jax.experimental import pallas as pl
from jax.experimental.pallas import tpu as pltpu

F32 = jnp.float32
BF16 = jnp.bfloat16
I32 = jnp.int32
U32 = jnp.uint32

NORM_EPS = 1e-6
MLA_HEADS = 8
Q_LORA = 256
KV_LORA = 256
QK_NOPE = 128
QK_ROPE = 64
V_HEAD = 128
QK_HEAD = QK_NOPE + QK_ROPE
ROPE_THETA = 10000.0
MLA_SCALE = QK_HEAD ** -0.5
SSM_HEAD_DIM = 64
SSM_GROUPS = 8
SSM_HPG = 4
SSM_HEADS = SSM_GROUPS * SSM_HPG
SSM_STATE = 128
SSM_CONV = 4
SSM_GROUP_W = SSM_HPG * SSM_HEAD_DIM
SSM_INNER = SSM_GROUPS * SSM_GROUP_W
SSM_BC = SSM_GROUPS * SSM_STATE
SSM_CONV_DIM = SSM_INNER + 2 * SSM_BC
N_EXPERTS = 32
TOP_K = 4
SWIGLU_LIMIT = 7.0
SWIGLU_ALPHA = 1.702
MOE_BLOCK = 512

LANES = 128
VMEM_LIMIT = 56 * 1024 * 1024


def _cparams(*sem):
    return pltpu.CompilerParams(dimension_semantics=tuple(sem), vmem_limit_bytes=VMEM_LIMIT)


def _dot(a, b):
    return jnp.dot(a, b, preferred_element_type=F32)


def _dot_nt(a, b):
    return lax.dot_general(a, b, (((1,), (1,)), ((), ())), preferred_element_type=F32)


def _dot_tn(a, b):
    return lax.dot_general(a, b, (((0,), (0,)), ((), ())), preferred_element_type=F32)


def _split3(a):
    hi = a.astype(BF16)
    r1 = a - hi.astype(F32)
    mid = r1.astype(BF16)
    lo = (r1 - mid.astype(F32)).astype(BF16)
    return hi, mid, lo


def _dot_f32ish(a, b):
    ah, am, _ = _split3(a)
    bh, bm, _ = _split3(b)
    return _dot(ah, bh) + (_dot(ah, bm) + _dot(am, bh))


def _pack_bf16_pair(lo, hi):
    lo_b = lax.bitcast_convert_type(lo.astype(BF16).astype(F32), U32)
    hi_b = lax.bitcast_convert_type(hi.astype(BF16).astype(F32), U32)
    return hi_b | lax.shift_right_logical(lo_b, jnp.uint32(16))


def _unpack_bf16_pair(w):
    lo = lax.bitcast_convert_type(lax.shift_left(w, jnp.uint32(16)), F32)
    hi = lax.bitcast_convert_type(w & jnp.uint32(0xFFFF0000), F32)
    return lo, hi


ROW_CHUNKS = 4


def _chunk_copies(buf2d, rows_hbm, r0, nrows, sem, *, to_hbm):
    cps = []
    for c in range(ROW_CHUNKS):
        v = buf2d.at[:, pl.ds(c * LANES, LANES)]
        h = rows_hbm.at[pl.ds(r0, nrows), c]
        cps.append(pltpu.make_async_copy(v, h, sem) if to_hbm else pltpu.make_async_copy(h, v, sem))
    return cps


def _sigmoid(x):
    return 1.0 / (1.0 + jnp.exp(-x))


def _rms(x):
    return x * lax.rsqrt(jnp.mean(x * x, axis=-1, keepdims=True) + NORM_EPS)


def _mod_kernel(c_ref, w_ref, b_ref, o_ref):
    c = c_ref[...]
    cond = c * _sigmoid(c)
    o_ref[...] = _dot_f32ish(cond, w_ref[...]) + b_ref[...]


def _modulation(c, w_mod, b_mod):
    depth, d, d6 = w_mod.shape
    bsz = c.shape[0]
    nj = d6 // d
    return pl.pallas_call(
        _mod_kernel,
        out_shape=jax.ShapeDtypeStruct((depth, nj, bsz, d), F32),
        grid=(depth, nj),
        in_specs=[
            pl.BlockSpec((bsz, d), lambda l, j: (0, 0)),
            pl.BlockSpec((None, d, d), lambda l, j: (l, 0, j)),
            pl.BlockSpec((None, 1, d), lambda l, j: (l, 0, j)),
        ],
        out_specs=pl.BlockSpec((None, None, bsz, d), lambda l, j: (l, j, 0, 0)),
        compiler_params=_cparams("arbitrary", "arbitrary"),
        name="mod",
    )(c, w_mod, b_mod.reshape(depth, 1, d6))


def _mla_pre_kernel(x_ref, pos_ref, sh_ref, sc_ref, g_ref, win_ref, gq_ref, gkv_ref, wq_ref, wkv_ref,
                    invf_ref, q_ref, k_ref, v_ref):
    x = x_ref[...]
    h = _rms(x) * g_ref[...] * (1.0 + sc_ref[...]) + sh_ref[...]
    lat = _dot(h.astype(BF16), win_ref[...])
    q_lat = _rms(lat[:, :Q_LORA]) * gq_ref[...]
    kv_lat = _rms(lat[:, Q_LORA:Q_LORA + KV_LORA]) * gkv_ref[...]
    ang = pos_ref[...] * invf_ref[...]
    cs = jnp.cos(ang)
    sn = jnp.sin(ang)
    o = Q_LORA + KV_LORA
    k_rope = lat[:, o:o + LANES] * cs + lat[:, o + LANES:o + 2 * LANES] * sn
    qq = _dot(q_lat.astype(BF16), wq_ref[...])
    kv = _dot(kv_lat.astype(BF16), wkv_ref[...])
    rot0 = MLA_HEADS * 2 * LANES
    k_rope_b = k_rope[:, :QK_ROPE].astype(BF16)
    for hd in range(MLA_HEADS):
        c0 = hd * 2 * LANES
        q_nope = qq[:, c0:c0 + LANES] * MLA_SCALE
        q_rope = (qq[:, c0 + LANES:c0 + 2 * LANES] * cs
                  + qq[:, rot0 + hd * LANES:rot0 + (hd + 1) * LANES] * sn) * MLA_SCALE
        q_ref[hd, :, 0:QK_NOPE] = q_nope.astype(BF16)
        q_ref[hd, :, QK_NOPE:QK_HEAD] = q_rope[:, :QK_ROPE].astype(BF16)
        k_ref[hd, :, 0:QK_NOPE] = kv[:, c0:c0 + LANES].astype(BF16)
        k_ref[hd, :, QK_NOPE:QK_HEAD] = k_rope_b
        v_ref[hd] = kv[:, c0 + LANES:c0 + 2 * LANES].astype(BF16)


def _rot_half_cols(w):
    half = QK_ROPE // 2
    return jnp.concatenate([-w[..., half:], w[..., :half]], axis=-1)


def _mla_pre(x2d, pos_f, sh, sc, g, w_in, g_q, g_kv, w_q_up, w_kv_up, bsz, seqlen, tm):
    n, d = x2d.shape
    hh = MLA_HEADS
    o = Q_LORA + KV_LORA
    wr = w_in[:, o:o + QK_ROPE]
    zpad = jnp.zeros((d, LANES - QK_ROPE), F32)
    w_in_ext = jnp.concatenate([w_in[:, :o], wr, zpad, _rot_half_cols(wr), zpad], axis=1).astype(BF16)
    wq = w_q_up.reshape(Q_LORA, hh, QK_HEAD)
    zq = jnp.zeros((Q_LORA, hh, LANES - QK_ROPE), F32)
    wq_main = jnp.concatenate([wq, zq], axis=-1).reshape(Q_LORA, hh * 2 * LANES)
    wq_rot = jnp.concatenate([_rot_half_cols(wq[..., QK_NOPE:]), zq], axis=-1).reshape(Q_LORA, hh * LANES)
    wq_ext = jnp.concatenate([wq_main, wq_rot], axis=1).astype(BF16)
    inv_freq = 1.0 / (ROPE_THETA ** (jnp.arange(0, QK_ROPE, 2, dtype=F32) / QK_ROPE))
    invf = jnp.concatenate([inv_freq, inv_freq, jnp.zeros((LANES - QK_ROPE,), F32)]).reshape(1, LANES)
    tpb = seqlen // tm
    vec = lambda i: (0, 0)
    outs = pl.pallas_call(
        _mla_pre_kernel,
        out_shape=(
            jax.ShapeDtypeStruct((bsz, hh, seqlen, QK_HEAD), BF16),
            jax.ShapeDtypeStruct((bsz, hh, seqlen, QK_HEAD), BF16),
            jax.ShapeDtypeStruct((bsz, hh, seqlen, V_HEAD), BF16),
        ),
        grid=(n // tm,),
        in_specs=[
            pl.BlockSpec((tm, d), lambda i: (i, 0)),
            pl.BlockSpec((tm, 1), lambda i: (i, 0)),
            pl.BlockSpec((None, 1, d), lambda i: (i // tpb, 0, 0)),
            pl.BlockSpec((None, 1, d), lambda i: (i // tpb, 0, 0)),
            pl.BlockSpec((1, d), vec),
            pl.BlockSpec(w_in_ext.shape, vec),
            pl.BlockSpec((1, Q_LORA), vec),
            pl.BlockSpec((1, KV_LORA), vec),
            pl.BlockSpec(wq_ext.shape, vec),
            pl.BlockSpec((KV_LORA, hh * 2 * LANES), vec),
            pl.BlockSpec((1, LANES), vec),
        ],
        out_specs=(
            pl.BlockSpec((None, hh, tm, QK_HEAD), lambda i: (i // tpb, 0, i % tpb, 0)),
            pl.BlockSpec((None, hh, tm, QK_HEAD), lambda i: (i // tpb, 0, i % tpb, 0)),
            pl.BlockSpec((None, hh, tm, V_HEAD), lambda i: (i // tpb, 0, i % tpb, 0)),
        ),
        compiler_params=_cparams("arbitrary"),
        name="mla_pre",
    )(x2d, pos_f, sh, sc, g.reshape(1, d), w_in_ext, g_q.reshape(1, -1), g_kv.reshape(1, -1), wq_ext,
      w_kv_up.astype(BF16), invf)
    return outs


ATTN_HEADS_PER_STEP = 4


def _attn_kernel(q_ref, k_ref, v_ref, o_ref, m_scr, acc_scr, *, tq, hp):
    qi = pl.program_id(2)
    m_scr[...] = jnp.full(m_scr.shape, -jnp.inf, F32)
    acc_scr[...] = jnp.zeros(acc_scr.shape, F32)

    def block(hd, r0, tk, masked):
        k = k_ref[hd, pl.ds(r0, tk), :]
        v_ext = jnp.concatenate([v_ref[hd, pl.ds(r0, tk), :], jnp.ones((tk, V_HEAD), BF16)], axis=1)
        s = _dot_nt(q_ref[hd], k)
        if masked:
            row = lax.broadcasted_iota(I32, (tq, tk), 0)
            col = lax.broadcasted_iota(I32, (tq, tk), 1)
            s = jnp.where(row + (tk - tq) >= col, s, -jnp.inf)
        m_prev = m_scr[hd]
        m_new = jnp.maximum(m_prev, jnp.max(s, axis=-1, keepdims=True))
        alpha = jnp.exp(m_prev - m_new)
        p = jnp.exp(s - m_new).astype(BF16)
        acc_scr[hd] = alpha * acc_scr[hd] + _dot(p, v_ext)
        m_scr[hd] = m_new

    def body(j, carry):
        for hd in range(hp):
            block(hd, pl.multiple_of(j * (2 * tq), 2 * tq), 2 * tq, False)
        return carry

    lax.fori_loop(0, qi // 2, body, 0)

    @pl.when(qi % 2 == 1)
    def _():
        for hd in range(hp):
            block(hd, pl.multiple_of((qi - 1) * tq, tq), 2 * tq, True)

    @pl.when(qi % 2 == 0)
    def _():
        for hd in range(hp):
            block(hd, pl.multiple_of(qi * tq, tq), tq, True)

    for hd in range(hp):
        acc = acc_scr[hd]
        o_ref[:, hd * V_HEAD:(hd + 1) * V_HEAD] = (acc[:, :V_HEAD] / acc[:, V_HEAD:]).astype(o_ref.dtype)


def _attention(q, k, v, tq):
    bsz, hh, seqlen, _ = q.shape
    hp = ATTN_HEADS_PER_STEP
    return pl.pallas_call(
        functools.partial(_attn_kernel, tq=tq, hp=hp),
        out_shape=jax.ShapeDtypeStruct((bsz, seqlen, hh * V_HEAD), BF16),
        grid=(bsz, hh // hp, seqlen // tq),
        in_specs=[
            pl.BlockSpec((None, hp, tq, QK_HEAD), lambda b, h, i: (b, h, i, 0)),
            pl.BlockSpec((None, hp, seqlen, QK_HEAD), lambda b, h, i: (b, h, 0, 0)),
            pl.BlockSpec((None, hp, seqlen, V_HEAD), lambda b, h, i: (b, h, 0, 0)),
        ],
        out_specs=pl.BlockSpec((None, tq, hp * V_HEAD), lambda b, h, i: (b, i, h)),
        scratch_shapes=[
            pltpu.VMEM((hp, tq, 1), F32),
            pltpu.VMEM((hp, tq, 2 * V_HEAD), F32),
        ],
        compiler_params=_cparams("arbitrary", "arbitrary", "arbitrary"),
        name="attn",
    )(q, k, v)


def _post_mixer_kernel(x_ref, o_ref, wout_ref, gt1_ref, sh_ref, sc_ref, g_ref, wr_ref, br_ref,
                       x1_ref, h2_hbm, meta_ref, gate_ref, cnt_ref, carry_scr, pbuf, psem, *, tm):
    i = pl.program_id(0)

    @pl.when(i == 0)
    def _():
        carry_scr[...] = jnp.zeros(carry_scr.shape, F32)

    y = _dot(o_ref[...], wout_ref[...])
    x1 = x_ref[...] + gt1_ref[...] * y
    x1_ref[...] = x1
    h2 = _rms(x1) * g_ref[...] * (1.0 + sc_ref[...]) + sh_ref[...]
    half = h2.shape[1] // 2

    @pl.when(i > 0)
    def _():
        for cp in _chunk_copies(pbuf, h2_hbm, (i - 1) * tm, tm, psem, to_hbm=True):
            cp.wait()

    pbuf[...] = _pack_bf16_pair(h2[:, :half], h2[:, half:])
    for cp in _chunk_copies(pbuf, h2_hbm, i * tm, tm, psem, to_hbm=True):
        cp.start()

    logits = _dot_f32ish(h2, wr_ref[...]) + br_ref[...]
    lane = lax.broadcasted_iota(I32, (tm, LANES), 1).astype(F32)
    work = logits
    idxs, vals = [], []
    for _ in range(TOP_K):
        mx = jnp.max(work, axis=-1, keepdims=True)
        idx = jnp.min(jnp.where(work == mx, lane, float(LANES)), axis=-1, keepdims=True)
        idxs.append(idx)
        vals.append(mx)
        work = jnp.where(lane == idx, -jnp.inf, work)
    exps = [jnp.exp(vk - vals[0]) for vk in vals]
    denom = exps[0] + exps[1] + exps[2] + exps[3]
    onehot = jnp.zeros((tm, LANES), F32)
    for idx in idxs:
        onehot = onehot + jnp.where(lane == idx, 1.0, 0.0)
    row = lax.broadcasted_iota(I32, (tm, tm), 0)
    col = lax.broadcasted_iota(I32, (tm, tm), 1)
    ltri = jnp.where(row > col, 1.0, 0.0).astype(BF16)
    cum = _dot(ltri, onehot.astype(BF16)) + carry_scr[...]
    meta = jnp.zeros((tm, LANES), F32)
    gates = jnp.zeros((tm, LANES), F32)
    for kk in range(TOP_K):
        rank = jnp.sum(jnp.where(lane == idxs[kk], cum, 0.0), axis=-1, keepdims=True)
        meta = jnp.where(lane == float(kk), idxs[kk], meta)
        meta = jnp.where(lane == float(TOP_K + kk), rank, meta)
        gates = jnp.where(lane == float(kk), exps[kk] / denom, gates)
    meta_ref[...] = jnp.transpose(meta)[:2 * TOP_K, :].astype(I32)
    gate_ref[...] = gates
    carry = carry_scr[...] + jnp.sum(onehot, axis=0, keepdims=True)
    carry_scr[...] = carry
    cnt_ref[...] = carry.astype(I32)

    @pl.when(i == pl.num_programs(0) - 1)
    def _():
        for cp in _chunk_copies(pbuf, h2_hbm, i * tm, tm, psem, to_hbm=True):
            cp.wait()


def _post_mixer(x2d, o2d, w_out, gt1, sh2, sc2, g_ffn, w_router, b_router, seqlen, tm):
    n, d = x2d.shape
    kdim = o2d.shape[1]
    tpb = seqlen // tm
    wr = jnp.concatenate([w_router, jnp.zeros((d, LANES - N_EXPERTS), F32)], axis=1)
    br = jnp.concatenate([b_router, jnp.full((LANES - N_EXPERTS,), -1e30, F32)]).reshape(1, LANES)
    vec = lambda i: (0, 0)
    bvec = lambda i: (i // tpb, 0, 0)
    return pl.pallas_call(
        functools.partial(_post_mixer_kernel, tm=tm),
        out_shape=(
            jax.ShapeDtypeStruct((n, d), F32),
            jax.ShapeDtypeStruct((n, ROW_CHUNKS, LANES), U32),
            jax.ShapeDtypeStruct((2 * TOP_K, n), I32),
            jax.ShapeDtypeStruct((n, LANES), F32),
            jax.ShapeDtypeStruct((1, LANES), I32),
        ),
        grid=(n // tm,),
        in_specs=[
            pl.BlockSpec((tm, d), lambda i: (i, 0)),
            pl.BlockSpec((tm, kdim), lambda i: (i, 0)),
            pl.BlockSpec((kdim, d), vec),
            pl.BlockSpec((None, 1, d), bvec),
            pl.BlockSpec((None, 1, d), bvec),
            pl.BlockSpec((None, 1, d), bvec),
            pl.BlockSpec((1, d), vec),
            pl.BlockSpec((d, LANES), vec),
            pl.BlockSpec((1, LANES), vec),
        ],
        out_specs=(
            pl.BlockSpec((tm, d), lambda i: (i, 0)),
            pl.BlockSpec(memory_space=pl.ANY),
            pl.BlockSpec((2 * TOP_K, tm), lambda i: (0, i)),
            pl.BlockSpec((tm, LANES), lambda i: (i, 0)),
            pl.BlockSpec((1, LANES), vec),
        ),
        scratch_shapes=[pltpu.VMEM((1, LANES), F32), pltpu.VMEM((tm, d // 2), U32), pltpu.SemaphoreType.DMA],
        compiler_params=_cparams("arbitrary"),
        name="post_mixer",
    )(x2d, o2d, w_out.astype(BF16), gt1, sh2, sc2, g_ffn.reshape(1, d), wr, br)


ROW_UNROLL = 8
MOE_ROW_TILE = 512


def _row_copy(src, s, dst, t, sem):
    return pltpu.make_async_copy(src.at[s], dst.at[t], sem)


def _dest_kernel(pstart_ref, meta_ref, dest_ref):
    e = meta_ref[0:TOP_K, :]
    dest = meta_ref[TOP_K:2 * TOP_K, :]
    for j in range(N_EXPERTS):
        dest = dest + jnp.where(e == j, pstart_ref[j], 0)
    dest_ref[...] = dest


def _dest_rows(meta_t, pstart):
    n = meta_t.shape[1]
    tn = min(4096, n)
    grid_spec = pltpu.PrefetchScalarGridSpec(
        num_scalar_prefetch=1,
        grid=(n // tn,),
        in_specs=[pl.BlockSpec((2 * TOP_K, tn), lambda i, ps: (0, i))],
        out_specs=pl.BlockSpec((TOP_K, tn), lambda i, ps: (0, i)),
    )
    return pl.pallas_call(
        _dest_kernel,
        out_shape=jax.ShapeDtypeStruct((TOP_K, n), I32),
        grid_spec=grid_spec,
        compiler_params=_cparams("arbitrary"),
        name="dest_rows",
    )(pstart, meta_t)


def _dispatch_kernel(pstart_ref, cnt_ref, dest_ref, h_ref, xs_hbm, zbuf, sem, zsem, *, tm, nsteps):
    i = pl.program_id(0)

    @pl.when(i == 0)
    def _():
        zbuf[...] = jnp.zeros(zbuf.shape, U32)

    experts_per_step = -(-N_EXPERTS // nsteps)

    def zero_fill(wait):
        for kx in range(experts_per_step):
            e = i * experts_per_step + kx

            @pl.when(e < N_EXPERTS)
            def _():
                def body(j, c):
                    cp = _row_copy(zbuf, 0, xs_hbm, j, zsem)
                    cp.wait() if wait else cp.start()
                    return c

                lax.fori_loop(pstart_ref[e] + cnt_ref[e], pstart_ref[e + 1], body, 0)

    zero_fill(wait=False)

    def start(tt, c):
        for u in range(ROW_UNROLL):
            t = tt * ROW_UNROLL + u
            for kk in range(TOP_K):
                _row_copy(h_ref, t, xs_hbm, dest_ref[kk, t], sem).start(priority=kk % 2)
        return c

    def wait(tt, c):
        for u in range(ROW_UNROLL * TOP_K):
            _row_copy(h_ref, 0, xs_hbm, 0, sem).wait()
        return c

    lax.fori_loop(0, tm // ROW_UNROLL, start, 0)
    lax.fori_loop(0, tm // ROW_UNROLL, wait, 0)
    zero_fill(wait=True)


def _dispatch(h2p, dest_t, pstart, counts, n_rows, tm):
    n = h2p.shape[0]
    grid_spec = pltpu.PrefetchScalarGridSpec(
        num_scalar_prefetch=2,
        grid=(n // tm,),
        in_specs=[
            pl.BlockSpec((TOP_K, tm), lambda i, ps, cn: (0, i), memory_space=pltpu.SMEM),
            pl.BlockSpec((tm, ROW_CHUNKS, LANES), lambda i, ps, cn: (i, 0, 0)),
        ],
        out_specs=pl.BlockSpec(memory_space=pl.ANY),
        scratch_shapes=[
            pltpu.VMEM((1, ROW_CHUNKS, LANES), U32),
            pltpu.SemaphoreType.DMA,
            pltpu.SemaphoreType.DMA,
        ],
    )
    return pl.pallas_call(
        functools.partial(_dispatch_kernel, tm=tm, nsteps=n // tm),
        out_shape=jax.ShapeDtypeStruct((n_rows, ROW_CHUNKS, LANES), U32),
        grid_spec=grid_spec,
        compiler_params=pltpu.CompilerParams(dimension_semantics=("arbitrary",), vmem_limit_bytes=VMEM_LIMIT,
                                             has_side_effects=True),
        name="dispatch",
    )(pstart, counts, dest_t, h2p)


def _ffn_kernel(be_ref, nu_ref, first_ref, wslot_ref, nxt_ref, xs_hbm, wgu_hbm, bgu_ref, wd_hbm, bd_ref, ys_hbm,
                wgu_b, wd_b, wgu_f, wd_f, xbuf, ybuf, xsem, ysem, wsem, *, ff, layer):
    i = pl.program_id(0)
    nu = nu_ref[0]
    slot = i % 2

    def w_copies(e, s):
        return [pltpu.make_async_copy(wgu_hbm.at[layer, e], wgu_f.at[s], wsem.at[s]),
                pltpu.make_async_copy(wd_hbm.at[layer, e], wd_f.at[s], wsem.at[s])]

    @pl.when(i == 0)
    def _():
        for cp in w_copies(be_ref[0], 0):
            cp.start()

    def x_copies(step, s):
        return _chunk_copies(xbuf.at[s], xs_hbm, step * MOE_BLOCK, MOE_BLOCK, xsem.at[s], to_hbm=False)

    def y_copies(step, s):
        return _chunk_copies(ybuf.at[s], ys_hbm, step * MOE_BLOCK, MOE_BLOCK, ysem.at[s], to_hbm=True)

    @pl.when(i == 0)
    def _():
        for cp in x_copies(0, 0):
            cp.start()

    @pl.when(i + 1 < nu)
    def _():
        for cp in x_copies(i + 1, 1 - slot):
            cp.start()

    @pl.when(first_ref[i] == 1)
    def _():
        ws = wslot_ref[i]
        for cp in w_copies(be_ref[i], ws):
            cp.wait()

        @pl.when(nxt_ref[i] >= 0)
        def _():
            for cp in w_copies(nxt_ref[i], 1 - ws):
                cp.start(priority=1)

        wgu_b[...] = wgu_f[ws].astype(BF16)
        wd_b[...] = wd_f[ws].astype(BF16)

    @pl.when(i < nu)
    def _():
        for cp in x_copies(i, slot):
            cp.wait()
        x_lo, x_hi = _unpack_bf16_pair(xbuf[slot])
        dh = x_lo.shape[1]
        gu = (_dot(x_lo.astype(BF16), wgu_b[0:dh, :]) + _dot(x_hi.astype(BF16), wgu_b[dh:2 * dh, :])
              + bgu_ref[...])
        g = jnp.minimum(gu[:, :ff], SWIGLU_LIMIT)
        lin = jnp.clip(gu[:, ff:], -SWIGLU_LIMIT, SWIGLU_LIMIT)
        act = g * _sigmoid(SWIGLU_ALPHA * g) * (lin + 1.0)
        y = _dot(act.astype(BF16), wd_b[...]) + bd_ref[...]

        @pl.when(i >= 2)
        def _():
            for cp in y_copies(i - 2, slot):
                cp.wait()

        ybuf[slot] = _pack_bf16_pair(y[:, :dh], y[:, dh:])
        for cp in y_copies(i, slot):
            cp.start()

    @pl.when(i == nu - 1)
    def _():
        for cp in y_copies(i, slot):
            cp.wait()

        @pl.when(i >= 1)
        def _():
            for cp in y_copies(i - 1, 1 - slot):
                cp.wait()


def _ffn(xs, blk_e, n_used, pad_end, w_gate_up, b_gate_up, w_down, b_down, layer):
    n_rows = xs.shape[0]
    depth, ne, d, ff2 = w_gate_up.shape
    ff = ff2 // 2
    n_blk = n_rows // MOE_BLOCK
    blk = jnp.arange(n_blk, dtype=I32)
    active = blk < n_used[0]
    first = (active & ((blk == 0) | (blk_e != jnp.roll(blk_e, 1)))).astype(I32)
    wslot = ((jnp.cumsum(first) - 1) % 2).astype(I32)
    nxt_blk = pad_end[blk_e] // MOE_BLOCK
    nxt = jnp.where(nxt_blk < n_used[0], blk_e[jnp.minimum(nxt_blk, n_blk - 1)], -1).astype(I32)
    row_buf = pltpu.VMEM((2, MOE_BLOCK, ROW_CHUNKS * LANES), U32)
    sp = lambda i, be, nu, fi, ws, nx: (layer, be[i], 0, 0)
    grid_spec = pltpu.PrefetchScalarGridSpec(
        num_scalar_prefetch=5,
        grid=(n_blk,),
        in_specs=[
            pl.BlockSpec(memory_space=pl.ANY),
            pl.BlockSpec(memory_space=pl.ANY),
            pl.BlockSpec((None, None, 1, ff2), sp),
            pl.BlockSpec(memory_space=pl.ANY),
            pl.BlockSpec((None, None, 1, d), sp),
        ],
        out_specs=pl.BlockSpec(memory_space=pl.ANY),
        scratch_shapes=[pltpu.VMEM((d, ff2), BF16), pltpu.VMEM((ff, d), BF16),
                        pltpu.VMEM((2, d, ff2), F32), pltpu.VMEM((2, ff, d), F32), row_buf, row_buf,
                        pltpu.SemaphoreType.DMA((2,)), pltpu.SemaphoreType.DMA((2,)),
                        pltpu.SemaphoreType.DMA((2,))],
    )
    return pl.pallas_call(
        functools.partial(_ffn_kernel, ff=ff, layer=layer),
        out_shape=jax.ShapeDtypeStruct((n_rows, ROW_CHUNKS, LANES), U32),
        grid_spec=grid_spec,
        compiler_params=_cparams("arbitrary"),
        name="ffn",
    )(blk_e, n_used, first, wslot, nxt, xs, w_gate_up, b_gate_up.reshape(depth, ne, 1, ff2), w_down,
      b_down.reshape(depth, ne, 1, d))


def _gather_scratch(tm):
    return [pltpu.VMEM((2, TOP_K, tm // ROW_UNROLL, ROW_CHUNKS, ROW_UNROLL, LANES), U32),
            pltpu.SemaphoreType.DMA((2,))]


def _gather_start(dest_ref, ys_hbm, buf, sem, tm, both_queues):
    def body(tt, c):
        for u in range(ROW_UNROLL):
            for kk in range(TOP_K):
                src = ys_hbm.at[dest_ref[kk, tt * ROW_UNROLL + u]]
                pltpu.make_async_copy(src, buf.at[kk, tt, :, u], sem).start(priority=kk % 2 if both_queues else 1)
        return c

    lax.fori_loop(0, tm // ROW_UNROLL, body, 0)


def _gather_next_unrolled(dest_next_ref, ys_hbm, buf, sem, tm):
    nslot = 1 - pl.program_id(0) % 2
    for tt in range(tm // ROW_UNROLL):
        for u in range(ROW_UNROLL):
            for kk in range(TOP_K):
                src = ys_hbm.at[dest_next_ref[kk, tt * ROW_UNROLL + u]]
                pltpu.make_async_copy(src, buf.at[nslot, kk, tt, :, u], sem.at[nslot]).start(priority=1)


def _gather_drain_last(ys_hbm, buf, sem, tm):
    i = pl.program_id(0)

    @pl.when(i == pl.num_programs(0) - 1)
    def _():
        nslot = 1 - i % 2
        _gather_wait(ys_hbm, buf.at[nslot], sem.at[nslot], tm)


def _gather_wait(ys_hbm, buf, sem, tm):
    def body(tt, c):
        for u in range(ROW_UNROLL * TOP_K):
            pltpu.make_async_copy(ys_hbm.at[0], buf.at[0, 0, :, 0], sem).wait()
        return c

    lax.fori_loop(0, tm // ROW_UNROLL, body, 0)


def _combine_rows(dest0_ref, dest_next_ref, gate_ref, ys_hbm, buf, sem, tm, both_queues):
    i = pl.program_id(0)
    slot = i % 2

    @pl.when(i == 0)
    def _():
        _gather_start(dest0_ref, ys_hbm, buf.at[0], sem.at[0], tm, both_queues)

    if dest_next_ref is not None:
        @pl.when(i + 1 < pl.num_programs(0))
        def _():
            _gather_start(dest_next_ref, ys_hbm, buf.at[1 - slot], sem.at[1 - slot], tm, both_queues)

    _gather_wait(ys_hbm, buf.at[slot], sem.at[slot], tm)
    gates = gate_ref[...]
    acc_lo = acc_hi = None
    for kk in range(TOP_K):
        rows = jnp.concatenate([buf[slot, kk, :, c, :, :].reshape(tm, LANES) for c in range(ROW_CHUNKS)], axis=1)
        lo, hi = _unpack_bf16_pair(rows)
        gk = gates[:, kk:kk + 1]
        acc_lo = gk * lo if acc_lo is None else acc_lo + gk * lo
        acc_hi = gk * hi if acc_hi is None else acc_hi + gk * hi
    return jnp.concatenate([acc_lo, acc_hi], axis=1)


def _combine_specs(tm, nsteps, idx):
    return [pl.BlockSpec((TOP_K, tm), lambda *a: (0, 0), memory_space=pltpu.SMEM),
            pl.BlockSpec((TOP_K, tm), lambda *a: (0, jnp.minimum(idx(*a) + 1, nsteps - 1)), memory_space=pltpu.SMEM)]


def _combine_kernel(dest0_ref, dest_next_ref, x1_ref, gate_ref, gt2_ref, gfin_ref, ys_hbm, o_ref, buf, sem,
                    *, tm, final):
    y = _combine_rows(dest0_ref, dest_next_ref, gate_ref, ys_hbm, buf, sem, tm, both_queues=True)
    x2 = x1_ref[...] + gt2_ref[...] * y
    if final:
        x2 = _rms(x2) * gfin_ref[...]
    o_ref[...] = x2


def _combine(x1, gates, gt2, g_final, ys, dest_t, seqlen, tm, final):
    n, d = x1.shape
    tpb = seqlen // tm
    grid_spec = pltpu.PrefetchScalarGridSpec(
        num_scalar_prefetch=0,
        grid=(n // tm,),
        in_specs=_combine_specs(tm, n // tm, lambda i: i) + [
            pl.BlockSpec((tm, d), lambda i: (i, 0)),
            pl.BlockSpec((tm, LANES), lambda i: (i, 0)),
            pl.BlockSpec((None, 1, d), lambda i: (i // tpb, 0, 0)),
            pl.BlockSpec((1, d), lambda i: (0, 0)),
            pl.BlockSpec(memory_space=pl.ANY),
        ],
        out_specs=pl.BlockSpec((tm, d), lambda i: (i, 0)),
        scratch_shapes=_gather_scratch(tm),
    )
    return pl.pallas_call(
        functools.partial(_combine_kernel, tm=tm, final=final),
        out_shape=jax.ShapeDtypeStruct((n, d), F32),
        grid_spec=grid_spec,
        compiler_params=_cparams("arbitrary"),
        name="combine",
    )(dest_t, dest_t, x1, gates, gt2, g_final.reshape(1, d), ys)


def _moe(x1, h2p, meta_t, gates, counts, gt2, g_final, w_gate_up, b_gate_up, w_down, b_down, layer, seqlen,
         final, defer_combine):
    n, d = x1.shape
    n_pair = n * TOP_K
    n_rows = -(-n_pair // MOE_BLOCK) * MOE_BLOCK + N_EXPERTS * MOE_BLOCK
    n_blk = n_rows // MOE_BLOCK
    cnt = counts[0, :N_EXPERTS]
    padded = (cnt + MOE_BLOCK - 1) // MOE_BLOCK * MOE_BLOCK
    pad_end = jnp.cumsum(padded)
    pstart = jnp.concatenate([pad_end - padded, pad_end[-1:]]).astype(I32)
    n_used = (pad_end[-1:] // MOE_BLOCK).astype(I32)
    blk_start = jnp.arange(n_blk, dtype=I32) * MOE_BLOCK
    blk_e = jnp.minimum(jnp.sum(blk_start[:, None] >= pad_end[None, :], axis=1), N_EXPERTS - 1).astype(I32)
    tm = min(MOE_ROW_TILE, seqlen)
    dest_t = _dest_rows(meta_t, pstart)
    xs = _dispatch(h2p, dest_t, pstart, cnt, n_rows, tm)
    ys = _ffn(xs, blk_e, n_used, pad_end.astype(I32), w_gate_up, b_gate_up, w_down, b_down, layer)
    if defer_combine:
        return x1, gates, gt2, ys, dest_t
    return _combine(x1, gates, gt2, g_final, ys, dest_t, seqlen, tm, final)


def _softplus(x):
    return jnp.maximum(x, 0.0) + jnp.log(1.0 + jnp.exp(-jnp.abs(x)))


SSD_TILE = 256


def _ssd_pre_kernel(dest0_ref, dest_next_ref, x1_ref, gate_ref, gt2_ref, ys_hbm,
                    sh_ref, sc_ref, g_ref, wz_ref, wx_ref, wdt_ref, wdtT_ref, cw_ref, cb_ref,
                    dtb_ref, dtbT_ref, a_ref, aT_ref,
                    x_out_ref, z_ref, xbc_ref, dt_ref, adt_ref, adtT_ref, ubuf, gbuf, gsem, *, tm, tpb, cchunk):
    i = pl.program_id(0)
    x = x1_ref[...] + gt2_ref[...] * _combine_rows(dest0_ref, None, gate_ref, ys_hbm, gbuf, gsem, tm,
                                                     both_queues=False)
    x_out_ref[...] = x
    _gather_next_unrolled(dest_next_ref, ys_hbm, gbuf, gsem, tm)
    h = _rms(x) * g_ref[...] * (1.0 + sc_ref[...]) + sh_ref[...]
    hb = h.astype(BF16)
    z_ref[...] = _dot(hb, wz_ref[...]).astype(BF16)
    dt = _softplus(_dot(hb, wdt_ref[...]) + dtb_ref[...])
    dt_ref[...] = dt
    adt_ref[...] = dt * a_ref[...]
    dtT = _softplus(_dot_nt(wdtT_ref[...], hb) + dtbT_ref[...])
    adtT_ref[...] = dtT * aT_ref[...]

    @pl.when(i % tpb == 0)
    def _():
        ubuf[...] = jnp.zeros(ubuf.shape, F32)

    for c0 in range(0, SSM_CONV_DIM, cchunk):
        cols = slice(c0, c0 + cchunk)
        u = _dot(hb, wx_ref[:, cols])
        ext = jnp.concatenate([ubuf[:, cols], u], axis=0)
        acc = cb_ref[:, cols] + cw_ref[SSM_CONV - 1:SSM_CONV, cols] * u
        for j in range(1, SSM_CONV):
            shifted = pltpu.roll(ext, j, axis=0)[8:, :]
            acc = acc + cw_ref[SSM_CONV - 1 - j:SSM_CONV - j, cols] * shifted
        xbc_ref[:, cols] = (acc * _sigmoid(acc)).astype(BF16)
        ubuf[:, cols] = u[tm - 8:, :]
    _gather_drain_last(ys_hbm, gbuf, gsem, tm)


def _ssd_pre(moe_out, sh, sc, g, w_in, conv_w, conv_b, dt_bias, a_log, bsz, seqlen, tm):
    x1, gates, gt2, ys, dest_t = moe_out
    n, d = x1.shape
    tm = min(tm, seqlen)
    tpb = seqlen // tm
    nh = SSM_HEADS
    wz = w_in[:, :SSM_INNER].astype(BF16)
    wx = w_in[:, SSM_INNER:SSM_INNER + SSM_CONV_DIM].astype(BF16)
    wdt_raw = w_in[:, SSM_INNER + SSM_CONV_DIM:]
    wdt = jnp.concatenate([wdt_raw, jnp.zeros((d, LANES - nh), F32)], axis=1).astype(BF16)
    wdtT = wdt_raw.T.astype(BF16)
    pad = jnp.zeros((LANES - nh,), F32)
    dtb = jnp.concatenate([dt_bias, pad]).reshape(1, LANES)
    a_neg = -jnp.exp(a_log.astype(F32))
    a_row = jnp.concatenate([a_neg, pad]).reshape(1, LANES)
    vec = lambda i: (0, 0)
    bvec = lambda i: (i // tpb, 0, 0)
    return pl.pallas_call(
        functools.partial(_ssd_pre_kernel, tm=tm, tpb=tpb, cchunk=1024),
        out_shape=(
            jax.ShapeDtypeStruct((n, d), F32),
            jax.ShapeDtypeStruct((n, SSM_INNER), BF16),
            jax.ShapeDtypeStruct((n, SSM_CONV_DIM), BF16),
            jax.ShapeDtypeStruct((n, LANES), F32),
            jax.ShapeDtypeStruct((n, LANES), F32),
            jax.ShapeDtypeStruct((bsz, nh, seqlen), F32),
        ),
        grid=(n // tm,),
        in_specs=_combine_specs(tm, n // tm, lambda i: i) + [
            pl.BlockSpec((tm, d), lambda i: (i, 0)),
            pl.BlockSpec((tm, LANES), lambda i: (i, 0)),
            pl.BlockSpec((None, 1, d), bvec),
            pl.BlockSpec(memory_space=pl.ANY),
            pl.BlockSpec((None, 1, d), bvec),
            pl.BlockSpec((None, 1, d), bvec),
            pl.BlockSpec((1, d), vec),
            pl.BlockSpec((d, SSM_INNER), vec),
            pl.BlockSpec((d, SSM_CONV_DIM), vec),
            pl.BlockSpec((d, LANES), vec),
            pl.BlockSpec((nh, d), vec),
            pl.BlockSpec((SSM_CONV, SSM_CONV_DIM), vec),
            pl.BlockSpec((1, SSM_CONV_DIM), vec),
            pl.BlockSpec((1, LANES), vec),
            pl.BlockSpec((nh, 1), vec),
            pl.BlockSpec((1, LANES), vec),
            pl.BlockSpec((nh, 1), vec),
        ],
        out_specs=(
            pl.BlockSpec((tm, d), lambda i: (i, 0)),
            pl.BlockSpec((tm, SSM_INNER), lambda i: (i, 0)),
            pl.BlockSpec((tm, SSM_CONV_DIM), lambda i: (i, 0)),
            pl.BlockSpec((tm, LANES), lambda i: (i, 0)),
            pl.BlockSpec((tm, LANES), lambda i: (i, 0)),
            pl.BlockSpec((None, nh, tm), lambda i: (i // tpb, 0, i % tpb)),
        ),
        scratch_shapes=[pltpu.VMEM((8, SSM_CONV_DIM), F32)] + _gather_scratch(tm),
        compiler_params=_cparams("arbitrary"),
        name="ssd_pre",
    )(dest_t, dest_t, x1, gates, gt2, ys, sh, sc, g.reshape(1, d), wz, wx, wdt, wdtT, conv_w,
      conv_b.reshape(1, -1), dtb, dt_bias.reshape(nh, 1), a_row, a_neg.reshape(nh, 1))


def _expand_heads(v, g, rows):
    lane = lax.broadcasted_iota(I32, (rows, LANES), 1)
    lo = lane < SSM_HEAD_DIM
    parts = []
    for j in range(0, SSM_HPG, 2):
        h0 = g * SSM_HPG + j
        parts.append(jnp.where(lo, v[:, h0:h0 + 1], v[:, h0 + 1:h0 + 2]))
    return jnp.concatenate(parts, axis=1)


def _ssd_scan_kernel(xbc_ref, z_ref, dt_ref, adt_ref, adtT_ref, d_ref, gn_ref, hx_ref, yn_ref, state, *, lc):
    c = pl.program_id(1)

    @pl.when(c == 0)
    def _():
        state[...] = jnp.zeros(state.shape, F32)

    row = lax.broadcasted_iota(I32, (lc, lc), 0)
    col = lax.broadcasted_iota(I32, (lc, lc), 1)
    causal = row >= col
    tri = jnp.where(causal, 1.0, 0.0).astype(BF16)
    triT = jnp.where(row <= col, 1.0, 0.0).astype(BF16)
    ah, am, al = _split3(adt_ref[...])
    a_cum = _dot(tri, ah) + (_dot(tri, am) + _dot(tri, al))
    th, tmid, tl = _split3(adtT_ref[...])
    a_cumT = _dot(th, triT) + (_dot(tmid, triT) + _dot(tl, triT))
    dt = dt_ref[...]
    a_last = a_cum[lc - 1:lc, :]
    e_cum = jnp.exp(a_cum)
    d2e = jnp.exp(a_last - a_cum)
    cdec = jnp.exp(a_last)
    dskip = d_ref[...]
    hx = hx_ref[...]
    dt_x = _dot(dt.astype(BF16), hx)
    dtd2e_x = _dot((dt * d2e).astype(BF16), hx)
    ecum_x = _dot(e_cum.astype(BF16), hx)
    lane2 = lax.broadcasted_iota(I32, (lc, SSM_GROUP_W), 1)
    for g in range(SSM_GROUPS):
        xg = xbc_ref[:, g * SSM_GROUP_W:(g + 1) * SSM_GROUP_W].astype(F32)
        b0 = SSM_INNER + g * SSM_STATE
        c0 = SSM_INNER + SSM_BC + g * SSM_STATE
        bg = xbc_ref[:, b0:b0 + SSM_STATE]
        cg = xbc_ref[:, c0:c0 + SSM_STATE]
        gcols = slice(g * SSM_GROUP_W, (g + 1) * SSM_GROUP_W)
        xdt_b = (xg * dt_x[:, gcols]).astype(BF16)
        cb = _dot_nt(cg, bg)
        y = jnp.zeros((lc, SSM_GROUP_W), F32)
        for j in range(SSM_HPG):
            hd = g * SSM_HPG + j
            seg = a_cum[:, hd:hd + 1] - a_cumT[hd:hd + 1, :]
            decay = jnp.exp(jnp.where(causal, seg, -jnp.inf))
            m = (cb * decay).astype(BF16)
            in_head = (lane2 >= j * SSM_HEAD_DIM) & (lane2 < (j + 1) * SSM_HEAD_DIM)
            y = y + _dot(m, jnp.where(in_head, xdt_b, jnp.zeros_like(xdt_b)))
        st = state[g]
        y = y + _dot(cg, st.astype(BF16)) * ecum_x[:, gcols]
        xd2e = (xg * dtd2e_x[:, gcols]).astype(BF16)
        state[g] = st * _expand_heads(cdec, g, 1) + _dot_tn(bg, xd2e)
        y = y + _expand_heads(dskip, g, 1) * xg
        zg = z_ref[:, g * SSM_GROUP_W:(g + 1) * SSM_GROUP_W].astype(F32)
        yz = y * (zg * _sigmoid(zg))
        yn = _rms(yz) * gn_ref[:, g * SSM_GROUP_W:(g + 1) * SSM_GROUP_W]
        yn_ref[:, g * SSM_GROUP_W:(g + 1) * SSM_GROUP_W] = yn.astype(BF16)


def _ssd_scan(z, xbc, dt, adt, adtT, d_skip, g_norm, bsz, seqlen, lc):
    n = z.shape[0]
    nc = seqlen // lc
    pad = jnp.zeros((LANES - SSM_HEADS,), F32)
    d_row = jnp.concatenate([d_skip, pad]).reshape(1, LANES)
    rows = lambda b, c: (b * nc + c, 0)
    vec = lambda b, c: (0, 0)
    head_of_col = jnp.arange(SSM_INNER, dtype=I32) // SSM_HEAD_DIM
    head_expand = (jnp.arange(LANES, dtype=I32)[:, None] == head_of_col[None, :]).astype(BF16)
    return pl.pallas_call(
        functools.partial(_ssd_scan_kernel, lc=lc),
        out_shape=jax.ShapeDtypeStruct((n, SSM_INNER), BF16),
        grid=(bsz, nc),
        in_specs=[
            pl.BlockSpec((lc, SSM_CONV_DIM), rows),
            pl.BlockSpec((lc, SSM_INNER), rows),
            pl.BlockSpec((lc, LANES), rows),
            pl.BlockSpec((lc, LANES), rows),
            pl.BlockSpec((None, SSM_HEADS, lc), lambda b, c: (b, 0, c)),
            pl.BlockSpec((1, LANES), vec),
            pl.BlockSpec((1, SSM_INNER), vec),
            pl.BlockSpec((LANES, SSM_INNER), vec),
        ],
        out_specs=pl.BlockSpec((lc, SSM_INNER), rows),
        scratch_shapes=[pltpu.VMEM((SSM_GROUPS, SSM_STATE, SSM_GROUP_W), F32)],
        compiler_params=_cparams("arbitrary", "arbitrary"),
        name="ssd_scan",
    )(xbc, z, dt, adt, adtT, d_row, g_norm.reshape(1, -1), head_expand)


def kernel(x, c, positions, w_mod, b_mod, g_mix_norm, g_ffn_norm, mla_w_in, mla_g_q, mla_g_kv, mla_w_q_up, mla_w_kv_up, mla_w_out, ssm_w_in, ssm_conv_w, ssm_conv_b, ssm_dt_bias, ssm_a_log, ssm_d, ssm_g_norm, ssm_w_out, moe_w_router, moe_b_router, moe_w_gate_up, moe_b_gate_up, moe_w_down, moe_b_down, g_final):
    bsz, seqlen, d = x.shape
    depth = w_mod.shape[0]
    n = bsz * seqlen
    tm = min(512, seqlen)
    mod = _modulation(c, w_mod, b_mod)
    mod = mod.reshape(depth, 6, bsz, 1, d)
    pos_f = positions.astype(F32).reshape(n, 1)
    xc = x.reshape(n, d)
    pending = None
    for i in range(depth):
        sh1, sc1, gt1, sh2, sc2, gt2 = [mod[i, j] for j in range(6)]
        j = i // 2
        if i % 2 == 0:
            q, k, v = _mla_pre(xc, pos_f, sh1, sc1, g_mix_norm[i], mla_w_in[j], mla_g_q[j], mla_g_kv[j],
                               mla_w_q_up[j], mla_w_kv_up[j], bsz, seqlen, tm)
            o = _attention(q, k, v, tm).reshape(n, MLA_HEADS * V_HEAD)
            w_out = mla_w_out[j]
        else:
            xc, z, xbc, dt, adt, adtT = _ssd_pre(pending, sh1, sc1, g_mix_norm[i], ssm_w_in[j], ssm_conv_w[j],
                                                 ssm_conv_b[j], ssm_dt_bias[j], ssm_a_log[j], bsz, seqlen,
                                                 SSD_TILE)
            o = _ssd_scan(z, xbc, dt, adt, adtT, ssm_d[j], ssm_g_norm[j], bsz, seqlen, min(256, seqlen))
            w_out = ssm_w_out[j]
        x1, h2, meta, gates, counts = _post_mixer(xc, o, w_out, gt1, sh2, sc2, g_ffn_norm[i],
                                                  moe_w_router[i], moe_b_router[i], seqlen, tm)
        defer = i + 1 < depth and (i + 1) % 2 == 1
        out = _moe(x1, h2, meta, gates, counts, gt2, g_final, moe_w_gate_up, moe_b_gate_up,
                   moe_w_down, moe_b_down, i, seqlen, final=(i == depth - 1), defer_combine=defer)
        if defer:
            pending = out
        else:
            xc = out
    return xc.reshape(bsz, seqlen, d)
```
